```python
import jax, jax.numpy as jnp
from jax import lax
import numpy as np

D_MODEL = 1024
BATCH = 8
SEQ = 4096
DEPTH = 4

BLOCK = 128
EPS = 1e-6
ROPE_BASE = 10000.0

MLA_HEADS = 8
MLA_NOPE = 64
MLA_ROPE = 32
MLA_V = 64
MLA_Q_RANK = 384
MLA_KV_RANK = 256
SB_HEADS = 8
SB_DIM = 64
RET_HEADS = 8
RET_DK = 64
RET_DV = 64

BRANCH_WIDTH = 512
N_BRANCH = 3
D_FF = 4 * D_MODEL

IN_SIZES = (MLA_Q_RANK, MLA_KV_RANK, MLA_ROPE,
            SB_HEADS * SB_DIM, SB_HEADS * SB_DIM, SB_HEADS * SB_DIM,
            RET_HEADS * RET_DK, RET_HEADS * RET_DK, RET_HEADS * RET_DV, RET_HEADS * RET_DV,
            N_BRANCH * D_MODEL)
D_IN = sum(IN_SIZES)

kernel_name = "hybrid_mla_stickbreak_retention_gated"


def rmsnorm(x, g):
    xf = x.astype(jnp.float32)
    y = xf * lax.rsqrt(jnp.mean(xf * xf, axis=-1, keepdims=True) + EPS)
    return (y * g.astype(jnp.float32)).astype(x.dtype)


def head_group_norm(y, g):
    yf = y.astype(jnp.float32)
    mu = jnp.mean(yf, axis=-1, keepdims=True)
    var = jnp.mean(jnp.square(yf - mu), axis=-1, keepdims=True)
    yn = ((yf - mu) * lax.rsqrt(var + EPS)).reshape(y.shape[0], y.shape[1], -1)
    return (yn * g.astype(jnp.float32)).astype(y.dtype)


def rope_tables(positions, dim):
    inv_freq = ROPE_BASE ** (-jnp.arange(0, dim, 2, dtype=jnp.float32) / dim)
    ang = positions.astype(jnp.float32)[..., None] * inv_freq
    return jnp.cos(ang)[:, :, None, :], jnp.sin(ang)[:, :, None, :]


def apply_rope(x, cos, sin):
    half = x.shape[-1] // 2
    xf = x.astype(jnp.float32)
    x1, x2 = xf[..., :half], xf[..., half:]
    return jnp.concatenate([x1 * cos - x2 * sin, x2 * cos + x1 * sin], axis=-1).astype(x.dtype)


def causal_softmax_attention(q, k, v, scale):
    S = q.shape[1]
    outs = []
    for start in range(0, S, BLOCK):
        end = start + BLOCK
        s = jnp.einsum('bqhd,bkhd->bhqk', q[:, start:end], k[:, :end]).astype(jnp.float32) * scale
        t_idx = start + jnp.arange(BLOCK)[:, None]
        s_idx = jnp.arange(end)[None, :]
        s = jnp.where(s_idx <= t_idx, s, -jnp.inf)
        p = jax.nn.softmax(s, axis=-1)
        outs.append(jnp.einsum('bhqk,bkhd->bqhd', p.astype(v.dtype), v[:, :end]))
    return jnp.concatenate(outs, axis=1)


def stick_breaking_attention(q, k, v):
    S = q.shape[1]
    scale = SB_DIM ** -0.5
    outs = []
    for start in range(0, S, BLOCK):
        end = start + BLOCK
        z = jnp.einsum('bqhd,bkhd->bhqk', q[:, start:end], k[:, :end]).astype(jnp.float32) * scale
        t_idx = start + jnp.arange(BLOCK)[:, None]
        s_idx = jnp.arange(end)[None, :]
        before = s_idx < t_idx
        log_not = jnp.where(before, jax.nn.log_sigmoid(-z), 0.0)
        between = lax.cumsum(log_not, axis=3, reverse=True) - log_not
        a = jnp.where(before, jnp.exp(jax.nn.log_sigmoid(z) + between), 0.0)
        outs.append(jnp.einsum('bhqk,bkhd->bqhd', a.astype(v.dtype), v[:, :end]))
    return jnp.concatenate(outs, axis=1)


def retention(q, k, v, log_gamma):
    B, S, H, dk = q.shape
    dv = v.shape[-1]
    C = BLOCK
    N = S // C
    qc = q.astype(jnp.float32).reshape(B, N, C, H, dk)
    kc = (k.astype(jnp.float32) * dk ** -0.5).reshape(B, N, C, H, dk)
    vc = v.astype(jnp.float32).reshape(B, N, C, H, dv)
    idx = jnp.arange(C, dtype=jnp.float32)
    diff = idx[:, None] - idx[None, :]
    intra_decay = jnp.where(diff[None] >= 0,
                            jnp.exp(jnp.maximum(diff, 0.0)[None] * log_gamma[:, None, None]), 0.0)
    scores = jnp.einsum('bnchd,bnmhd->bnhcm', qc, kc) * intra_decay
    y_intra = jnp.einsum('bnhcm,bnmhe->bnche', scores, vc)
    zeta = jnp.exp((C - 1 - idx)[:, None] * log_gamma[None, :])
    kv = jnp.einsum('bnchd,bnche->nbhde', kc * zeta[None, None, :, :, None], vc)
    chunk_decay = jnp.exp(C * log_gamma)[None, :, None, None]

    def step(state, kv_n):
        return state * chunk_decay + kv_n, state

    _, prev = lax.scan(step, jnp.zeros((B, H, dk, dv), jnp.float32), kv)
    xi = jnp.exp((idx + 1.0)[:, None] * log_gamma[None, :])
    y_cross = jnp.einsum('bnchd,nbhde->bnche', qc, prev) * xi[None, None, :, :, None]
    return (y_intra + y_cross).reshape(B, S, H, dv).astype(v.dtype)


def setup_inputs(seed: int = 0) -> dict:
    key = jax.random.key(seed)
    ks = jax.random.split(key, 16)
    f32 = jnp.float32

    def dense(k, shape, fan_in):
        return jax.random.normal(k, shape, f32) * fan_in ** -0.5

    def gain(k, shape):
        return 1.0 + 0.02 * jax.random.normal(k, shape, f32)

    x = jax.random.normal(ks[0], (BATCH, SEQ, D_MODEL), f32)
    offset = jax.random.randint(ks[1], (BATCH, 1), 0, 1024, dtype=jnp.int32)
    positions = offset + jnp.arange(SEQ, dtype=jnp.int32)[None, :]
    return {
        "x": x,
        "positions": positions,
        "norm_mix_g": gain(ks[2], (DEPTH, D_MODEL)),
        "w_in": dense(ks[3], (DEPTH, D_MODEL, D_IN), D_MODEL),
        "mla_q_norm_g": gain(ks[4], (DEPTH, MLA_Q_RANK)),
        "mla_w_uq": dense(ks[5], (DEPTH, MLA_Q_RANK, MLA_HEADS * (MLA_NOPE + MLA_ROPE)), MLA_Q_RANK),
        "mla_kv_norm_g": gain(ks[6], (DEPTH, MLA_KV_RANK)),
        "mla_w_ukv": dense(ks[7], (DEPTH, MLA_KV_RANK, MLA_HEADS * (MLA_NOPE + MLA_V)), MLA_KV_RANK),
        "ret_norm_g": gain(ks[8], (DEPTH, RET_HEADS * RET_DV)),
        "w_branch": dense(ks[9], (DEPTH, N_BRANCH, BRANCH_WIDTH, D_MODEL), BRANCH_WIDTH),
        "w_out": dense(ks[10], (DEPTH, D_MODEL, D_MODEL), D_MODEL),
        "norm_mlp_g": gain(ks[11], (DEPTH, D_MODEL)),
        "w_up": dense(ks[12], (DEPTH, D_MODEL, D_FF), D_MODEL),
        "w_down": dense(ks[13], (DEPTH, D_FF, D_MODEL), D_FF),
        "final_norm_g": gain(ks[14], (D_MODEL,)),
    }


def reference(x, positions, norm_mix_g, w_in, mla_q_norm_g, mla_w_uq, mla_kv_norm_g, mla_w_ukv,
              ret_norm_g, w_branch, w_out, norm_mlp_g, w_up, w_down, final_norm_g):
    B, S, _ = x.shape
    cos_m, sin_m = rope_tables(positions, MLA_ROPE)
    cos_r, sin_r = rope_tables(positions, RET_DK)
    log_gamma = jnp.log1p(-jnp.exp2(-5.0 - jnp.arange(RET_HEADS, dtype=jnp.float32)))
    splits = [int(v) for v in np.cumsum(IN_SIZES)[:-1]]

    for l in range(DEPTH):
        h = rmsnorm(x, norm_mix_g[l])
        proj = h @ w_in[l]
        (c_q, c_kv, k_pe, sb_q, sb_k, sb_v, r_q, r_k, r_v, r_g, gate_logits) = jnp.split(proj, splits, axis=-1)

        q = (rmsnorm(c_q, mla_q_norm_g[l]) @ mla_w_uq[l]).reshape(B, S, MLA_HEADS, MLA_NOPE + MLA_ROPE)
        kv = (rmsnorm(c_kv, mla_kv_norm_g[l]) @ mla_w_ukv[l]).reshape(B, S, MLA_HEADS, MLA_NOPE + MLA_V)
        k_rot = jnp.broadcast_to(apply_rope(k_pe[:, :, None, :], cos_m, sin_m), (B, S, MLA_HEADS, MLA_ROPE))
        q_a = jnp.concatenate([q[..., :MLA_NOPE], apply_rope(q[..., MLA_NOPE:], cos_m, sin_m)], axis=-1)
        k_a = jnp.concatenate([kv[..., :MLA_NOPE], k_rot], axis=-1)
        y_a = causal_softmax_attention(q_a, k_a, kv[..., MLA_NOPE:], (MLA_NOPE + MLA_ROPE) ** -0.5).reshape(B, S, -1)

        y_b = stick_breaking_attention(sb_q.reshape(B, S, SB_HEADS, SB_DIM),
                                       sb_k.reshape(B, S, SB_HEADS, SB_DIM),
                                       sb_v.reshape(B, S, SB_HEADS, SB_DIM)).reshape(B, S, -1)

        y_c = retention(apply_rope(r_q.reshape(B, S, RET_HEADS, RET_DK), cos_r, sin_r),
                        apply_rope(r_k.reshape(B, S, RET_HEADS, RET_DK), cos_r, sin_r),
                        r_v.reshape(B, S, RET_HEADS, RET_DV), log_gamma)
        y_c = jax.nn.silu(r_g) * head_group_norm(y_c, ret_norm_g[l])

        branches = jnp.stack([y_a, y_b, y_c], axis=2)
        up = jnp.einsum('bsnw,nwd->bsnd', branches, w_branch[l])
        gates = jax.nn.sigmoid(gate_logits.reshape(B, S, N_BRANCH, D_MODEL))
        x = x + jnp.einsum('bsnd,bsnd->bsd', gates, up) @ w_out[l]

        h = rmsnorm(x, norm_mlp_g[l])
        x = x + jnp.square(jax.nn.relu(h @ w_up[l])) @ w_down[l]

    return rmsnorm(x, final_norm_g)
```

```python
import functools

import numpy as np
import jax
import jax.numpy as jnp
from jax import lax
from jax.experimental import pallas as pl
from jax.experimental.pallas import tpu as pltpu

F32 = jnp.float32
BF16 = jnp.bfloat16

D_MODEL = 1024
EPS = 1e-6
ROPE_BASE = 10000.0
CHUNK = 128

MLA_HEADS = 8
MLA_NOPE = 64
MLA_ROPE = 32
MLA_V = 64
MLA_Q_RANK = 384
MLA_KV_RANK = 256
SB_HEADS = 8
SB_DIM = 64
RET_HEADS = 8
RET_DK = 64
RET_DV = 64
BRANCH_WIDTH = 512
N_BRANCH = 3
D_FF = 4 * D_MODEL

LANES = 128
LAT_WIDTH = MLA_Q_RANK + MLA_KV_RANK + 2 * LANES
VMEM_LIMIT = 56 * 1024 * 1024

NT_DIMS = (((1,), (1,)), ((), ()))
TN_DIMS = (((0,), (0,)), ((), ()))


def _dot(a, b):
    return jnp.dot(a, b, preferred_element_type=F32)


def _dot_nt(a, b):
    return lax.dot_general(a, b, NT_DIMS, preferred_element_type=F32)


def _dot_hilo(x, m_bf16):
    hi = x.astype(BF16)
    lo = (x - hi.astype(F32)).astype(BF16)
    return _dot(hi, m_bf16) + _dot(lo, m_bf16)


def _rep_lanes(x, n):
    return x if n == 1 else jnp.concatenate([x] * n, axis=1)


def _rms(x, g):
    return x * lax.rsqrt(jnp.mean(x * x, axis=-1, keepdims=True) + EPS) * g


def _params(sem):
    return pltpu.CompilerParams(dimension_semantics=sem, vmem_limit_bytes=VMEM_LIMIT)


def _norm_matmul_kernel(x_ref, g_ref, w_ref, o_ref, h_ref):
    @pl.when(pl.program_id(1) == 0)
    def _():
        h_ref[...] = _rms(x_ref[...], g_ref[...]).astype(BF16)

    o_ref[...] = _dot(h_ref[...], w_ref[...]).astype(o_ref.dtype)


def _norm_matmul(x2d, g, w, out_dtype, tm, tn):
    t, d = x2d.shape
    n = w.shape[1]
    return pl.pallas_call(
        _norm_matmul_kernel,
        grid=(t // tm, n // tn),
        in_specs=[pl.BlockSpec((tm, d), lambda i, j: (i, 0)),
                  pl.BlockSpec((1, d), lambda i, j: (0, 0)),
                  pl.BlockSpec((d, tn), lambda i, j: (0, j))],
        out_specs=pl.BlockSpec((tm, tn), lambda i, j: (i, j)),
        out_shape=jax.ShapeDtypeStruct((t, n), out_dtype),
        scratch_shapes=[pltpu.VMEM((tm, d), BF16)],
        compiler_params=_params(("parallel", "arbitrary")),
        name="norm_matmul",
    )(x2d, g, w)


def _mla_prep_kernel(lat_ref, cq_ref, sq_ref, gq_ref, gkv_ref, wqa_ref, wqb_ref, wk_ref, wv_ref,
                     q_ref, k_ref, v_ref):
    lat = lat_ref[...]
    c_q = lat[:, :MLA_Q_RANK]
    c_kv = lat[:, MLA_Q_RANK:MLA_Q_RANK + MLA_KV_RANK]
    kpe_a = lat[:, MLA_Q_RANK + MLA_KV_RANK:MLA_Q_RANK + MLA_KV_RANK + LANES]
    kpe_b = lat[:, MLA_Q_RANK + MLA_KV_RANK + LANES:]
    cos_t = cq_ref[...]
    sin_t = sq_ref[...]
    nq = _rms(c_q, gq_ref[...]).astype(BF16)
    nkv = _rms(c_kv, gkv_ref[...]).astype(BF16)
    q = (_dot(nq, wqa_ref[...]) * _rep_lanes(cos_t, MLA_HEADS)
         + _dot(nq, wqb_ref[...]) * _rep_lanes(sin_t, MLA_HEADS))
    q_ref[...] = q.astype(BF16)
    k_rot = kpe_a * cos_t + kpe_b * sin_t
    k_ref[...] = (_dot(nkv, wk_ref[...]) + _rep_lanes(k_rot, MLA_HEADS)).astype(BF16)
    v_ref[...] = _dot(nkv, wv_ref[...]).astype(BF16)


def _mla_prep(lat, cos_t, sin_t, gq, gkv, wqa, wqb, wk, wv, tm):
    t = lat.shape[0]
    hw = MLA_HEADS * LANES
    full = lambda a: pl.BlockSpec(a.shape, lambda i: (0,) * a.ndim)
    row = lambda w: pl.BlockSpec((tm, w), lambda i: (i, 0))
    return pl.pallas_call(
        _mla_prep_kernel,
        grid=(t // tm,),
        in_specs=[row(LAT_WIDTH), row(LANES), row(LANES), full(gq), full(gkv),
                  full(wqa), full(wqb), full(wk), full(wv)],
        out_specs=[row(hw), row(hw), row(MLA_HEADS * MLA_V)],
        out_shape=[jax.ShapeDtypeStruct((t, hw), BF16), jax.ShapeDtypeStruct((t, hw), BF16),
                   jax.ShapeDtypeStruct((t, MLA_HEADS * MLA_V), BF16)],
        compiler_params=_params(("parallel",)),
        name="mla_prep",
    )(lat, cos_t, sin_t, gq, gkv, wqa, wqb, wk, wv)


def _mla_attn_kernel(q_ref, k_ref, v_ref, o_ref, m_ref, l_ref, acc_ref, *, tq, scale):
    i = pl.program_id(2)
    row = lax.broadcasted_iota(jnp.int32, (tq, tq), 0)
    col = lax.broadcasted_iota(jnp.int32, (tq, tq), 1)
    causal = col <= row
    nrep = tq // LANES

    def tile(q, e, j, diag):
        ks = pl.multiple_of(j * tq, tq)
        k = k_ref[0, pl.ds(ks, tq), e * LANES:(e + 1) * LANES]
        v = v_ref[0, pl.ds(ks, tq), :]
        s = _dot_nt(q, k) * scale
        if diag:
            s = jnp.where(causal, s, -jnp.inf)
        m_prev = m_ref[...]
        m_next = jnp.maximum(m_prev, jnp.max(s, axis=1, keepdims=True))
        alpha = jnp.exp(m_prev - m_next)
        p = jnp.exp(s - _rep_lanes(m_next, nrep))
        l_ref[...] = alpha * l_ref[...] + jnp.sum(p, axis=1, keepdims=True)
        acc_ref[...] = alpha * acc_ref[...] + _dot(p.astype(BF16), v)
        m_ref[...] = m_next

    outs = []
    for e in range(2):
        q = q_ref[0, :, e * LANES:(e + 1) * LANES]
        m_ref[...] = jnp.full(m_ref.shape, -jnp.inf, F32)
        l_ref[...] = jnp.zeros(l_ref.shape, F32)
        acc_ref[...] = jnp.zeros(acc_ref.shape, F32)
        tile(q, e, i, True)

        def body(j, carry, q=q, e=e):
            tile(q, e, j, False)
            return carry

        lax.fori_loop(0, i, body, 0)
        outs.append(acc_ref[...] / l_ref[...])
    lane = lax.broadcasted_iota(jnp.int32, (tq, LANES), 1)
    o_ref[0] = jnp.where(lane < MLA_V, outs[0], outs[1]).astype(o_ref.dtype)


def _mla_attn(q, k, v, tq):
    b, s, _ = q.shape
    scale = float((MLA_NOPE + MLA_ROPE) ** -0.5)
    return pl.pallas_call(
        functools.partial(_mla_attn_kernel, tq=tq, scale=scale),
        grid=(b, MLA_HEADS // 2, s // tq),
        in_specs=[pl.BlockSpec((1, tq, 2 * LANES), lambda bi, p, i: (bi, i, p)),
                  pl.BlockSpec((1, s, 2 * LANES), lambda bi, p, i: (bi, 0, p)),
                  pl.BlockSpec((1, s, LANES), lambda bi, p, i: (bi, 0, p))],
        out_specs=pl.BlockSpec((1, tq, LANES), lambda bi, p, i: (bi, i, p)),
        out_shape=jax.ShapeDtypeStruct((b, s, MLA_HEADS * MLA_V), BF16),
        scratch_shapes=[pltpu.VMEM((tq, LANES), F32)] * 3,
        compiler_params=_params(("parallel", "parallel", "arbitrary")),
        name="mla_attn",
    )(q, k, v)


def _sb_attn_kernel(q_ref, k_ref, v_ref, u_ref, o_ref, c_ref, acc_ref, *, tq):
    i = pl.program_id(2)
    row = lax.broadcasted_iota(jnp.int32, (tq, tq), 0)
    col = lax.broadcasted_iota(jnp.int32, (tq, tq), 1)
    before = col < row
    lane = lax.broadcasted_iota(jnp.int32, (tq, LANES), 1)
    nrep = tq // LANES
    q_pair = q_ref[0]
    u = u_ref[...]

    def tile(qm, j, diag):
        ks = pl.multiple_of(j * tq, tq)
        k = k_ref[0, pl.ds(ks, tq), :]
        v = v_ref[0, pl.ds(ks, tq), :]
        z = _dot_nt(qm, k)
        log_beta = jnp.minimum(z, 0.0) - jnp.log1p(jnp.exp(-jnp.abs(z)))
        log_not = log_beta - z
        if diag:
            log_not = jnp.where(before, log_not, 0.0)
        between = _dot_hilo(log_not, u) + _rep_lanes(c_ref[...], nrep)
        a = jnp.exp(log_beta + between)
        if diag:
            a = jnp.where(before, a, 0.0)
        acc_ref[...] += _dot(a.astype(BF16), v)
        c_ref[...] += jnp.sum(log_not, axis=1, keepdims=True)

    outs = []
    for e in range(2):
        qm = jnp.where((lane >= e * SB_DIM) & (lane < (e + 1) * SB_DIM), q_pair, jnp.zeros_like(q_pair))
        c_ref[...] = jnp.zeros(c_ref.shape, F32)
        acc_ref[...] = jnp.zeros(acc_ref.shape, F32)
        tile(qm, i, True)

        def body(jj, carry, qm=qm):
            tile(qm, i - 1 - jj, False)
            return carry

        lax.fori_loop(0, i, body, 0)
        outs.append(acc_ref[...])
    o_ref[0] = jnp.where(lane < SB_DIM, outs[0], outs[1]).astype(o_ref.dtype)


def _sb_attn(proj_bf, tq):
    b, s, _ = proj_bf.shape
    npair = SB_HEADS // 2
    idx = np.arange(tq)
    u = jnp.asarray((idx[:, None] > idx[None, :]).astype(np.float32), dtype=BF16)
    return pl.pallas_call(
        functools.partial(_sb_attn_kernel, tq=tq),
        grid=(b, npair, s // tq),
        in_specs=[pl.BlockSpec((1, tq, LANES), lambda bi, p, i: (bi, i, p)),
                  pl.BlockSpec((1, s, LANES), lambda bi, p, i: (bi, 0, npair + p)),
                  pl.BlockSpec((1, s, LANES), lambda bi, p, i: (bi, 0, 2 * npair + p)),
                  pl.BlockSpec((tq, tq), lambda bi, p, i: (0, 0))],
        out_specs=pl.BlockSpec((1, tq, LANES), lambda bi, p, i: (bi, i, p)),
        out_shape=jax.ShapeDtypeStruct((b, s, SB_HEADS * SB_DIM), BF16),
        scratch_shapes=[pltpu.VMEM((tq, LANES), F32)] * 2,
        compiler_params=_params(("parallel", "parallel", "arbitrary")),
        name="sb_attn",
    )(proj_bf, proj_bf, proj_bf, u)


def _retention_kernel(q_ref, k_ref, g_ref, v_ref, cos_ref, sin_ref, dec_ref, zeta_ref, xi_ref,
                      cd_ref, bd_ref, avg_ref, gn_ref, o_ref, state_ref):
    @pl.when(pl.program_id(1) == 0)
    def _():
        state_ref[...] = jnp.zeros(state_ref.shape, F32)

    width = RET_HEADS * RET_DK
    npair = RET_HEADS // 2
    half = RET_DK // 2
    lane_w = lax.broadcasted_iota(jnp.int32, (CHUNK, width), 1)
    first_half = (lane_w % RET_DK) < half
    cos_t = _rep_lanes(cos_ref[0], width // LANES)
    sin_t = _rep_lanes(sin_ref[0], width // LANES)

    def rope(x):
        swapped = jnp.where(first_half, pltpu.roll(x, width - half, 1), pltpu.roll(x, half, 1))
        return x * cos_t + swapped * sin_t

    q = rope(q_ref[0])
    k = rope(k_ref[0])
    kz = (k * zeta_ref[...]).astype(BF16)
    qb = q.astype(BF16)
    kb = k.astype(BF16)
    v = v_ref[0]
    lane = lax.broadcasted_iota(jnp.int32, (CHUNK, LANES), 1)
    ys = []
    for p in range(npair):
        sl = slice(p * LANES, (p + 1) * LANES)
        qp, kp, vp = qb[:, sl], kb[:, sl], v[:, sl]
        intra = []
        for e in range(2):
            qm = jnp.where((lane >= e * RET_DK) & (lane < (e + 1) * RET_DK), qp, jnp.zeros_like(qp))
            sc = _dot_nt(qm, kp) * dec_ref[2 * p + e]
            intra.append(_dot(sc.astype(BF16), vp))
        y_intra = jnp.where(lane < RET_DV, intra[0], intra[1])
        state = state_ref[p]
        y_cross = _dot(qp, state.astype(BF16)) * xi_ref[:, sl]
        kv = lax.dot_general(kz[:, sl], vp, TN_DIMS, preferred_element_type=F32)
        state_ref[p] = state * cd_ref[p] + kv * bd_ref[...]
        ys.append(y_intra + y_cross)
    y = jnp.concatenate(ys, axis=1)
    avg = avg_ref[...]
    d = y - _dot_hilo(y, avg)
    var = _dot_hilo(d * d, avg)
    yn = d * lax.rsqrt(var + EPS) * gn_ref[...]
    g = g_ref[0]
    o_ref[0] = (g * jax.nn.sigmoid(g) * yn).astype(o_ref.dtype)


def _retention_consts():
    h = np.arange(RET_HEADS, dtype=np.float64)
    log_gamma = np.log1p(-np.exp2(-5.0 - h))
    idx = np.arange(CHUNK, dtype=np.float64)
    diff = idx[:, None] - idx[None, :]
    dec = np.where(diff[None] >= 0, np.exp(np.maximum(diff, 0.0)[None] * log_gamma[:, None, None]), 0.0)
    zeta = np.exp((CHUNK - 1 - idx)[:, None] * log_gamma[None, :])
    xi = np.exp((idx + 1.0)[:, None] * log_gamma[None, :])
    zeta_l = np.repeat(zeta, RET_DK, axis=1)
    xi_l = np.repeat(xi, RET_DV, axis=1)
    chunk_decay = np.exp(CHUNK * log_gamma)
    npair = RET_HEADS // 2
    head_of = np.arange(LANES) // RET_DK
    bd = (head_of[:, None] == head_of[None, :]).astype(np.float64)
    cd = np.stack([bd * chunk_decay[2 * p + head_of][:, None] for p in range(npair)])
    gh = np.arange(RET_HEADS * RET_DV) // RET_DV
    avg = (gh[:, None] == gh[None, :]).astype(np.float64) / RET_DV
    f = lambda a: jnp.asarray(a, dtype=F32)
    return f(dec), f(zeta_l), f(xi_l), f(cd), f(bd), jnp.asarray(avg, dtype=BF16)


def _retention(proj_ret, proj_bf, cos_t, sin_t, gn):
    b, s, _ = proj_ret.shape
    width = RET_HEADS * RET_DK
    dec, zeta_l, xi_l, cd, bd, avg = _retention_consts()
    full = lambda a: pl.BlockSpec(a.shape, lambda bi, n: (0,) * a.ndim)
    blk = lambda c: pl.BlockSpec((1, CHUNK, width), lambda bi, n: (bi, n, c))
    tab = pl.BlockSpec((1, CHUNK, LANES), lambda bi, n: (bi, n, 0))
    return pl.pallas_call(
        _retention_kernel,
        grid=(b, s // CHUNK),
        in_specs=[blk(0), blk(1), blk(2), blk(3), tab, tab,
                  full(dec), full(zeta_l), full(xi_l), full(cd), full(bd), full(avg), full(gn)],
        out_specs=blk(0),
        out_shape=jax.ShapeDtypeStruct((b, s, RET_HEADS * RET_DV), BF16),
        scratch_shapes=[pltpu.VMEM((RET_HEADS // 2, LANES, LANES), F32)],
        compiler_params=_params(("parallel", "arbitrary")),
        name="retention",
    )(proj_ret, proj_ret, proj_ret, proj_bf, cos_t, sin_t, dec, zeta_l, xi_l, cd, bd, avg, gn)


def _merge_kernel(x_ref, ya_ref, yb_ref, yc_ref, gl_ref, wb_ref, wo_ref, o_ref):
    merged = None
    for n, y_ref in enumerate((ya_ref, yb_ref, yc_ref)):
        gate = jax.nn.sigmoid(gl_ref[:, n * D_MODEL:(n + 1) * D_MODEL])
        term = gate * _dot(y_ref[...], wb_ref[n])
        merged = term if merged is None else merged + term
    o_ref[...] = x_ref[...] + _dot(merged.astype(BF16), wo_ref[...])


def _merge(x2d, ya, yb, yc, gates, wb, wo, tm):
    t = x2d.shape[0]
    full = lambda a: pl.BlockSpec(a.shape, lambda i: (0,) * a.ndim)
    row = lambda w: pl.BlockSpec((tm, w), lambda i: (i, 0))
    return pl.pallas_call(
        _merge_kernel,
        grid=(t // tm,),
        in_specs=[row(D_MODEL), row(BRANCH_WIDTH), row(BRANCH_WIDTH), row(BRANCH_WIDTH),
                  row(N_BRANCH * D_MODEL), full(wb), full(wo)],
        out_specs=row(D_MODEL),
        out_shape=jax.ShapeDtypeStruct((t, D_MODEL), F32),
        compiler_params=_params(("parallel",)),
        name="merge",
    )(x2d, ya, yb, yc, gates, wb, wo)


def _mlp_kernel(x_ref, g_ref, wu_ref, wd_ref, gf_ref, o_ref, h_ref, acc_ref, *, final_norm):
    j = pl.program_id(1)

    @pl.when(j == 0)
    def _():
        h_ref[...] = _rms(x_ref[...], g_ref[...]).astype(BF16)
        acc_ref[...] = jnp.zeros(acc_ref.shape, F32)

    u = jnp.maximum(_dot(h_ref[...], wu_ref[...]), 0.0)
    acc_ref[...] += _dot((u * u).astype(BF16), wd_ref[...])

    @pl.when(j == pl.num_programs(1) - 1)
    def _():
        y = x_ref[...] + acc_ref[...]
        o_ref[...] = _rms(y, gf_ref[...]) if final_norm else y


def _mlp(x2d, g, wu, wd, gf, final_norm, tm, tf):
    t, d = x2d.shape
    vec = pl.BlockSpec((1, d), lambda i, j: (0, 0))
    return pl.pallas_call(
        functools.partial(_mlp_kernel, final_norm=final_norm),
        grid=(t // tm, D_FF // tf),
        in_specs=[pl.BlockSpec((tm, d), lambda i, j: (i, 0)), vec,
                  pl.BlockSpec((d, tf), lambda i, j: (0, j)),
                  pl.BlockSpec((tf, d), lambda i, j: (j, 0)), vec],
        out_specs=pl.BlockSpec((tm, d), lambda i, j: (i, 0)),
        out_shape=jax.ShapeDtypeStruct((t, d), F32),
        scratch_shapes=[pltpu.VMEM((tm, d), BF16), pltpu.VMEM((tm, d), F32)],
        compiler_params=_params(("parallel", "arbitrary")),
        name="mlp",
    )(x2d, g, wu, wd, gf)


def _pick(n, pref):
    t = min(pref, n)
    while n % t:
        t //= 2
    return t


def _rope_tables(positions):
    pos = positions.astype(F32)[..., None]

    def cs(dim):
        inv_freq = ROPE_BASE ** (-jnp.arange(0, dim, 2, dtype=F32) / dim)
        ang = pos * inv_freq
        return jnp.cos(ang), jnp.sin(ang)

    b, s = positions.shape
    cm, sm = cs(MLA_ROPE)
    ones = jnp.ones((b, s, MLA_NOPE), F32)
    z_nope = jnp.zeros((b, s, MLA_NOPE), F32)
    z_pad = jnp.zeros((b, s, LANES - MLA_NOPE - MLA_ROPE), F32)
    mla_cos = jnp.concatenate([ones, cm, cm, z_pad], axis=-1).reshape(b * s, LANES)
    mla_sin = jnp.concatenate([z_nope, -sm, sm, z_pad], axis=-1).reshape(b * s, LANES)
    cr, sr = cs(RET_DK)
    ret_cos = jnp.concatenate([cr, cr] * (LANES // RET_DK), axis=-1)
    ret_sin = jnp.concatenate([-sr, sr] * (LANES // RET_DK), axis=-1)
    return mla_cos, mla_sin, ret_cos, ret_sin


def _prep_weights(w_in, mla_w_uq, mla_w_ukv, w_branch, w_out, w_up, w_down):
    depth, d, _ = w_in.shape
    offs = np.cumsum((0, MLA_Q_RANK, MLA_KV_RANK, MLA_ROPE) + (BRANCH_WIDTH,) * 7 + (N_BRANCH * D_MODEL,))
    seg = lambda n: w_in[:, :, int(offs[n]):int(offs[n + 1])]
    c_q, c_kv, k_pe, sb_q, sb_k, sb_v, r_q, r_k, r_v, r_g, gates = (seg(n) for n in range(11))
    zeros = lambda *shape: jnp.zeros(shape, w_in.dtype)
    hr = MLA_ROPE // 2
    pad = LANES - MLA_NOPE - MLA_ROPE
    w_lat = jnp.concatenate(
        [c_q, c_kv,
         zeros(depth, d, MLA_NOPE), k_pe, zeros(depth, d, pad),
         zeros(depth, d, MLA_NOPE), k_pe[..., hr:], k_pe[..., :hr], zeros(depth, d, pad)], axis=-1)
    w_ret = jnp.concatenate([r_q, r_k * (RET_DK ** -0.5), r_g], axis=-1)
    w_bf = jnp.concatenate([sb_q * (SB_DIM ** -0.5), sb_k, sb_v, r_v], axis=-1)

    uq = mla_w_uq.reshape(depth, MLA_Q_RANK, MLA_HEADS, MLA_NOPE + MLA_ROPE)
    zq = lambda w: jnp.zeros((depth, MLA_Q_RANK, MLA_HEADS, w), uq.dtype)
    nope, rope = uq[..., :MLA_NOPE], uq[..., MLA_NOPE:]
    wqa = jnp.concatenate([nope, rope, zq(pad)], axis=-1)
    wqb = jnp.concatenate([zq(MLA_NOPE), rope[..., hr:], rope[..., :hr], zq(pad)], axis=-1)
    ukv = mla_w_ukv.reshape(depth, MLA_KV_RANK, MLA_HEADS, MLA_NOPE + MLA_V)
    wk = jnp.concatenate([ukv[..., :MLA_NOPE],
                          jnp.zeros((depth, MLA_KV_RANK, MLA_HEADS, LANES - MLA_NOPE), ukv.dtype)], axis=-1)
    wv = ukv[..., MLA_NOPE:]
    flat = lambda w: w.reshape(depth, w.shape[1], -1).astype(BF16)
    bf = lambda w: w.astype(BF16)
    return dict(w_lat=bf(w_lat), w_ret=bf(w_ret), w_gate=bf(gates), w_bf=bf(w_bf),
                wqa=flat(wqa), wqb=flat(wqb), wk=flat(wk), wv=flat(wv),
                wb=bf(w_branch), wo=bf(w_out), wu=bf(w_up), wd=bf(w_down))


def kernel(x, positions, norm_mix_g, w_in, mla_q_norm_g, mla_w_uq, mla_kv_norm_g, mla_w_ukv, ret_norm_g, w_branch, w_out, norm_mlp_g, w_up, w_down, final_norm_g):
    b, s, d = x.shape
    depth = w_in.shape[0]
    t = b * s
    assert d == D_MODEL and s % CHUNK == 0
    tm_proj = _pick(t, 1024)
    tm_row = _pick(t, 512)
    t_attn = _pick(s, 256)
    w = _prep_weights(w_in, mla_w_uq, mla_w_ukv, w_branch, w_out, w_up, w_down)
    mla_cos, mla_sin, ret_cos, ret_sin = _rope_tables(positions)
    vec = lambda a: a.reshape(1, -1)

    x2d = x.reshape(t, d)
    for l in range(depth):
        g_mix = vec(norm_mix_g[l])
        lat = _norm_matmul(x2d, g_mix, w["w_lat"][l], F32, tm_proj, LAT_WIDTH)
        proj_ret = _norm_matmul(x2d, g_mix, w["w_ret"][l], F32, tm_proj, 512)
        gates = _norm_matmul(x2d, g_mix, w["w_gate"][l], F32, tm_proj, 512)
        proj_bf = _norm_matmul(x2d, g_mix, w["w_bf"][l], BF16, tm_proj, 512)
        proj_bf = proj_bf.reshape(b, s, -1)

        q_a, k_a, v_a = _mla_prep(lat, mla_cos, mla_sin, vec(mla_q_norm_g[l]), vec(mla_kv_norm_g[l]),
                                  w["wqa"][l], w["wqb"][l], w["wk"][l], w["wv"][l], tm_row)
        y_a = _mla_attn(q_a.reshape(b, s, -1), k_a.reshape(b, s, -1), v_a.reshape(b, s, -1), t_attn)
        y_b = _sb_attn(proj_bf, t_attn)
        y_c = _retention(proj_ret.reshape(b, s, -1), proj_bf, ret_cos, ret_sin, vec(ret_norm_g[l]))

        x2d = _merge(x2d, y_a.reshape(t, -1), y_b.reshape(t, -1), y_c.reshape(t, -1), gates,
                     w["wb"][l], w["wo"][l], tm_row)
        x2d = _mlp(x2d, vec(norm_mlp_g[l]), w["wu"][l], w["wd"][l], vec(final_norm_g),
                   l == depth - 1, tm_proj, 1024)
    return x2d.reshape(b, s, d)
```

```python
import functools

import numpy as np
import jax
import jax.numpy as jnp
from jax import lax
from jax.experimental import pallas as pl
from jax.experimental.pallas import tpu as pltpu

F32 = jnp.float32
BF16 = jnp.bfloat16

D_MODEL = 1024
EPS = 1e-6
ROPE_BASE = 10000.0
CHUNK = 128

MLA_HEADS = 8
MLA_NOPE = 64
MLA_ROPE = 32
MLA_V = 64
MLA_Q_RANK = 384
MLA_KV_RANK = 256
SB_HEADS = 8
SB_DIM = 64
RET_HEADS = 8
RET_DK = 64
RET_DV = 64
BRANCH_WIDTH = 512
N_BRANCH = 3
D_FF = 4 * D_MODEL

LANES = 128
LAT_WIDTH = MLA_Q_RANK + MLA_KV_RANK + 2 * LANES
VMEM_LIMIT = 56 * 1024 * 1024
LOG2E = 1.4426950408889634
MLA_Q_SCALE = (MLA_NOPE + MLA_ROPE) ** -0.5 * LOG2E
SB_SUB = 256
SB_SKIP_LOG2 = -160.0

NT_DIMS = (((1,), (1,)), ((), ()))
TN_DIMS = (((0,), (0,)), ((), ()))


def _dot(a, b):
    return jnp.dot(a, b, preferred_element_type=F32)


def _dot_nt(a, b):
    return lax.dot_general(a, b, NT_DIMS, preferred_element_type=F32)


def _dot_hilo(x, m_bf16):
    hi = x.astype(BF16)
    lo = (x - hi.astype(F32)).astype(BF16)
    return _dot(hi, m_bf16) + _dot(lo, m_bf16)


def _rep_lanes(x, n):
    return x if n == 1 else jnp.concatenate([x] * n, axis=1)


def _rms(x, g):
    return x * lax.rsqrt(jnp.mean(x * x, axis=-1, keepdims=True) + EPS) * g


def _params(sem):
    return pltpu.CompilerParams(dimension_semantics=sem, vmem_limit_bytes=VMEM_LIMIT)


IN_GROUPS = ((LAT_WIDTH, F32),
             (3 * BRANCH_WIDTH, F32),
             (N_BRANCH * D_MODEL, F32),
             (4 * BRANCH_WIDTH, BF16))
IN_CHUNK = 512


def _in_proj_kernel(x_ref, g_ref, w_ref, *o_refs):
    h = _rms(x_ref[...], g_ref[...]).astype(BF16)
    col = 0
    for o_ref, (width, _) in zip(o_refs, IN_GROUPS):
        for c0 in range(0, width, IN_CHUNK):
            c1 = min(c0 + IN_CHUNK, width)
            o_ref[:, c0:c1] = _dot(h, w_ref[:, col + c0:col + c1]).astype(o_ref.dtype)
        col += width


def _in_proj(x2d, g, w, tm):
    t, d = x2d.shape
    return pl.pallas_call(
        _in_proj_kernel,
        grid=(t // tm,),
        in_specs=[pl.BlockSpec((tm, d), lambda i: (i, 0)),
                  pl.BlockSpec((1, d), lambda i: (0, 0)),
                  pl.BlockSpec(w.shape, lambda i: (0, 0), pipeline_mode=pl.Buffered(1))],
        out_specs=[pl.BlockSpec((tm, width), lambda i: (i, 0)) for width, _ in IN_GROUPS],
        out_shape=[jax.ShapeDtypeStruct((t, width), dt) for width, dt in IN_GROUPS],
        compiler_params=_params(("parallel",)),
        name="in_proj",
    )(x2d, g, w)


def _mla_prep_kernel(lat_ref, cq_ref, sq_ref, gq_ref, gkv_ref, wqa_ref, wqb_ref, wk_ref, wv_ref,
                     q_ref, k_ref, v_ref):
    lat = lat_ref[...]
    c_q = lat[:, :MLA_Q_RANK]
    c_kv = lat[:, MLA_Q_RANK:MLA_Q_RANK + MLA_KV_RANK]
    kpe_a = lat[:, MLA_Q_RANK + MLA_KV_RANK:MLA_Q_RANK + MLA_KV_RANK + LANES]
    kpe_b = lat[:, MLA_Q_RANK + MLA_KV_RANK + LANES:]
    cos_t = cq_ref[...]
    sin_t = sq_ref[...]
    nq = _rms(c_q, gq_ref[...]).astype(BF16)
    nkv = _rms(c_kv, gkv_ref[...]).astype(BF16)
    q = (_dot(nq, wqa_ref[...]) * _rep_lanes(cos_t, MLA_HEADS)
         + _dot(nq, wqb_ref[...]) * _rep_lanes(sin_t, MLA_HEADS))
    q_ref[...] = (q * MLA_Q_SCALE).astype(BF16)
    k_rot = kpe_a * cos_t + kpe_b * sin_t
    k_ref[...] = (_dot(nkv, wk_ref[...]) + _rep_lanes(k_rot, MLA_HEADS)).astype(BF16)
    v_ref[...] = _dot(nkv, wv_ref[...]).astype(BF16)


def _mla_prep(lat, cos_t, sin_t, gq, gkv, wqa, wqb, wk, wv, tm):
    t = lat.shape[0]
    hw = MLA_HEADS * LANES
    full = lambda a: pl.BlockSpec(a.shape, lambda i: (0,) * a.ndim)
    row = lambda w: pl.BlockSpec((tm, w), lambda i: (i, 0))
    return pl.pallas_call(
        _mla_prep_kernel,
        grid=(t // tm,),
        in_specs=[row(LAT_WIDTH), row(LANES), row(LANES), full(gq), full(gkv),
                  full(wqa), full(wqb), full(wk), full(wv)],
        out_specs=[row(hw), row(hw), row(MLA_HEADS * MLA_V)],
        out_shape=[jax.ShapeDtypeStruct((t, hw), BF16), jax.ShapeDtypeStruct((t, hw), BF16),
                   jax.ShapeDtypeStruct((t, MLA_HEADS * MLA_V), BF16)],
        compiler_params=_params(("parallel",)),
        name="mla_prep",
    )(lat, cos_t, sin_t, gq, gkv, wqa, wqb, wk, wv)


def _mla_attn_kernel(q_ref, k_ref, v_ref, o_ref, m_ref, l_ref, acc_ref, *, tq):
    i = pl.program_id(2)
    nrep = tq // LANES

    def tile(e, j, diag):
        ks = pl.multiple_of(j * tq, tq)
        q = q_ref[0, :, e * LANES:(e + 1) * LANES]
        k = k_ref[0, pl.ds(ks, tq), e * LANES:(e + 1) * LANES]
        v = v_ref[0, pl.ds(ks, tq), :]
        s = _dot_nt(q, k)
        if diag:
            row = lax.broadcasted_iota(jnp.int32, (tq, tq), 0)
            col = lax.broadcasted_iota(jnp.int32, (tq, tq), 1)
            s = jnp.where(col <= row, s, -jnp.inf)
        m_prev = m_ref[e]
        m_next = jnp.maximum(m_prev, jnp.max(s, axis=1, keepdims=True))
        alpha = jnp.exp2(m_prev - m_next)
        p = jnp.exp2(s - _rep_lanes(m_next, nrep))
        l_ref[e] = alpha * l_ref[e] + jnp.sum(p, axis=1, keepdims=True)
        acc_ref[e] = alpha * acc_ref[e] + _dot(p.astype(BF16), v)
        m_ref[e] = m_next

    m_ref[...] = jnp.full(m_ref.shape, -jnp.inf, F32)
    l_ref[...] = jnp.zeros(l_ref.shape, F32)
    acc_ref[...] = jnp.zeros(acc_ref.shape, F32)

    def body(j, carry):
        tile(0, j, False)
        tile(1, j, False)
        return carry

    lax.fori_loop(0, i, body, 0)
    tile(0, i, True)
    tile(1, i, True)
    lane = lax.broadcasted_iota(jnp.int32, (tq, LANES), 1)
    o_ref[0] = jnp.where(lane < MLA_V, acc_ref[0] / l_ref[0], acc_ref[1] / l_ref[1]).astype(o_ref.dtype)


def _mla_attn(q, k, v, tq):
    b, s, _ = q.shape
    return pl.pallas_call(
        functools.partial(_mla_attn_kernel, tq=tq),
        grid=(b, MLA_HEADS // 2, s // tq),
        in_specs=[pl.BlockSpec((1, tq, 2 * LANES), lambda bi, p, i: (bi, i, p)),
                  pl.BlockSpec((1, s, 2 * LANES), lambda bi, p, i: (bi, 0, p)),
                  pl.BlockSpec((1, s, LANES), lambda bi, p, i: (bi, 0, p))],
        out_specs=pl.BlockSpec((1, tq, LANES), lambda bi, p, i: (bi, i, p)),
        out_shape=jax.ShapeDtypeStruct((b, s, MLA_HEADS * MLA_V), BF16),
        scratch_shapes=[pltpu.VMEM((2, tq, LANES), F32)] * 3,
        compiler_params=_params(("parallel", "parallel", "arbitrary")),
        name="mla_attn",
    )(q, k, v)


def _sb_attn_kernel(q_ref, k_ref, v_ref, u_ref, o_ref, c_ref, acc_ref, *, tq):
    i = pl.program_id(2)
    lane = lax.broadcasted_iota(jnp.int32, (tq, LANES), 1)
    nsub = tq // SB_SUB

    def tile(e, j, diag):
        ks = pl.multiple_of(j * tq, tq)
        q_pair = q_ref[0]
        qm = jnp.where((lane >= e * SB_DIM) & (lane < (e + 1) * SB_DIM), q_pair, jnp.zeros_like(q_pair))
        k = k_ref[0, pl.ds(ks, tq), :]
        v = v_ref[0, pl.ds(ks, tq), :]
        z = _dot_nt(qm, k)
        log_beta = jnp.minimum(z, 0.0) - jnp.log(1.0 + jnp.exp2(-jnp.abs(z))) * LOG2E
        log_not = log_beta - z
        if diag:
            row = lax.broadcasted_iota(jnp.int32, (tq, tq), 0)
            col = lax.broadcasted_iota(jnp.int32, (tq, tq), 1)
            before = col < row
            log_not = jnp.where(before, log_not, 0.0)
        hi = log_not.astype(BF16)
        lo = (log_not - hi.astype(F32)).astype(BF16)
        c = c_ref[e]
        parts = [None] * nsub
        for sub in reversed(range(nsub)):
            sl = slice(sub * SB_SUB, (sub + 1) * SB_SUB)
            within = _dot(jnp.concatenate([hi[:, sl], lo[:, sl]], axis=1), u_ref[...])
            parts[sub] = jnp.exp2(log_beta[:, sl] + within + _rep_lanes(c, SB_SUB // LANES))
            c = c + jnp.sum(log_not[:, sl], axis=1, keepdims=True)
        a = jnp.concatenate(parts, axis=1)
        if diag:
            a = jnp.where(before, a, 0.0)
        acc_ref[e] += _dot(a.astype(BF16), v)
        c_ref[e] = c

    c_ref[...] = jnp.zeros(c_ref.shape, F32)
    acc_ref[...] = jnp.zeros(acc_ref.shape, F32)
    tile(0, i, True)
    tile(1, i, True)

    def live():
        return (jnp.max(c_ref[...]) > SB_SKIP_LOG2).astype(jnp.int32)

    def cond(carry):
        jj, go = carry
        return jnp.logical_and(jj < i, go > 0)

    def body(carry):
        jj, _ = carry
        tile(0, i - 1 - jj, False)
        tile(1, i - 1 - jj, False)
        return jj + 1, live()

    lax.while_loop(cond, body, (jnp.int32(0), live()))
    o_ref[0] = jnp.where(lane < SB_DIM, acc_ref[0], acc_ref[1]).astype(o_ref.dtype)


def _sb_attn(proj_bf, tq):
    b, s, _ = proj_bf.shape
    assert tq % SB_SUB == 0
    npair = SB_HEADS // 2
    idx = np.arange(SB_SUB)
    tri = (idx[:, None] > idx[None, :]).astype(np.float32)
    u = jnp.asarray(np.concatenate([tri, tri], axis=0), dtype=BF16)
    return pl.pallas_call(
        functools.partial(_sb_attn_kernel, tq=tq),
        grid=(b, npair, s // tq),
        in_specs=[pl.BlockSpec((1, tq, LANES), lambda bi, p, i: (bi, i, p)),
                  pl.BlockSpec((1, s, LANES), lambda bi, p, i: (bi, 0, npair + p)),
                  pl.BlockSpec((1, s, LANES), lambda bi, p, i: (bi, 0, 2 * npair + p)),
                  pl.BlockSpec(u.shape, lambda bi, p, i: (0, 0))],
        out_specs=pl.BlockSpec((1, tq, LANES), lambda bi, p, i: (bi, i, p)),
        out_shape=jax.ShapeDtypeStruct((b, s, SB_HEADS * SB_DIM), BF16),
        scratch_shapes=[pltpu.VMEM((2, tq, LANES), F32)] * 2,
        compiler_params=_params(("parallel", "parallel", "arbitrary")),
        name="sb_attn",
    )(proj_bf, proj_bf, proj_bf, u)


def _retention_kernel(q_ref, k_ref, g_ref, v_ref, cos_ref, sin_ref, dec_ref, zeta_ref, xi_ref,
                      cd_ref, bd_ref, avg_ref, gn_ref, o_ref, state_ref):
    @pl.when(pl.program_id(1) == 0)
    def _():
        state_ref[...] = jnp.zeros(state_ref.shape, F32)

    width = RET_HEADS * RET_DK
    npair = RET_HEADS // 2
    half = RET_DK // 2
    lane_w = lax.broadcasted_iota(jnp.int32, (CHUNK, width), 1)
    first_half = (lane_w % RET_DK) < half
    cos_t = _rep_lanes(cos_ref[0], width // LANES)
    sin_t = _rep_lanes(sin_ref[0], width // LANES)

    def rope(x):
        swapped = jnp.where(first_half, pltpu.roll(x, width - half, 1), pltpu.roll(x, half, 1))
        return x * cos_t + swapped * sin_t

    q = rope(q_ref[0])
    k = rope(k_ref[0])
    kz = (k * zeta_ref[...]).astype(BF16)
    qb = q.astype(BF16)
    kb = k.astype(BF16)
    v = v_ref[0]
    lane = lax.broadcasted_iota(jnp.int32, (CHUNK, LANES), 1)
    ys = []
    for p in range(npair):
        sl = slice(p * LANES, (p + 1) * LANES)
        qp, kp, vp = qb[:, sl], kb[:, sl], v[:, sl]
        intra = []
        for e in range(2):
            qm = jnp.where((lane >= e * RET_DK) & (lane < (e + 1) * RET_DK), qp, jnp.zeros_like(qp))
            sc = _dot_nt(qm, kp) * dec_ref[2 * p + e]
            intra.append(_dot(sc.astype(BF16), vp))
        y_intra = jnp.where(lane < RET_DV, intra[0], intra[1])
        state = state_ref[p]
        y_cross = _dot(qp, state.astype(BF16)) * xi_ref[:, sl]
        kv = lax.dot_general(kz[:, sl], vp, TN_DIMS, preferred_element_type=F32)
        state_ref[p] = state * cd_ref[p] + kv * bd_ref[...]
        ys.append(y_intra + y_cross)
    y = jnp.concatenate(ys, axis=1)
    avg = avg_ref[...]
    d = y - _dot_hilo(y, avg)
    var = _dot_hilo(d * d, avg)
    yn = d * lax.rsqrt(var + EPS) * gn_ref[...]
    g = g_ref[0]
    o_ref[0] = (g * jax.nn.sigmoid(g) * yn).astype(o_ref.dtype)


def _retention_consts():
    h = np.arange(RET_HEADS, dtype=np.float64)
    log_gamma = np.log1p(-np.exp2(-5.0 - h))
    idx = np.arange(CHUNK, dtype=np.float64)
    diff = idx[:, None] - idx[None, :]
    dec = np.where(diff[None] >= 0, np.exp(np.maximum(diff, 0.0)[None] * log_gamma[:, None, None]), 0.0)
    zeta = np.exp((CHUNK - 1 - idx)[:, None] * log_gamma[None, :])
    xi = np.exp((idx + 1.0)[:, None] * log_gamma[None, :])
    zeta_l = np.repeat(zeta, RET_DK, axis=1)
    xi_l = np.repeat(xi, RET_DV, axis=1)
    chunk_decay = np.exp(CHUNK * log_gamma)
    npair = RET_HEADS // 2
    head_of = np.arange(LANES) // RET_DK
    bd = (head_of[:, None] == head_of[None, :]).astype(np.float64)
    cd = np.stack([bd * chunk_decay[2 * p + head_of][:, None] for p in range(npair)])
    gh = np.arange(RET_HEADS * RET_DV) // RET_DV
    avg = (gh[:, None] == gh[None, :]).astype(np.float64) / RET_DV
    f = lambda a: jnp.asarray(a, dtype=F32)
    return f(dec), f(zeta_l), f(xi_l), f(cd), f(bd), jnp.asarray(avg, dtype=BF16)


def _retention(proj_ret, proj_bf, cos_t, sin_t, gn):
    b, s, _ = proj_ret.shape
    width = RET_HEADS * RET_DK
    dec, zeta_l, xi_l, cd, bd, avg = _retention_consts()
    full = lambda a: pl.BlockSpec(a.shape, lambda bi, n: (0,) * a.ndim)
    blk = lambda c: pl.BlockSpec((1, CHUNK, width), lambda bi, n: (bi, n, c))
    tab = pl.BlockSpec((1, CHUNK, LANES), lambda bi, n: (bi, n, 0))
    return pl.pallas_call(
        _retention_kernel,
        grid=(b, s // CHUNK),
        in_specs=[blk(0), blk(1), blk(2), blk(3), tab, tab,
                  full(dec), full(zeta_l), full(xi_l), full(cd), full(bd), full(avg), full(gn)],
        out_specs=blk(0),
        out_shape=jax.ShapeDtypeStruct((b, s, RET_HEADS * RET_DV), BF16),
        scratch_shapes=[pltpu.VMEM((RET_HEADS // 2, LANES, LANES), F32)],
        compiler_params=_params(("parallel", "arbitrary")),
        name="retention",
    )(proj_ret, proj_ret, proj_ret, proj_bf, cos_t, sin_t, dec, zeta_l, xi_l, cd, bd, avg, gn)


def _merge_kernel(x_ref, ya_ref, yb_ref, yc_ref, gl_ref, wb_ref, wo_ref, o_ref):
    merged = None
    for n, y_ref in enumerate((ya_ref, yb_ref, yc_ref)):
        gate = jax.nn.sigmoid(gl_ref[:, n * D_MODEL:(n + 1) * D_MODEL])
        term = gate * _dot(y_ref[...], wb_ref[n])
        merged = term if merged is None else merged + term
    o_ref[...] = x_ref[...] + _dot(merged.astype(BF16), wo_ref[...])


def _merge(x2d, ya, yb, yc, gates, wb, wo, tm):
    t = x2d.shape[0]
    full = lambda a: pl.BlockSpec(a.shape, lambda i: (0,) * a.ndim)
    row = lambda w: pl.BlockSpec((tm, w), lambda i: (i, 0))
    return pl.pallas_call(
        _merge_kernel,
        grid=(t // tm,),
        in_specs=[row(D_MODEL), row(BRANCH_WIDTH), row(BRANCH_WIDTH), row(BRANCH_WIDTH),
                  row(N_BRANCH * D_MODEL), full(wb), full(wo)],
        out_specs=row(D_MODEL),
        out_shape=jax.ShapeDtypeStruct((t, D_MODEL), F32),
        compiler_params=_params(("parallel",)),
        name="merge",
    )(x2d, ya, yb, yc, gates, wb, wo)


def _mlp_kernel(x_ref, g_ref, wu_ref, wd_ref, gf_ref, o_ref, h_ref, acc_ref, *, final_norm):
    j = pl.program_id(1)

    @pl.when(j == 0)
    def _():
        h_ref[...] = _rms(x_ref[...], g_ref[...]).astype(BF16)
        acc_ref[...] = jnp.zeros(acc_ref.shape, F32)

    u = jnp.maximum(_dot(h_ref[...], wu_ref[...]), 0.0)
    acc_ref[...] += _dot((u * u).astype(BF16), wd_ref[...])

    @pl.when(j == pl.num_programs(1) - 1)
    def _():
        y = x_ref[...] + acc_ref[...]
        o_ref[...] = _rms(y, gf_ref[...]) if final_norm else y


def _mlp(x2d, g, wu, wd, gf, final_norm, tm, tf):
    t, d = x2d.shape
    vec = pl.BlockSpec((1, d), lambda i, j: (0, 0))
    return pl.pallas_call(
        functools.partial(_mlp_kernel, final_norm=final_norm),
        grid=(t // tm, D_FF // tf),
        in_specs=[pl.BlockSpec((tm, d), lambda i, j: (i, 0)), vec,
                  pl.BlockSpec((d, tf), lambda i, j: (0, j)),
                  pl.BlockSpec((tf, d), lambda i, j: (j, 0)), vec],
        out_specs=pl.BlockSpec((tm, d), lambda i, j: (i, 0)),
        out_shape=jax.ShapeDtypeStruct((t, d), F32),
        scratch_shapes=[pltpu.VMEM((tm, d), BF16), pltpu.VMEM((tm, d), F32)],
        compiler_params=_params(("parallel", "arbitrary")),
        name="mlp",
    )(x2d, g, wu, wd, gf)


def _pick(n, pref):
    t = min(pref, n)
    while n % t:
        t //= 2
    return t


def _rope_tables(positions):
    pos = positions.astype(F32)[..., None]

    def cs(dim):
        inv_freq = ROPE_BASE ** (-jnp.arange(0, dim, 2, dtype=F32) / dim)
        ang = pos * inv_freq
        return jnp.cos(ang), jnp.sin(ang)

    b, s = positions.shape
    cm, sm = cs(MLA_ROPE)
    ones = jnp.ones((b, s, MLA_NOPE), F32)
    z_nope = jnp.zeros((b, s, MLA_NOPE), F32)
    z_pad = jnp.zeros((b, s, LANES - MLA_NOPE - MLA_ROPE), F32)
    mla_cos = jnp.concatenate([ones, cm, cm, z_pad], axis=-1).reshape(b * s, LANES)
    mla_sin = jnp.concatenate([z_nope, -sm, sm, z_pad], axis=-1).reshape(b * s, LANES)
    cr, sr = cs(RET_DK)
    ret_cos = jnp.concatenate([cr, cr] * (LANES // RET_DK), axis=-1)
    ret_sin = jnp.concatenate([-sr, sr] * (LANES // RET_DK), axis=-1)
    return mla_cos, mla_sin, ret_cos, ret_sin


def _prep_weights(w_in, mla_w_uq, mla_w_ukv, w_branch, w_out, w_up, w_down):
    depth, d, _ = w_in.shape
    offs = np.cumsum((0, MLA_Q_RANK, MLA_KV_RANK, MLA_ROPE) + (BRANCH_WIDTH,) * 7 + (N_BRANCH * D_MODEL,))
    seg = lambda n: w_in[:, :, int(offs[n]):int(offs[n + 1])]
    c_q, c_kv, k_pe, sb_q, sb_k, sb_v, r_q, r_k, r_v, r_g, gates = (seg(n) for n in range(11))
    zeros = lambda *shape: jnp.zeros(shape, w_in.dtype)
    hr = MLA_ROPE // 2
    pad = LANES - MLA_NOPE - MLA_ROPE
    w_lat = jnp.concatenate(
        [c_q, c_kv,
         zeros(depth, d, MLA_NOPE), k_pe, zeros(depth, d, pad),
         zeros(depth, d, MLA_NOPE), k_pe[..., hr:], k_pe[..., :hr], zeros(depth, d, pad)], axis=-1)
    w_ret = jnp.concatenate([r_q, r_k * (RET_DK ** -0.5), r_g], axis=-1)
    w_bf = jnp.concatenate([sb_q * (SB_DIM ** -0.5 * LOG2E), sb_k, sb_v, r_v], axis=-1)

    uq = mla_w_uq.reshape(depth, MLA_Q_RANK, MLA_HEADS, MLA_NOPE + MLA_ROPE)
    zq = lambda w: jnp.zeros((depth, MLA_Q_RANK, MLA_HEADS, w), uq.dtype)
    nope, rope = uq[..., :MLA_NOPE], uq[..., MLA_NOPE:]
    wqa = jnp.concatenate([nope, rope, zq(pad)], axis=-1)
    wqb = jnp.concatenate([zq(MLA_NOPE), rope[..., hr:], rope[..., :hr], zq(pad)], axis=-1)
    ukv = mla_w_ukv.reshape(depth, MLA_KV_RANK, MLA_HEADS, MLA_NOPE + MLA_V)
    wk = jnp.concatenate([ukv[..., :MLA_NOPE],
                          jnp.zeros((depth, MLA_KV_RANK, MLA_HEADS, LANES - MLA_NOPE), ukv.dtype)], axis=-1)
    wv = ukv[..., MLA_NOPE:]
    flat = lambda w: w.reshape(depth, w.shape[1], -1).astype(BF16)
    bf = lambda w: w.astype(BF16)
    return dict(w_in=bf(jnp.concatenate([w_lat, w_ret, gates, w_bf], axis=-1)),
                wqa=flat(wqa), wqb=flat(wqb), wk=flat(wk), wv=flat(wv),
                wb=bf(w_branch), wo=bf(w_out), wu=bf(w_up), wd=bf(w_down))


def kernel(x, positions, norm_mix_g, w_in, mla_q_norm_g, mla_w_uq, mla_kv_norm_g, mla_w_ukv, ret_norm_g, w_branch, w_out, norm_mlp_g, w_up, w_down, final_norm_g):
    b, s, d = x.shape
    depth = w_in.shape[0]
    t = b * s
    assert d == D_MODEL and s % CHUNK == 0
    tm_proj = _pick(t, 1024)
    tm_row = _pick(t, 512)
    t_attn = _pick(s, 512)
    w = _prep_weights(w_in, mla_w_uq, mla_w_ukv, w_branch, w_out, w_up, w_down)
    mla_cos, mla_sin, ret_cos, ret_sin = _rope_tables(positions)
    vec = lambda a: a.reshape(1, -1)

    x2d = x.reshape(t, d)
    for l in range(depth):
        g_mix = vec(norm_mix_g[l])
        lat, proj_ret, gates, proj_bf = _in_proj(x2d, g_mix, w["w_in"][l], tm_row)
        proj_bf = proj_bf.reshape(b, s, -1)

        q_a, k_a, v_a = _mla_prep(lat, mla_cos, mla_sin, vec(mla_q_norm_g[l]), vec(mla_kv_norm_g[l]),
                                  w["wqa"][l], w["wqb"][l], w["wk"][l], w["wv"][l], tm_row)
        y_a = _mla_attn(q_a.reshape(b, s, -1), k_a.reshape(b, s, -1), v_a.reshape(b, s, -1), t_attn)
        y_b = _sb_attn(proj_bf, t_attn)
        y_c = _retention(proj_ret.reshape(b, s, -1), proj_bf, ret_cos, ret_sin, vec(ret_norm_g[l]))

        x2d = _merge(x2d, y_a.reshape(t, -1), y_b.reshape(t, -1), y_c.reshape(t, -1), gates,
                     w["wb"][l], w["wo"][l], tm_row)
        x2d = _mlp(x2d, vec(norm_mlp_g[l]), w["wu"][l], w["wd"][l], vec(final_norm_g),
                   l == depth - 1, tm_proj, 1024)
    return x2d.reshape(b, s, d)
```

```python
import functools

import numpy as np
import jax
import jax.numpy as jnp
from jax import lax
from jax.experimental import pallas as pl
from jax.experimental.pallas import tpu as pltpu

F32 = jnp.float32
BF16 = jnp.bfloat16

D_MODEL = 1024
EPS = 1e-6
ROPE_BASE = 10000.0
CHUNK = 128

MLA_HEADS = 8
MLA_NOPE = 64
MLA_ROPE = 32
MLA_V = 64
MLA_Q_RANK = 384
MLA_KV_RANK = 256
SB_HEADS = 8
SB_DIM = 64
RET_HEADS = 8
RET_DK = 64
RET_DV = 64
BRANCH_WIDTH = 512
N_BRANCH = 3
D_FF = 4 * D_MODEL

LANES = 128
LAT_WIDTH = MLA_Q_RANK + MLA_KV_RANK + 2 * LANES
VMEM_LIMIT = 56 * 1024 * 1024
LOG2E = 1.4426950408889634
MLA_Q_SCALE = (MLA_NOPE + MLA_ROPE) ** -0.5 * LOG2E
SB_SUB = 256
SB_SKIP_LOG2 = -160.0

NT_DIMS = (((1,), (1,)), ((), ()))
TN_DIMS = (((0,), (0,)), ((), ()))


def _dot(a, b):
    return jnp.dot(a, b, preferred_element_type=F32)


def _dot_nt(a, b):
    return lax.dot_general(a, b, NT_DIMS, preferred_element_type=F32)


def _dot_hilo(x, m_bf16):
    hi = x.astype(BF16)
    lo = (x - hi.astype(F32)).astype(BF16)
    return _dot(hi, m_bf16) + _dot(lo, m_bf16)


def _rep_lanes(x, n):
    return x if n == 1 else jnp.concatenate([x] * n, axis=1)


def _rms(x, g):
    return x * lax.rsqrt(jnp.mean(x * x, axis=-1, keepdims=True) + EPS) * g


def _params(sem):
    return pltpu.CompilerParams(dimension_semantics=sem, vmem_limit_bytes=VMEM_LIMIT)


IN_GROUPS = ((LAT_WIDTH, F32),
             (3 * BRANCH_WIDTH, F32),
             (N_BRANCH * D_MODEL, F32),
             (4 * BRANCH_WIDTH, BF16))
IN_CHUNK = 512


def _in_proj_kernel(x_ref, g_ref, w_ref, *o_refs):
    h = _rms(x_ref[...], g_ref[...]).astype(BF16)
    col = 0
    for o_ref, (width, _) in zip(o_refs, IN_GROUPS):
        for c0 in range(0, width, IN_CHUNK):
            c1 = min(c0 + IN_CHUNK, width)
            o_ref[:, c0:c1] = _dot(h, w_ref[:, col + c0:col + c1]).astype(o_ref.dtype)
        col += width


def _in_proj(x2d, g, w, tm):
    t, d = x2d.shape
    return pl.pallas_call(
        _in_proj_kernel,
        grid=(t // tm,),
        in_specs=[pl.BlockSpec((tm, d), lambda i: (i, 0)),
                  pl.BlockSpec((1, d), lambda i: (0, 0)),
                  pl.BlockSpec(w.shape, lambda i: (0, 0), pipeline_mode=pl.Buffered(1))],
        out_specs=[pl.BlockSpec((tm, width), lambda i: (i, 0)) for width, _ in IN_GROUPS],
        out_shape=[jax.ShapeDtypeStruct((t, width), dt) for width, dt in IN_GROUPS],
        compiler_params=_params(("parallel",)),
        name="in_proj",
    )(x2d, g, w)


def _mla_prep_kernel(lat_ref, cq_ref, sq_ref, gq_ref, gkv_ref, wqa_ref, wqb_ref, wk_ref, wv_ref,
                     q_ref, k_ref, v_ref):
    lat = lat_ref[...]
    c_q = lat[:, :MLA_Q_RANK]
    c_kv = lat[:, MLA_Q_RANK:MLA_Q_RANK + MLA_KV_RANK]
    kpe_a = lat[:, MLA_Q_RANK + MLA_KV_RANK:MLA_Q_RANK + MLA_KV_RANK + LANES]
    kpe_b = lat[:, MLA_Q_RANK + MLA_KV_RANK + LANES:]
    cos_t = cq_ref[...]
    sin_t = sq_ref[...]
    nq = _rms(c_q, gq_ref[...]).astype(BF16)
    nkv = _rms(c_kv, gkv_ref[...]).astype(BF16)
    q = (_dot(nq, wqa_ref[...]) * _rep_lanes(cos_t, MLA_HEADS)
         + _dot(nq, wqb_ref[...]) * _rep_lanes(sin_t, MLA_HEADS))
    q_ref[...] = (q * MLA_Q_SCALE).astype(BF16)
    k_rot = kpe_a * cos_t + kpe_b * sin_t
    k_ref[...] = (_dot(nkv, wk_ref[...]) + _rep_lanes(k_rot, MLA_HEADS)).astype(BF16)
    v_ref[...] = _dot(nkv, wv_ref[...]).astype(BF16)


def _mla_prep(lat, cos_t, sin_t, gq, gkv, wqa, wqb, wk, wv, tm):
    t = lat.shape[0]
    hw = MLA_HEADS * LANES
    full = lambda a: pl.BlockSpec(a.shape, lambda i: (0,) * a.ndim)
    row = lambda w: pl.BlockSpec((tm, w), lambda i: (i, 0))
    return pl.pallas_call(
        _mla_prep_kernel,
        grid=(t // tm,),
        in_specs=[row(LAT_WIDTH), row(LANES), row(LANES), full(gq), full(gkv),
                  full(wqa), full(wqb), full(wk), full(wv)],
        out_specs=[row(hw), row(hw), row(MLA_HEADS * MLA_V)],
        out_shape=[jax.ShapeDtypeStruct((t, hw), BF16), jax.ShapeDtypeStruct((t, hw), BF16),
                   jax.ShapeDtypeStruct((t, MLA_HEADS * MLA_V), BF16)],
        compiler_params=_params(("parallel",)),
        name="mla_prep",
    )(lat, cos_t, sin_t, gq, gkv, wqa, wqb, wk, wv)


def _mla_attn_kernel(q_ref, k_ref, v_ref, o_ref, m_ref, l_ref, acc_ref, *, tq, nheads):
    i = pl.program_id(2)
    nrep = tq // LANES

    def tiles(j, diag):
        ks = pl.multiple_of(j * tq, tq)
        heads = [slice(e * LANES, (e + 1) * LANES) for e in range(nheads)]
        scores = [_dot_nt(q_ref[0, :, heads[e]], k_ref[0, pl.ds(ks, tq), heads[e]]) for e in range(nheads)]
        if diag:
            row = lax.broadcasted_iota(jnp.int32, (tq, tq), 0)
            col = lax.broadcasted_iota(jnp.int32, (tq, tq), 1)
            scores = [jnp.where(col <= row, s, -jnp.inf) for s in scores]
        probs, alphas = [], []
        for e, s in enumerate(scores):
            m_prev = m_ref[e]
            m_next = jnp.maximum(m_prev, jnp.max(s, axis=1, keepdims=True))
            alpha = jnp.exp2(m_prev - m_next)
            p = jnp.exp2(s - _rep_lanes(m_next, nrep))
            l_ref[e] = alpha * l_ref[e] + jnp.sum(p, axis=1, keepdims=True)
            m_ref[e] = m_next
            probs.append(p.astype(BF16))
            alphas.append(alpha)
        for e in range(nheads):
            v = v_ref[0, pl.ds(ks, tq), heads[e // 2]]
            acc_ref[e] = alphas[e] * acc_ref[e] + _dot(probs[e], v)

    m_ref[...] = jnp.full(m_ref.shape, -jnp.inf, F32)
    l_ref[...] = jnp.zeros(l_ref.shape, F32)
    acc_ref[...] = jnp.zeros(acc_ref.shape, F32)

    def body(j, carry):
        tiles(j, False)
        return carry

    lax.fori_loop(0, i, body, 0)
    tiles(i, True)
    lane = lax.broadcasted_iota(jnp.int32, (tq, LANES), 1)
    for p in range(nheads // 2):
        o_ref[0, :, p * LANES:(p + 1) * LANES] = jnp.where(
            lane < MLA_V, acc_ref[2 * p] / l_ref[2 * p], acc_ref[2 * p + 1] / l_ref[2 * p + 1]).astype(o_ref.dtype)


def _mla_attn(q, k, v, tq, nheads):
    b, s, _ = q.shape
    return pl.pallas_call(
        functools.partial(_mla_attn_kernel, tq=tq, nheads=nheads),
        grid=(b, MLA_HEADS // nheads, s // tq),
        in_specs=[pl.BlockSpec((1, tq, nheads * LANES), lambda bi, p, i: (bi, i, p)),
                  pl.BlockSpec((1, s, nheads * LANES), lambda bi, p, i: (bi, 0, p)),
                  pl.BlockSpec((1, s, nheads * MLA_V), lambda bi, p, i: (bi, 0, p))],
        out_specs=pl.BlockSpec((1, tq, nheads * MLA_V), lambda bi, p, i: (bi, i, p)),
        out_shape=jax.ShapeDtypeStruct((b, s, MLA_HEADS * MLA_V), BF16),
        scratch_shapes=[pltpu.VMEM((nheads, tq, LANES), F32)] * 3,
        compiler_params=_params(("parallel", "parallel", "arbitrary")),
        name="mla_attn",
    )(q, k, v)


def _sb_attn_kernel(q_ref, k_ref, v_ref, u_ref, o_ref, qm_ref, c_ref, acc_ref, *, tq, nheads):
    i = pl.program_id(2)
    npairs = nheads // 2
    lane = lax.broadcasted_iota(jnp.int32, (tq, LANES), 1)
    zero = jnp.zeros((tq, LANES), BF16)
    for p in range(npairs):
        q_pair = q_ref[0, :, p * LANES:(p + 1) * LANES]
        qm_ref[p, :tq] = jnp.where(lane < SB_DIM, q_pair, zero)
        qm_ref[p, tq:] = jnp.where(lane < SB_DIM, zero, q_pair)

    def tiles(j, diag):
        ks = pl.multiple_of(j * tq, tq)
        pairs = [slice(p * LANES, (p + 1) * LANES) for p in range(npairs)]
        if diag:
            row = lax.broadcasted_iota(jnp.int32, (2 * tq, tq), 0)
            col = lax.broadcasted_iota(jnp.int32, (2 * tq, tq), 1)
            before = col < jnp.where(row < tq, row, row - tq)
        zs = [_dot_nt(qm_ref[p], k_ref[0, pl.ds(ks, tq), pairs[p]]) for p in range(npairs)]
        log_betas, log_nots, splits = [], [], []
        for z in zs:
            log_beta = jnp.minimum(z, 0.0) - jnp.log(1.0 + jnp.exp2(-jnp.abs(z))) * LOG2E
            log_not = log_beta - z
            if diag:
                log_not = jnp.where(before, log_not, 0.0)
            hi = log_not.astype(BF16)
            lo = (log_not - hi.astype(F32)).astype(BF16)
            log_betas.append(log_beta)
            log_nots.append(log_not)
            splits.append(jnp.concatenate([hi, lo], axis=1))
        withins = [_dot(hilo, u_ref[...]) for hilo in splits]
        weights = []
        for p in range(npairs):
            c = c_ref[p]
            a = jnp.exp2(log_betas[p] + withins[p] + _rep_lanes(c, tq // LANES))
            if diag:
                a = jnp.where(before, a, 0.0)
            weights.append(a.astype(BF16))
            c_ref[p] = c + jnp.sum(log_nots[p], axis=1, keepdims=True)
        for p in range(npairs):
            acc_ref[p] += _dot(weights[p], v_ref[0, pl.ds(ks, tq), pairs[p]])

    c_ref[...] = jnp.zeros(c_ref.shape, F32)
    acc_ref[...] = jnp.zeros(acc_ref.shape, F32)
    tiles(i, True)

    def live():
        return (jnp.max(c_ref[...]) > SB_SKIP_LOG2).astype(jnp.int32)

    def cond(carry):
        jj, go = carry
        return jnp.logical_and(jj < i, go > 0)

    def body(carry):
        jj, _ = carry
        tiles(i - 1 - jj, False)
        return jj + 1, live()

    lax.while_loop(cond, body, (jnp.int32(0), live()))
    for p in range(npairs):
        o_ref[0, :, p * LANES:(p + 1) * LANES] = jnp.where(
            lane < SB_DIM, acc_ref[p, :tq], acc_ref[p, tq:]).astype(o_ref.dtype)


def _sb_attn(proj_bf, tq, nheads):
    b, s, _ = proj_bf.shape
    assert tq == SB_SUB
    ngroup = SB_HEADS // nheads
    width = nheads * SB_DIM
    idx = np.arange(SB_SUB)
    tri = (idx[:, None] > idx[None, :]).astype(np.float32)
    u = jnp.asarray(np.concatenate([tri, tri], axis=0), dtype=BF16)
    return pl.pallas_call(
        functools.partial(_sb_attn_kernel, tq=tq, nheads=nheads),
        grid=(b, ngroup, s // tq),
        in_specs=[pl.BlockSpec((1, tq, width), lambda bi, p, i: (bi, i, p)),
                  pl.BlockSpec((1, s, width), lambda bi, p, i: (bi, 0, ngroup + p)),
                  pl.BlockSpec((1, s, width), lambda bi, p, i: (bi, 0, 2 * ngroup + p)),
                  pl.BlockSpec(u.shape, lambda bi, p, i: (0, 0))],
        out_specs=pl.BlockSpec((1, tq, width), lambda bi, p, i: (bi, i, p)),
        out_shape=jax.ShapeDtypeStruct((b, s, SB_HEADS * SB_DIM), BF16),
        scratch_shapes=[pltpu.VMEM((nheads // 2, 2 * tq, LANES), BF16),
                        pltpu.VMEM((nheads // 2, 2 * tq, LANES), F32),
                        pltpu.VMEM((nheads // 2, 2 * tq, LANES), F32)],
        compiler_params=_params(("parallel", "parallel", "arbitrary")),
        name="sb_attn",
    )(proj_bf, proj_bf, proj_bf, u)


def _retention_kernel(q_ref, k_ref, g_ref, v_ref, cos_ref, sin_ref, dec_ref, zeta_ref, xi_ref,
                      cd_ref, bd_ref, avg_ref, gn_ref, o_ref, state_ref, *, nchunk):
    @pl.when(pl.program_id(1) == 0)
    def _():
        state_ref[...] = jnp.zeros(state_ref.shape, F32)

    rows = nchunk * CHUNK
    width = RET_HEADS * RET_DK
    npair = RET_HEADS // 2
    half = RET_DK // 2
    lane_w = lax.broadcasted_iota(jnp.int32, (rows, width), 1)
    first_half = (lane_w % RET_DK) < half
    head0 = (lane_w % LANES) < RET_DK
    cos_t = _rep_lanes(cos_ref[0], width // LANES)
    sin_t = _rep_lanes(sin_ref[0], width // LANES)

    def rope(x):
        swapped = jnp.where(first_half, pltpu.roll(x, width - half, 1), pltpu.roll(x, half, 1))
        return x * cos_t + swapped * sin_t

    q = rope(q_ref[0])
    k = rope(k_ref[0])
    kz = (k * zeta_ref[...]).astype(BF16)
    qb = q.astype(BF16)
    kb = k.astype(BF16)
    zero = jnp.zeros_like(qb)
    q0 = jnp.where(head0, qb, zero)
    q1 = jnp.where(head0, zero, qb)
    v = v_ref[0]
    units = [(c, p) for c in range(nchunk) for p in range(npair)]
    rs = lambda c: slice(c * CHUNK, (c + 1) * CHUNK)
    ls = lambda p: slice(p * LANES, (p + 1) * LANES)
    scores = [_dot_nt(jnp.concatenate([q0[rs(c), ls(p)], q1[rs(c), ls(p)]], axis=0), kb[rs(c), ls(p)])
              for c, p in units]
    scores = [(s * dec_ref[p]).astype(BF16) for s, (c, p) in zip(scores, units)]
    intra = [_dot(s, v[rs(c), ls(p)]) for s, (c, p) in zip(scores, units)]
    kvs = [lax.dot_general(kz[rs(c), ls(p)], v[rs(c), ls(p)], TN_DIMS, preferred_element_type=F32)
           for c, p in units]
    lane = lax.broadcasted_iota(jnp.int32, (CHUNK, LANES), 1)
    chunks = []
    for c in range(nchunk):
        ys = []
        for p in range(npair):
            u = c * npair + p
            state = state_ref[p]
            y_cross = _dot(qb[rs(c), ls(p)], state.astype(BF16)) * xi_ref[:, ls(p)]
            state_ref[p] = state * cd_ref[p] + kvs[u] * bd_ref[...]
            y_intra = jnp.where(lane < RET_DV, intra[u][:CHUNK], intra[u][CHUNK:])
            ys.append(y_intra + y_cross)
        chunks.append(jnp.concatenate(ys, axis=1))
    y = jnp.concatenate(chunks, axis=0)
    avg = avg_ref[...]
    d = y - _dot_hilo(y, avg)
    var = _dot_hilo(d * d, avg)
    yn = d * lax.rsqrt(var + EPS) * gn_ref[...]
    g = g_ref[0]
    o_ref[0] = (g * jax.nn.sigmoid(g) * yn).astype(o_ref.dtype)


def _retention_consts(nchunk):
    h = np.arange(RET_HEADS, dtype=np.float64)
    log_gamma = np.log1p(-np.exp2(-5.0 - h))
    idx = np.arange(CHUNK, dtype=np.float64)
    diff = idx[:, None] - idx[None, :]
    dec = np.where(diff[None] >= 0, np.exp(np.maximum(diff, 0.0)[None] * log_gamma[:, None, None]), 0.0)
    npair = RET_HEADS // 2
    dec = dec.reshape(npair, 2 * CHUNK, CHUNK)
    zeta = np.exp((CHUNK - 1 - idx)[:, None] * log_gamma[None, :])
    xi = np.exp((idx + 1.0)[:, None] * log_gamma[None, :])
    zeta_l = np.tile(np.repeat(zeta, RET_DK, axis=1), (nchunk, 1))
    xi_l = np.repeat(xi, RET_DV, axis=1)
    chunk_decay = np.exp(CHUNK * log_gamma)
    head_of = np.arange(LANES) // RET_DK
    bd = (head_of[:, None] == head_of[None, :]).astype(np.float64)
    cd = np.stack([bd * chunk_decay[2 * p + head_of][:, None] for p in range(npair)])
    gh = np.arange(RET_HEADS * RET_DV) // RET_DV
    avg = (gh[:, None] == gh[None, :]).astype(np.float64) / RET_DV
    f = lambda a: jnp.asarray(a, dtype=F32)
    return f(dec), f(zeta_l), f(xi_l), f(cd), f(bd), jnp.asarray(avg, dtype=BF16)


def _retention(proj_ret, proj_bf, cos_t, sin_t, gn, nchunk):
    b, s, _ = proj_ret.shape
    width = RET_HEADS * RET_DK
    rows = nchunk * CHUNK
    dec, zeta_l, xi_l, cd, bd, avg = _retention_consts(nchunk)
    full = lambda a: pl.BlockSpec(a.shape, lambda bi, n: (0,) * a.ndim)
    blk = lambda c: pl.BlockSpec((1, rows, width), lambda bi, n: (bi, n, c))
    tab = pl.BlockSpec((1, rows, LANES), lambda bi, n: (bi, n, 0))
    return pl.pallas_call(
        functools.partial(_retention_kernel, nchunk=nchunk),
        grid=(b, s // rows),
        in_specs=[blk(0), blk(1), blk(2), blk(3), tab, tab,
                  full(dec), full(zeta_l), full(xi_l), full(cd), full(bd), full(avg), full(gn)],
        out_specs=blk(0),
        out_shape=jax.ShapeDtypeStruct((b, s, RET_HEADS * RET_DV), BF16),
        scratch_shapes=[pltpu.VMEM((RET_HEADS // 2, LANES, LANES), F32)],
        compiler_params=_params(("parallel", "arbitrary")),
        name="retention",
    )(proj_ret, proj_ret, proj_ret, proj_bf, cos_t, sin_t, dec, zeta_l, xi_l, cd, bd, avg, gn)


def _merge_kernel(x_ref, ya_ref, yb_ref, yc_ref, gl_ref, wb_ref, wo_ref, o_ref):
    merged = None
    for n, y_ref in enumerate((ya_ref, yb_ref, yc_ref)):
        gate = jax.nn.sigmoid(gl_ref[:, n * D_MODEL:(n + 1) * D_MODEL])
        term = gate * _dot(y_ref[...], wb_ref[n])
        merged = term if merged is None else merged + term
    o_ref[...] = x_ref[...] + _dot(merged.astype(BF16), wo_ref[...])


def _merge(x2d, ya, yb, yc, gates, wb, wo, tm):
    t = x2d.shape[0]
    full = lambda a: pl.BlockSpec(a.shape, lambda i: (0,) * a.ndim)
    row = lambda w: pl.BlockSpec((tm, w), lambda i: (i, 0))
    return pl.pallas_call(
        _merge_kernel,
        grid=(t // tm,),
        in_specs=[row(D_MODEL), row(BRANCH_WIDTH), row(BRANCH_WIDTH), row(BRANCH_WIDTH),
                  row(N_BRANCH * D_MODEL), full(wb), full(wo)],
        out_specs=row(D_MODEL),
        out_shape=jax.ShapeDtypeStruct((t, D_MODEL), F32),
        compiler_params=_params(("parallel",)),
        name="merge",
    )(x2d, ya, yb, yc, gates, wb, wo)


def _mlp_kernel(x_ref, g_ref, wu_ref, wd_ref, gf_ref, o_ref, h_ref, acc_ref, *, final_norm):
    j = pl.program_id(1)

    @pl.when(j == 0)
    def _():
        h_ref[...] = _rms(x_ref[...], g_ref[...]).astype(BF16)
        acc_ref[...] = jnp.zeros(acc_ref.shape, F32)

    u = jnp.maximum(_dot(h_ref[...], wu_ref[...]), 0.0)
    acc_ref[...] += _dot((u * u).astype(BF16), wd_ref[...])

    @pl.when(j == pl.num_programs(1) - 1)
    def _():
        y = x_ref[...] + acc_ref[...]
        o_ref[...] = _rms(y, gf_ref[...]) if final_norm else y


def _mlp(x2d, g, wu, wd, gf, final_norm, tm, tf):
    t, d = x2d.shape
    vec = pl.BlockSpec((1, d), lambda i, j: (0, 0))
    return pl.pallas_call(
        functools.partial(_mlp_kernel, final_norm=final_norm),
        grid=(t // tm, D_FF // tf),
        in_specs=[pl.BlockSpec((tm, d), lambda i, j: (i, 0)), vec,
                  pl.BlockSpec((d, tf), lambda i, j: (0, j)),
                  pl.BlockSpec((tf, d), lambda i, j: (j, 0)), vec],
        out_specs=pl.BlockSpec((tm, d), lambda i, j: (i, 0)),
        out_shape=jax.ShapeDtypeStruct((t, d), F32),
        scratch_shapes=[pltpu.VMEM((tm, d), BF16), pltpu.VMEM((tm, d), F32)],
        compiler_params=_params(("parallel", "arbitrary")),
        name="mlp",
    )(x2d, g, wu, wd, gf)


def _pick(n, pref):
    t = min(pref, n)
    while n % t:
        t //= 2
    return t


def _rope_tables(positions):
    pos = positions.astype(F32)[..., None]

    def cs(dim):
        inv_freq = ROPE_BASE ** (-jnp.arange(0, dim, 2, dtype=F32) / dim)
        ang = pos * inv_freq
        return jnp.cos(ang), jnp.sin(ang)

    b, s = positions.shape
    cm, sm = cs(MLA_ROPE)
    ones = jnp.ones((b, s, MLA_NOPE), F32)
    z_nope = jnp.zeros((b, s, MLA_NOPE), F32)
    z_pad = jnp.zeros((b, s, LANES - MLA_NOPE - MLA_ROPE), F32)
    mla_cos = jnp.concatenate([ones, cm, cm, z_pad], axis=-1).reshape(b * s, LANES)
    mla_sin = jnp.concatenate([z_nope, -sm, sm, z_pad], axis=-1).reshape(b * s, LANES)
    cr, sr = cs(RET_DK)
    ret_cos = jnp.concatenate([cr, cr] * (LANES // RET_DK), axis=-1)
    ret_sin = jnp.concatenate([-sr, sr] * (LANES // RET_DK), axis=-1)
    return mla_cos, mla_sin, ret_cos, ret_sin


def _prep_weights(w_in, mla_w_uq, mla_w_ukv, w_branch, w_out, w_up, w_down):
    depth, d, _ = w_in.shape
    offs = np.cumsum((0, MLA_Q_RANK, MLA_KV_RANK, MLA_ROPE) + (BRANCH_WIDTH,) * 7 + (N_BRANCH * D_MODEL,))
    seg = lambda n: w_in[:, :, int(offs[n]):int(offs[n + 1])]
    c_q, c_kv, k_pe, sb_q, sb_k, sb_v, r_q, r_k, r_v, r_g, gates = (seg(n) for n in range(11))
    zeros = lambda *shape: jnp.zeros(shape, w_in.dtype)
    hr = MLA_ROPE // 2
    pad = LANES - MLA_NOPE - MLA_ROPE
    w_lat = jnp.concatenate(
        [c_q, c_kv,
         zeros(depth, d, MLA_NOPE), k_pe, zeros(depth, d, pad),
         zeros(depth, d, MLA_NOPE), k_pe[..., hr:], k_pe[..., :hr], zeros(depth, d, pad)], axis=-1)
    w_ret = jnp.concatenate([r_q, r_k * (RET_DK ** -0.5), r_g], axis=-1)
    w_bf = jnp.concatenate([sb_q * (SB_DIM ** -0.5 * LOG2E), sb_k, sb_v, r_v], axis=-1)

    uq = mla_w_uq.reshape(depth, MLA_Q_RANK, MLA_HEADS, MLA_NOPE + MLA_ROPE)
    zq = lambda w: jnp.zeros((depth, MLA_Q_RANK, MLA_HEADS, w), uq.dtype)
    nope, rope = uq[..., :MLA_NOPE], uq[..., MLA_NOPE:]
    wqa = jnp.concatenate([nope, rope, zq(pad)], axis=-1)
    wqb = jnp.concatenate([zq(MLA_NOPE), rope[..., hr:], rope[..., :hr], zq(pad)], axis=-1)
    ukv = mla_w_ukv.reshape(depth, MLA_KV_RANK, MLA_HEADS, MLA_NOPE + MLA_V)
    wk = jnp.concatenate([ukv[..., :MLA_NOPE],
                          jnp.zeros((depth, MLA_KV_RANK, MLA_HEADS, LANES - MLA_NOPE), ukv.dtype)], axis=-1)
    wv = ukv[..., MLA_NOPE:]
    flat = lambda w: w.reshape(depth, w.shape[1], -1).astype(BF16)
    bf = lambda w: w.astype(BF16)
    return dict(w_in=bf(jnp.concatenate([w_lat, w_ret, gates, w_bf], axis=-1)),
                wqa=flat(wqa), wqb=flat(wqb), wk=flat(wk), wv=flat(wv),
                wb=bf(w_branch), wo=bf(w_out), wu=bf(w_up), wd=bf(w_down))


def kernel(x, positions, norm_mix_g, w_in, mla_q_norm_g, mla_w_uq, mla_kv_norm_g, mla_w_ukv, ret_norm_g, w_branch, w_out, norm_mlp_g, w_up, w_down, final_norm_g):
    b, s, d = x.shape
    depth = w_in.shape[0]
    t = b * s
    assert d == D_MODEL and s % CHUNK == 0
    tm_proj = _pick(t, 1024)
    tm_row = _pick(t, 512)
    t_mla = _pick(s, 512)
    t_sb = max(_pick(s, 256), SB_SUB)
    w = _prep_weights(w_in, mla_w_uq, mla_w_ukv, w_branch, w_out, w_up, w_down)
    mla_cos, mla_sin, ret_cos, ret_sin = _rope_tables(positions)
    vec = lambda a: a.reshape(1, -1)

    x2d = x.reshape(t, d)
    for l in range(depth):
        g_mix = vec(norm_mix_g[l])
        lat, proj_ret, gates, proj_bf = _in_proj(x2d, g_mix, w["w_in"][l], tm_row)
        proj_bf = proj_bf.reshape(b, s, -1)

        q_a, k_a, v_a = _mla_prep(lat, mla_cos, mla_sin, vec(mla_q_norm_g[l]), vec(mla_kv_norm_g[l]),
                                  w["wqa"][l], w["wqb"][l], w["wk"][l], w["wv"][l], tm_row)
        y_a = _mla_attn(q_a.reshape(b, s, -1), k_a.reshape(b, s, -1), v_a.reshape(b, s, -1), t_mla, 4)
        y_b = _sb_attn(proj_bf, t_sb, 8)
        y_c = _retention(proj_ret.reshape(b, s, -1), proj_bf, ret_cos, ret_sin, vec(ret_norm_g[l]),
                         _pick(s // CHUNK, 4))

        x2d = _merge(x2d, y_a.reshape(t, -1), y_b.reshape(t, -1), y_c.reshape(t, -1), gates,
                     w["wb"][l], w["wo"][l], tm_row)
        x2d = _mlp(x2d, vec(norm_mlp_g[l]), w["wu"][l], w["wd"][l], vec(final_norm_g),
                   l == depth - 1, tm_proj, 1024)
    return x2d.reshape(b, s, d)
```

```python
import functools

import numpy as np
import jax
import jax.numpy as jnp
from jax import lax
from jax.experimental import pallas as pl
from jax.experimental.pallas import tpu as pltpu

F32 = jnp.float32
BF16 = jnp.bfloat16

D_MODEL = 1024
EPS = 1e-6
ROPE_BASE = 10000.0
CHUNK = 128

MLA_HEADS = 8
MLA_NOPE = 64
MLA_ROPE = 32
MLA_V = 64
MLA_Q_RANK = 384
MLA_KV_RANK = 256
SB_HEADS = 8
SB_DIM = 64
RET_HEADS = 8
RET_DK = 64
RET_DV = 64
BRANCH_WIDTH = 512
N_BRANCH = 3
D_FF = 4 * D_MODEL

LANES = 128
LAT_WIDTH = MLA_Q_RANK + MLA_KV_RANK + 2 * LANES
VMEM_LIMIT = 56 * 1024 * 1024
LOG2E = 1.4426950408889634
MLA_Q_SCALE = (MLA_NOPE + MLA_ROPE) ** -0.5 * LOG2E
SB_SUB = 256
SB_SKIP_LOG2 = -160.0

NT_DIMS = (((1,), (1,)), ((), ()))
TN_DIMS = (((0,), (0,)), ((), ()))


def _dot(a, b):
    return jnp.dot(a, b, preferred_element_type=F32)


def _dot_nt(a, b):
    return lax.dot_general(a, b, NT_DIMS, preferred_element_type=F32)


def _dot_hilo(x, m_bf16):
    hi = x.astype(BF16)
    lo = (x - hi.astype(F32)).astype(BF16)
    return _dot(hi, m_bf16) + _dot(lo, m_bf16)


def _rep_lanes(x, n):
    return x if n == 1 else jnp.concatenate([x] * n, axis=1)


def _rms(x, g):
    return x * lax.rsqrt(jnp.mean(x * x, axis=-1, keepdims=True) + EPS) * g


def _params(sem):
    return pltpu.CompilerParams(dimension_semantics=sem, vmem_limit_bytes=VMEM_LIMIT)


IN_GROUPS = ((LAT_WIDTH, F32),
             (3 * BRANCH_WIDTH, F32),
             (4 * BRANCH_WIDTH, BF16))
IN_CHUNK = 512


def _in_proj_kernel(x_ref, g_ref, w_ref, *o_refs):
    h = _rms(x_ref[...], g_ref[...]).astype(BF16)
    col = 0
    for o_ref, (width, _) in zip(o_refs, IN_GROUPS):
        for c0 in range(0, width, IN_CHUNK):
            c1 = min(c0 + IN_CHUNK, width)
            o_ref[:, c0:c1] = _dot(h, w_ref[:, col + c0:col + c1]).astype(o_ref.dtype)
        col += width


def _in_proj(x2d, g, w, tm):
    t, d = x2d.shape
    return pl.pallas_call(
        _in_proj_kernel,
        grid=(t // tm,),
        in_specs=[pl.BlockSpec((tm, d), lambda i: (i, 0)),
                  pl.BlockSpec((1, d), lambda i: (0, 0)),
                  pl.BlockSpec(w.shape, lambda i: (0, 0), pipeline_mode=pl.Buffered(1))],
        out_specs=[pl.BlockSpec((tm, width), lambda i: (i, 0)) for width, _ in IN_GROUPS],
        out_shape=[jax.ShapeDtypeStruct((t, width), dt) for width, dt in IN_GROUPS],
        compiler_params=_params(("parallel",)),
        name="in_proj",
    )(x2d, g, w)


def _mla_prep_kernel(lat_ref, cq_ref, sq_ref, gq_ref, gkv_ref, wqa_ref, wqb_ref, wk_ref, wv_ref,
                     q_ref, k_ref, v_ref):
    lat = lat_ref[...]
    c_q = lat[:, :MLA_Q_RANK]
    c_kv = lat[:, MLA_Q_RANK:MLA_Q_RANK + MLA_KV_RANK]
    kpe_a = lat[:, MLA_Q_RANK + MLA_KV_RANK:MLA_Q_RANK + MLA_KV_RANK + LANES]
    kpe_b = lat[:, MLA_Q_RANK + MLA_KV_RANK + LANES:]
    cos_t = cq_ref[...]
    sin_t = sq_ref[...]
    nq = _rms(c_q, gq_ref[...]).astype(BF16)
    nkv = _rms(c_kv, gkv_ref[...]).astype(BF16)
    q = (_dot(nq, wqa_ref[...]) * _rep_lanes(cos_t, MLA_HEADS)
         + _dot(nq, wqb_ref[...]) * _rep_lanes(sin_t, MLA_HEADS))
    q_ref[...] = (q * MLA_Q_SCALE).astype(BF16)
    k_rot_t = jnp.transpose(kpe_a * cos_t + kpe_b * sin_t)
    k_t = _dot_nt(wk_ref[...], nkv) + jnp.concatenate([k_rot_t] * MLA_HEADS, axis=0)
    k_ref[0] = k_t.astype(BF16)
    v_ref[...] = _dot(nkv, wv_ref[...]).astype(BF16)


def _mla_prep(lat, cos_t, sin_t, gq, gkv, wqa, wqb, wk_t, wv, b, tm):
    t = lat.shape[0]
    hw = MLA_HEADS * LANES
    nsb = t // b // tm
    full = lambda a: pl.BlockSpec(a.shape, lambda i: (0,) * a.ndim)
    row = lambda w: pl.BlockSpec((tm, w), lambda i: (i, 0))
    return pl.pallas_call(
        _mla_prep_kernel,
        grid=(t // tm,),
        in_specs=[row(LAT_WIDTH), row(LANES), row(LANES), full(gq), full(gkv),
                  full(wqa), full(wqb), full(wk_t), full(wv)],
        out_specs=[row(hw), pl.BlockSpec((1, hw, tm), lambda i: (i // nsb, 0, i % nsb)),
                   row(MLA_HEADS * MLA_V)],
        out_shape=[jax.ShapeDtypeStruct((t, hw), BF16), jax.ShapeDtypeStruct((b, hw, t // b), BF16),
                   jax.ShapeDtypeStruct((t, MLA_HEADS * MLA_V), BF16)],
        compiler_params=_params(("parallel",)),
        name="mla_prep",
    )(lat, cos_t, sin_t, gq, gkv, wqa, wqb, wk_t, wv)


def _mla_attn_kernel(q_ref, kt_ref, v_ref, o_ref, m_ref, l_ref, acc_ref, *, tq, tk, nheads):
    i = pl.program_id(2)
    npairs = nheads // 2
    heads = [slice(e * LANES, (e + 1) * LANES) for e in range(nheads)]

    def tiles(j, r0, nr, masked):
        ks = pl.multiple_of(j * tk, tk)
        rows = slice(r0, r0 + nr)
        scores = [_dot(q_ref[0, rows, heads[e]], kt_ref[0, heads[e], pl.ds(ks, tk)]) for e in range(nheads)]
        if masked:
            q_pos = i * tq + r0 + lax.broadcasted_iota(jnp.int32, (nr, tk), 0)
            k_pos = ks + lax.broadcasted_iota(jnp.int32, (nr, tk), 1)
            scores = [jnp.where(k_pos <= q_pos, s, -jnp.inf) for s in scores]
        probs, alphas = [], []
        for e, s in enumerate(scores):
            m_prev = m_ref[e, rows]
            m_next = jnp.maximum(m_prev, jnp.max(s, axis=1, keepdims=True))
            alpha = jnp.exp2(m_prev - m_next)
            p = jnp.exp2(s - _rep_lanes(m_next, tk // LANES))
            l_ref[e, rows] = alpha * l_ref[e, rows] + jnp.sum(p, axis=1, keepdims=True)
            m_ref[e, rows] = m_next
            probs.append(p.astype(BF16))
            alphas.append(alpha)
        for p in range(npairs):
            pv = _dot(jnp.concatenate([probs[2 * p], probs[2 * p + 1]], axis=0), v_ref[0, pl.ds(ks, tk), heads[p]])
            for e in range(2):
                arows = slice(e * tq + r0, e * tq + r0 + nr)
                acc_ref[p, arows] = alphas[2 * p + e] * acc_ref[p, arows] + pv[e * nr:(e + 1) * nr]

    m_ref[...] = jnp.full(m_ref.shape, -jnp.inf, F32)
    l_ref[...] = jnp.zeros(l_ref.shape, F32)
    acc_ref[...] = jnp.zeros(acc_ref.shape, F32)

    def body(j, carry):
        tiles(j, 0, tq, False)
        return carry

    lax.fori_loop(0, 2 * i, body, 0)
    tiles(2 * i, 0, tq, True)
    tiles(2 * i + 1, tk, tq - tk, True)
    lane = lax.broadcasted_iota(jnp.int32, (tq, LANES), 1)
    for p in range(npairs):
        o_ref[0, :, heads[p]] = jnp.where(
            lane < MLA_V, acc_ref[p, :tq] / l_ref[2 * p], acc_ref[p, tq:] / l_ref[2 * p + 1]).astype(o_ref.dtype)


def _mla_attn(q, k_t, v, tq, nheads):
    b, s, _ = q.shape
    tk = tq // 2
    return pl.pallas_call(
        functools.partial(_mla_attn_kernel, tq=tq, tk=tk, nheads=nheads),
        grid=(b, MLA_HEADS // nheads, s // tq),
        in_specs=[pl.BlockSpec((1, tq, nheads * LANES), lambda bi, p, i: (bi, i, p)),
                  pl.BlockSpec((1, nheads * LANES, s), lambda bi, p, i: (bi, p, 0)),
                  pl.BlockSpec((1, s, nheads * MLA_V), lambda bi, p, i: (bi, 0, p))],
        out_specs=pl.BlockSpec((1, tq, nheads * MLA_V), lambda bi, p, i: (bi, i, p)),
        out_shape=jax.ShapeDtypeStruct((b, s, MLA_HEADS * MLA_V), BF16),
        scratch_shapes=[pltpu.VMEM((nheads, tq, LANES), F32), pltpu.VMEM((nheads, tq, LANES), F32),
                        pltpu.VMEM((nheads // 2, 2 * tq, LANES), F32)],
        compiler_params=_params(("parallel", "parallel", "arbitrary")),
        name="mla_attn",
    )(q, k_t, v)


def _sb_attn_kernel(q_ref, k_ref, v_ref, u_ref, o_ref, qm_ref, c_ref, acc_ref, *, tq, nheads):
    i = pl.program_id(2)
    npairs = nheads // 2
    lane = lax.broadcasted_iota(jnp.int32, (tq, LANES), 1)
    zero = jnp.zeros((tq, LANES), BF16)
    for p in range(npairs):
        q_pair = q_ref[0, :, p * LANES:(p + 1) * LANES]
        qm_ref[p, :tq] = jnp.where(lane < SB_DIM, q_pair, zero)
        qm_ref[p, tq:] = jnp.where(lane < SB_DIM, zero, q_pair)

    def tiles(j, diag):
        ks = pl.multiple_of(j * tq, tq)
        pairs = [slice(p * LANES, (p + 1) * LANES) for p in range(npairs)]
        if diag:
            row = lax.broadcasted_iota(jnp.int32, (2 * tq, tq), 0)
            col = lax.broadcasted_iota(jnp.int32, (2 * tq, tq), 1)
            before = col < jnp.where(row < tq, row, row - tq)
        zs = [_dot_nt(qm_ref[p], k_ref[0, pl.ds(ks, tq), pairs[p]]) for p in range(npairs)]
        log_betas, log_nots, splits = [], [], []
        for z in zs:
            log_beta = jnp.minimum(z, 0.0) - jnp.log(1.0 + jnp.exp2(-jnp.abs(z))) * LOG2E
            log_not = log_beta - z
            if diag:
                log_not = jnp.where(before, log_not, 0.0)
            hi = log_not.astype(BF16)
            lo = (log_not - hi.astype(F32)).astype(BF16)
            log_betas.append(log_beta)
            log_nots.append(log_not)
            splits.append(jnp.concatenate([hi, lo], axis=1))
        withins = [_dot(hilo, u_ref[...]) for hilo in splits]
        weights = []
        for p in range(npairs):
            c = c_ref[p]
            a = jnp.exp2(log_betas[p] + withins[p] + _rep_lanes(c, tq // LANES))
            if diag:
                a = jnp.where(before, a, 0.0)
            weights.append(a.astype(BF16))
            c_ref[p] = c + jnp.sum(log_nots[p], axis=1, keepdims=True)
        for p in range(npairs):
            acc_ref[p] += _dot(weights[p], v_ref[0, pl.ds(ks, tq), pairs[p]])

    c_ref[...] = jnp.zeros(c_ref.shape, F32)
    acc_ref[...] = jnp.zeros(acc_ref.shape, F32)
    tiles(i, True)

    def live():
        return (jnp.max(c_ref[...]) > SB_SKIP_LOG2).astype(jnp.int32)

    def cond(carry):
        jj, go = carry
        return jnp.logical_and(jj < i, go > 0)

    def body(carry):
        jj, _ = carry
        tiles(i - 1 - jj, False)
        return jj + 1, live()

    lax.while_loop(cond, body, (jnp.int32(0), live()))
    for p in range(npairs):
        o_ref[0, :, p * LANES:(p + 1) * LANES] = jnp.where(
            lane < SB_DIM, acc_ref[p, :tq], acc_ref[p, tq:]).astype(o_ref.dtype)


def _sb_attn(proj_bf, tq, nheads):
    b, s, _ = proj_bf.shape
    assert tq == SB_SUB
    ngroup = SB_HEADS // nheads
    width = nheads * SB_DIM
    idx = np.arange(SB_SUB)
    tri = (idx[:, None] > idx[None, :]).astype(np.float32)
    u = jnp.asarray(np.concatenate([tri, tri], axis=0), dtype=BF16)
    return pl.pallas_call(
        functools.partial(_sb_attn_kernel, tq=tq, nheads=nheads),
        grid=(b, ngroup, s // tq),
        in_specs=[pl.BlockSpec((1, tq, width), lambda bi, p, i: (bi, i, p)),
                  pl.BlockSpec((1, s, width), lambda bi, p, i: (bi, 0, ngroup + p)),
                  pl.BlockSpec((1, s, width), lambda bi, p, i: (bi, 0, 2 * ngroup + p)),
                  pl.BlockSpec(u.shape, lambda bi, p, i: (0, 0))],
        out_specs=pl.BlockSpec((1, tq, width), lambda bi, p, i: (bi, i, p)),
        out_shape=jax.ShapeDtypeStruct((b, s, SB_HEADS * SB_DIM), BF16),
        scratch_shapes=[pltpu.VMEM((nheads // 2, 2 * tq, LANES), BF16),
                        pltpu.VMEM((nheads // 2, 2 * tq, LANES), F32),
                        pltpu.VMEM((nheads // 2, 2 * tq, LANES), F32)],
        compiler_params=_params(("parallel", "parallel", "arbitrary")),
        name="sb_attn",
    )(proj_bf, proj_bf, proj_bf, u)


def _retention_kernel(q_ref, k_ref, g_ref, v_ref, cos_ref, sin_ref, dec_ref, zeta_ref, xi_ref,
                      cd_ref, bd_ref, avg_ref, gn_ref, o_ref, state_ref, *, nchunk):
    @pl.when(pl.program_id(1) == 0)
    def _():
        state_ref[...] = jnp.zeros(state_ref.shape, F32)

    rows = nchunk * CHUNK
    width = RET_HEADS * RET_DK
    npair = RET_HEADS // 2
    half = RET_DK // 2
    lane_w = lax.broadcasted_iota(jnp.int32, (rows, width), 1)
    first_half = (lane_w % RET_DK) < half
    head0 = (lane_w % LANES) < RET_DK
    cos_t = _rep_lanes(cos_ref[0], width // LANES)
    sin_t = _rep_lanes(sin_ref[0], width // LANES)

    def rope(x):
        swapped = jnp.where(first_half, pltpu.roll(x, width - half, 1), pltpu.roll(x, half, 1))
        return x * cos_t + swapped * sin_t

    q = rope(q_ref[0])
    k = rope(k_ref[0])
    kz = (k * zeta_ref[...]).astype(BF16)
    qb = q.astype(BF16)
    kb = k.astype(BF16)
    zero = jnp.zeros_like(qb)
    q0 = jnp.where(head0, qb, zero)
    q1 = jnp.where(head0, zero, qb)
    v = v_ref[0]
    units = [(c, p) for c in range(nchunk) for p in range(npair)]
    rs = lambda c: slice(c * CHUNK, (c + 1) * CHUNK)
    ls = lambda p: slice(p * LANES, (p + 1) * LANES)
    scores = [_dot_nt(jnp.concatenate([q0[rs(c), ls(p)], q1[rs(c), ls(p)]], axis=0), kb[rs(c), ls(p)])
              for c, p in units]
    scores = [(s * dec_ref[p]).astype(BF16) for s, (c, p) in zip(scores, units)]
    intra = [_dot(s, v[rs(c), ls(p)]) for s, (c, p) in zip(scores, units)]
    kvs = [lax.dot_general(kz[rs(c), ls(p)], v[rs(c), ls(p)], TN_DIMS, preferred_element_type=F32)
           for c, p in units]
    lane = lax.broadcasted_iota(jnp.int32, (CHUNK, LANES), 1)
    chunks = []
    for c in range(nchunk):
        ys = []
        for p in range(npair):
            u = c * npair + p
            state = state_ref[p]
            y_cross = _dot(qb[rs(c), ls(p)], state.astype(BF16)) * xi_ref[:, ls(p)]
            state_ref[p] = state * cd_ref[p] + kvs[u] * bd_ref[...]
            y_intra = jnp.where(lane < RET_DV, intra[u][:CHUNK], intra[u][CHUNK:])
            ys.append(y_intra + y_cross)
        chunks.append(jnp.concatenate(ys, axis=1))
    y = jnp.concatenate(chunks, axis=0)
    avg = avg_ref[...]
    d = y - _dot_hilo(y, avg)
    var = _dot_hilo(d * d, avg)
    yn = d * lax.rsqrt(var + EPS) * gn_ref[...]
    g = g_ref[0]
    o_ref[0] = (g * jax.nn.sigmoid(g) * yn).astype(o_ref.dtype)


def _retention_consts(nchunk):
    h = np.arange(RET_HEADS, dtype=np.float64)
    log_gamma = np.log1p(-np.exp2(-5.0 - h))
    idx = np.arange(CHUNK, dtype=np.float64)
    diff = idx[:, None] - idx[None, :]
    dec = np.where(diff[None] >= 0, np.exp(np.maximum(diff, 0.0)[None] * log_gamma[:, None, None]), 0.0)
    npair = RET_HEADS // 2
    dec = dec.reshape(npair, 2 * CHUNK, CHUNK)
    zeta = np.exp((CHUNK - 1 - idx)[:, None] * log_gamma[None, :])
    xi = np.exp((idx + 1.0)[:, None] * log_gamma[None, :])
    zeta_l = np.tile(np.repeat(zeta, RET_DK, axis=1), (nchunk, 1))
    xi_l = np.repeat(xi, RET_DV, axis=1)
    chunk_decay = np.exp(CHUNK * log_gamma)
    head_of = np.arange(LANES) // RET_DK
    bd = (head_of[:, None] == head_of[None, :]).astype(np.float64)
    cd = np.stack([bd * chunk_decay[2 * p + head_of][:, None] for p in range(npair)])
    gh = np.arange(RET_HEADS * RET_DV) // RET_DV
    avg = (gh[:, None] == gh[None, :]).astype(np.float64) / RET_DV
    f = lambda a: jnp.asarray(a, dtype=F32)
    return f(dec), f(zeta_l), f(xi_l), f(cd), f(bd), jnp.asarray(avg, dtype=BF16)


def _retention(proj_ret, proj_bf, cos_t, sin_t, gn, nchunk):
    b, s, _ = proj_ret.shape
    width = RET_HEADS * RET_DK
    rows = nchunk * CHUNK
    dec, zeta_l, xi_l, cd, bd, avg = _retention_consts(nchunk)
    full = lambda a: pl.BlockSpec(a.shape, lambda bi, n: (0,) * a.ndim)
    blk = lambda c: pl.BlockSpec((1, rows, width), lambda bi, n: (bi, n, c))
    tab = pl.BlockSpec((1, rows, LANES), lambda bi, n: (bi, n, 0))
    return pl.pallas_call(
        functools.partial(_retention_kernel, nchunk=nchunk),
        grid=(b, s // rows),
        in_specs=[blk(0), blk(1), blk(2), blk(3), tab, tab,
                  full(dec), full(zeta_l), full(xi_l), full(cd), full(bd), full(avg), full(gn)],
        out_specs=blk(0),
        out_shape=jax.ShapeDtypeStruct((b, s, RET_HEADS * RET_DV), BF16),
        scratch_shapes=[pltpu.VMEM((RET_HEADS // 2, LANES, LANES), F32)],
        compiler_params=_params(("parallel", "arbitrary")),
        name="retention",
    )(proj_ret, proj_ret, proj_ret, proj_bf, cos_t, sin_t, dec, zeta_l, xi_l, cd, bd, avg, gn)


def _merge_kernel(x_ref, g_ref, ya_ref, yb_ref, yc_ref, wg_ref, wb_ref, wo_ref, o_ref):
    x = x_ref[...]
    h = _rms(x, g_ref[...]).astype(BF16)
    merged = None
    for n, y_ref in enumerate((ya_ref, yb_ref, yc_ref)):
        gate = jax.nn.sigmoid(_dot(h, wg_ref[:, n * D_MODEL:(n + 1) * D_MODEL]))
        term = gate * _dot(y_ref[...], wb_ref[n])
        merged = term if merged is None else merged + term
    o_ref[...] = x + _dot(merged.astype(BF16), wo_ref[...])


def _merge(x2d, g, ya, yb, yc, wg, wb, wo, tm):
    t = x2d.shape[0]
    full = lambda a: pl.BlockSpec(a.shape, lambda i: (0,) * a.ndim, pipeline_mode=pl.Buffered(1))
    row = lambda w: pl.BlockSpec((tm, w), lambda i: (i, 0))
    return pl.pallas_call(
        _merge_kernel,
        grid=(t // tm,),
        in_specs=[row(D_MODEL), full(g), row(BRANCH_WIDTH), row(BRANCH_WIDTH), row(BRANCH_WIDTH),
                  full(wg), full(wb), full(wo)],
        out_specs=row(D_MODEL),
        out_shape=jax.ShapeDtypeStruct((t, D_MODEL), F32),
        compiler_params=_params(("parallel",)),
        name="merge",
    )(x2d, g, ya, yb, yc, wg, wb, wo)


def _mlp_kernel(x_ref, g_ref, wu_ref, wd_ref, gf_ref, o_ref, h_ref, acc_ref, *, final_norm):
    j = pl.program_id(1)

    @pl.when(j == 0)
    def _():
        h_ref[...] = _rms(x_ref[...], g_ref[...]).astype(BF16)
        acc_ref[...] = jnp.zeros(acc_ref.shape, F32)

    u = jnp.maximum(_dot(h_ref[...], wu_ref[...]), 0.0)
    acc_ref[...] += _dot((u * u).astype(BF16), wd_ref[...])

    @pl.when(j == pl.num_programs(1) - 1)
    def _():
        y = x_ref[...] + acc_ref[...]
        o_ref[...] = _rms(y, gf_ref[...]) if final_norm else y


def _mlp(x2d, g, wu, wd, gf, final_norm, tm, tf):
    t, d = x2d.shape
    vec = pl.BlockSpec((1, d), lambda i, j: (0, 0))
    return pl.pallas_call(
        functools.partial(_mlp_kernel, final_norm=final_norm),
        grid=(t // tm, D_FF // tf),
        in_specs=[pl.BlockSpec((tm, d), lambda i, j: (i, 0)), vec,
                  pl.BlockSpec((d, tf), lambda i, j: (0, j)),
                  pl.BlockSpec((tf, d), lambda i, j: (j, 0)), vec],
        out_specs=pl.BlockSpec((tm, d), lambda i, j: (i, 0)),
        out_shape=jax.ShapeDtypeStruct((t, d), F32),
        scratch_shapes=[pltpu.VMEM((tm, d), BF16), pltpu.VMEM((tm, d), F32)],
        compiler_params=_params(("parallel", "arbitrary")),
        name="mlp",
    )(x2d, g, wu, wd, gf)


def _pick(n, pref):
    t = min(pref, n)
    while n % t:
        t //= 2
    return t


def _rope_tables(positions):
    pos = positions.astype(F32)[..., None]

    def cs(dim):
        inv_freq = ROPE_BASE ** (-jnp.arange(0, dim, 2, dtype=F32) / dim)
        ang = pos * inv_freq
        return jnp.cos(ang), jnp.sin(ang)

    b, s = positions.shape
    cm, sm = cs(MLA_ROPE)
    ones = jnp.ones((b, s, MLA_NOPE), F32)
    z_nope = jnp.zeros((b, s, MLA_NOPE), F32)
    z_pad = jnp.zeros((b, s, LANES - MLA_NOPE - MLA_ROPE), F32)
    mla_cos = jnp.concatenate([ones, cm, cm, z_pad], axis=-1).reshape(b * s, LANES)
    mla_sin = jnp.concatenate([z_nope, -sm, sm, z_pad], axis=-1).reshape(b * s, LANES)
    cr, sr = cs(RET_DK)
    ret_cos = jnp.concatenate([cr, cr] * (LANES // RET_DK), axis=-1)
    ret_sin = jnp.concatenate([-sr, sr] * (LANES // RET_DK), axis=-1)
    return mla_cos, mla_sin, ret_cos, ret_sin


def _prep_weights(w_in, mla_w_uq, mla_w_ukv, w_branch, w_out, w_up, w_down):
    depth, d, _ = w_in.shape
    offs = [int(o) for o in np.cumsum((0, MLA_Q_RANK, MLA_KV_RANK, MLA_ROPE) + (BRANCH_WIDTH,) * 7
                                      + (N_BRANCH * D_MODEL,))]
    col_scale = np.ones(offs[-1], np.float32)
    col_scale[offs[3]:offs[4]] = SB_DIM ** -0.5 * LOG2E
    col_scale[offs[7]:offs[8]] = RET_DK ** -0.5
    w = (w_in * col_scale).astype(BF16)
    seg = lambda n: w[:, :, offs[n]:offs[n + 1]]
    c_q, c_kv, k_pe, sb_q, sb_k, sb_v, r_q, r_k, r_v, r_g, gates = (seg(n) for n in range(11))
    zeros = lambda *shape: jnp.zeros(shape, BF16)
    hr = MLA_ROPE // 2
    pad = LANES - MLA_NOPE - MLA_ROPE
    w_proj = jnp.concatenate(
        [c_q, c_kv,
         zeros(depth, d, MLA_NOPE), k_pe, zeros(depth, d, pad),
         zeros(depth, d, MLA_NOPE), k_pe[..., hr:], k_pe[..., :hr], zeros(depth, d, pad),
         r_q, r_k, r_g,
         sb_q, sb_k, sb_v, r_v], axis=-1)

    uq = mla_w_uq.astype(BF16).reshape(depth, MLA_Q_RANK, MLA_HEADS, MLA_NOPE + MLA_ROPE)
    zq = lambda width: jnp.zeros((depth, MLA_Q_RANK, MLA_HEADS, width), BF16)
    nope, rope = uq[..., :MLA_NOPE], uq[..., MLA_NOPE:]
    wqa = jnp.concatenate([nope, rope, zq(pad)], axis=-1)
    wqb = jnp.concatenate([zq(MLA_NOPE), rope[..., hr:], rope[..., :hr], zq(pad)], axis=-1)
    ukv = mla_w_ukv.astype(BF16).reshape(depth, MLA_KV_RANK, MLA_HEADS, MLA_NOPE + MLA_V)
    wk = jnp.concatenate([ukv[..., :MLA_NOPE],
                          jnp.zeros((depth, MLA_KV_RANK, MLA_HEADS, LANES - MLA_NOPE), BF16)], axis=-1)
    wv = ukv[..., MLA_NOPE:]
    flat = lambda a: a.reshape(depth, a.shape[1], -1)
    bf = lambda a: a.astype(BF16)
    return dict(w_in=w_proj, w_gate=gates,
                wqa=flat(wqa), wqb=flat(wqb), wk_t=jnp.swapaxes(flat(wk), 1, 2), wv=flat(wv),
                wb=bf(w_branch), wo=bf(w_out), wu=bf(w_up), wd=bf(w_down))


def kernel(x, positions, norm_mix_g, w_in, mla_q_norm_g, mla_w_uq, mla_kv_norm_g, mla_w_ukv, ret_norm_g, w_branch, w_out, norm_mlp_g, w_up, w_down, final_norm_g):
    b, s, d = x.shape
    depth = w_in.shape[0]
    t = b * s
    assert d == D_MODEL and s % CHUNK == 0
    tm_proj = _pick(t, 1024)
    tm_row = _pick(t, 512)
    t_mla = _pick(s, 1024)
    t_sb = max(_pick(s, 256), SB_SUB)
    w = _prep_weights(w_in, mla_w_uq, mla_w_ukv, w_branch, w_out, w_up, w_down)
    mla_cos, mla_sin, ret_cos, ret_sin = _rope_tables(positions)
    vec = lambda a: a.reshape(1, -1)

    x2d = x.reshape(t, d)
    for l in range(depth):
        g_mix = vec(norm_mix_g[l])
        lat, proj_ret, proj_bf = _in_proj(x2d, g_mix, w["w_in"][l], tm_row)
        proj_bf = proj_bf.reshape(b, s, -1)

        q_a, k_t, v_a = _mla_prep(lat, mla_cos, mla_sin, vec(mla_q_norm_g[l]), vec(mla_kv_norm_g[l]),
                                  w["wqa"][l], w["wqb"][l], w["wk_t"][l], w["wv"][l], b, tm_row)
        y_a = _mla_attn(q_a.reshape(b, s, -1), k_t, v_a.reshape(b, s, -1), t_mla, 4)
        y_b = _sb_attn(proj_bf, t_sb, 8)
        y_c = _retention(proj_ret.reshape(b, s, -1), proj_bf, ret_cos, ret_sin, vec(ret_norm_g[l]),
                         _pick(s // CHUNK, 4))

        x2d = _merge(x2d, g_mix, y_a.reshape(t, -1), y_b.reshape(t, -1), y_c.reshape(t, -1),
                     w["w_gate"][l], w["wb"][l], w["wo"][l], tm_row)
        x2d = _mlp(x2d, vec(norm_mlp_g[l]), w["wu"][l], w["wd"][l], vec(final_norm_g),
                   l == depth - 1, tm_proj, 1024)
    return x2d.reshape(b, s, d)
```

```python
import functools

import numpy as np
import jax
import jax.numpy as jnp
from jax import lax
from jax.experimental import pallas as pl
from jax.experimental.pallas import tpu as pltpu

F32 = jnp.float32
BF16 = jnp.bfloat16

D_MODEL = 1024
EPS = 1e-6
ROPE_BASE = 10000.0
CHUNK = 128

MLA_HEADS = 8
MLA_NOPE = 64
MLA_ROPE = 32
MLA_V = 64
MLA_Q_RANK = 384
MLA_KV_RANK = 256
SB_HEADS = 8
SB_DIM = 64
RET_HEADS = 8
RET_DK = 64
RET_DV = 64
BRANCH_WIDTH = 512
N_BRANCH = 3
D_FF = 4 * D_MODEL

LANES = 128
LAT_WIDTH = MLA_Q_RANK + MLA_KV_RANK + 2 * LANES
VMEM_LIMIT = 56 * 1024 * 1024
LOG2E = 1.4426950408889634
MLA_Q_SCALE = (MLA_NOPE + MLA_ROPE) ** -0.5 * LOG2E
SB_SUB = 256
SB_SKIP_LOG2 = -160.0

NT_DIMS = (((1,), (1,)), ((), ()))
TN_DIMS = (((0,), (0,)), ((), ()))


def _dot(a, b):
    return jnp.dot(a, b, preferred_element_type=F32)


def _dot_nt(a, b):
    return lax.dot_general(a, b, NT_DIMS, preferred_element_type=F32)


def _dot_hilo(x, m_bf16):
    hi = x.astype(BF16)
    lo = (x - hi.astype(F32)).astype(BF16)
    return _dot(hi, m_bf16) + _dot(lo, m_bf16)


def _rep_lanes(x, n):
    return x if n == 1 else jnp.concatenate([x] * n, axis=1)


def _rms(x, g):
    return x * lax.rsqrt(jnp.mean(x * x, axis=-1, keepdims=True) + EPS) * g


def _params(sem):
    return pltpu.CompilerParams(dimension_semantics=sem, vmem_limit_bytes=VMEM_LIMIT)


IN_GROUPS = ((LAT_WIDTH, F32),
             (3 * BRANCH_WIDTH, F32),
             (4 * BRANCH_WIDTH, BF16))
IN_CHUNK = 512


def _in_proj_kernel(x_ref, g_ref, w_ref, *o_refs):
    h = _rms(x_ref[...], g_ref[...]).astype(BF16)
    col = 0
    for o_ref, (width, _) in zip(o_refs, IN_GROUPS):
        for c0 in range(0, width, IN_CHUNK):
            c1 = min(c0 + IN_CHUNK, width)
            o_ref[:, c0:c1] = _dot(h, w_ref[:, col + c0:col + c1]).astype(o_ref.dtype)
        col += width


def _in_proj(x2d, g, w, tm):
    t, d = x2d.shape
    return pl.pallas_call(
        _in_proj_kernel,
        grid=(t // tm,),
        in_specs=[pl.BlockSpec((tm, d), lambda i: (i, 0)),
                  pl.BlockSpec((1, d), lambda i: (0, 0)),
                  pl.BlockSpec(w.shape, lambda i: (0, 0), pipeline_mode=pl.Buffered(1))],
        out_specs=[pl.BlockSpec((tm, width), lambda i: (i, 0)) for width, _ in IN_GROUPS],
        out_shape=[jax.ShapeDtypeStruct((t, width), dt) for width, dt in IN_GROUPS],
        compiler_params=_params(("parallel",)),
        name="in_proj",
    )(x2d, g, w)


def _mla_prep_kernel(lat_ref, cq_ref, sq_ref, gq_ref, gkv_ref, wqa_ref, wqb_ref, wk_ref, wv_ref,
                     q_ref, k_ref, v_ref):
    lat = lat_ref[...]
    c_q = lat[:, :MLA_Q_RANK]
    c_kv = lat[:, MLA_Q_RANK:MLA_Q_RANK + MLA_KV_RANK]
    kpe_a = lat[:, MLA_Q_RANK + MLA_KV_RANK:MLA_Q_RANK + MLA_KV_RANK + LANES]
    kpe_b = lat[:, MLA_Q_RANK + MLA_KV_RANK + LANES:]
    cos_t = cq_ref[...]
    sin_t = sq_ref[...]
    nq = _rms(c_q, gq_ref[...]).astype(BF16)
    nkv = _rms(c_kv, gkv_ref[...]).astype(BF16)
    q = (_dot(nq, wqa_ref[...]) * _rep_lanes(cos_t, MLA_HEADS)
         + _dot(nq, wqb_ref[...]) * _rep_lanes(sin_t, MLA_HEADS))
    q_ref[...] = (q * MLA_Q_SCALE).astype(BF16)
    k_rot_t = jnp.transpose(kpe_a * cos_t + kpe_b * sin_t)
    k_t = _dot_nt(wk_ref[...], nkv) + jnp.concatenate([k_rot_t] * MLA_HEADS, axis=0)
    k_ref[0] = k_t.astype(BF16)
    v_ref[...] = _dot(nkv, wv_ref[...]).astype(BF16)


def _mla_prep(lat, cos_t, sin_t, gq, gkv, wqa, wqb, wk_t, wv, b, tm):
    t = lat.shape[0]
    hw = MLA_HEADS * LANES
    nsb = t // b // tm
    full = lambda a: pl.BlockSpec(a.shape, lambda i: (0,) * a.ndim)
    row = lambda w: pl.BlockSpec((tm, w), lambda i: (i, 0))
    return pl.pallas_call(
        _mla_prep_kernel,
        grid=(t // tm,),
        in_specs=[row(LAT_WIDTH), row(LANES), row(LANES), full(gq), full(gkv),
                  full(wqa), full(wqb), full(wk_t), full(wv)],
        out_specs=[row(hw), pl.BlockSpec((1, hw, tm), lambda i: (i // nsb, 0, i % nsb)),
                   row(MLA_HEADS * MLA_V)],
        out_shape=[jax.ShapeDtypeStruct((t, hw), BF16), jax.ShapeDtypeStruct((b, hw, t // b), BF16),
                   jax.ShapeDtypeStruct((t, MLA_HEADS * MLA_V), BF16)],
        compiler_params=_params(("parallel",)),
        name="mla_prep",
    )(lat, cos_t, sin_t, gq, gkv, wqa, wqb, wk_t, wv)


def _mla_attn_kernel(q_ref, kt_ref, v_ref, o_ref, m_ref, l_ref, acc_ref, *, tq, tk, nheads):
    i = pl.program_id(2)
    npairs = nheads // 2
    heads = [slice(e * LANES, (e + 1) * LANES) for e in range(nheads)]

    def tiles(j, r0, nr, masked):
        ks = pl.multiple_of(j * tk, tk)
        rows = slice(r0, r0 + nr)
        scores = [_dot(q_ref[0, rows, heads[e]], kt_ref[0, heads[e], pl.ds(ks, tk)]) for e in range(nheads)]
        if masked:
            q_pos = i * tq + r0 + lax.broadcasted_iota(jnp.int32, (nr, tk), 0)
            k_pos = ks + lax.broadcasted_iota(jnp.int32, (nr, tk), 1)
            scores = [jnp.where(k_pos <= q_pos, s, -jnp.inf) for s in scores]
        probs, alphas = [], []
        for e, s in enumerate(scores):
            m_prev = m_ref[e, rows]
            m_next = jnp.maximum(m_prev, jnp.max(s, axis=1, keepdims=True))
            alpha = jnp.exp2(m_prev - m_next)
            p = jnp.exp2(s - _rep_lanes(m_next, tk // LANES))
            l_ref[e, rows] = alpha * l_ref[e, rows] + jnp.sum(p, axis=1, keepdims=True)
            m_ref[e, rows] = m_next
            probs.append(p.astype(BF16))
            alphas.append(alpha)
        for p in range(npairs):
            pv = _dot(jnp.concatenate([probs[2 * p], probs[2 * p + 1]], axis=0), v_ref[0, pl.ds(ks, tk), heads[p]])
            for e in range(2):
                arows = slice(e * tq + r0, e * tq + r0 + nr)
                acc_ref[p, arows] = alphas[2 * p + e] * acc_ref[p, arows] + pv[e * nr:(e + 1) * nr]

    m_ref[...] = jnp.full(m_ref.shape, -jnp.inf, F32)
    l_ref[...] = jnp.zeros(l_ref.shape, F32)
    acc_ref[...] = jnp.zeros(acc_ref.shape, F32)

    def body(j, carry):
        tiles(j, 0, tq, False)
        return carry

    lax.fori_loop(0, 2 * i, body, 0)
    tiles(2 * i, 0, tq, True)
    tiles(2 * i + 1, tk, tq - tk, True)
    lane = lax.broadcasted_iota(jnp.int32, (tq, LANES), 1)
    for p in range(npairs):
        o_ref[0, :, heads[p]] = jnp.where(
            lane < MLA_V, acc_ref[p, :tq] / l_ref[2 * p], acc_ref[p, tq:] / l_ref[2 * p + 1]).astype(o_ref.dtype)


def _mla_attn(q, k_t, v, tq, nheads):
    b, s, _ = q.shape
    tk = tq // 2
    return pl.pallas_call(
        functools.partial(_mla_attn_kernel, tq=tq, tk=tk, nheads=nheads),
        grid=(b, MLA_HEADS // nheads, s // tq),
        in_specs=[pl.BlockSpec((1, tq, nheads * LANES), lambda bi, p, i: (bi, i, p)),
                  pl.BlockSpec((1, nheads * LANES, s), lambda bi, p, i: (bi, p, 0)),
                  pl.BlockSpec((1, s, nheads * MLA_V), lambda bi, p, i: (bi, 0, p))],
        out_specs=pl.BlockSpec((1, tq, nheads * MLA_V), lambda bi, p, i: (bi, i, p)),
        out_shape=jax.ShapeDtypeStruct((b, s, MLA_HEADS * MLA_V), BF16),
        scratch_shapes=[pltpu.VMEM((nheads, tq, LANES), F32), pltpu.VMEM((nheads, tq, LANES), F32),
                        pltpu.VMEM((nheads // 2, 2 * tq, LANES), F32)],
        compiler_params=_params(("parallel", "parallel", "arbitrary")),
        name="mla_attn",
    )(q, k_t, v)


def _sb_attn_kernel(q_ref, k_ref, v_ref, u_ref, o_ref, qm_ref, c_ref, acc_ref, *, tq, nheads):
    i = pl.program_id(2)
    npairs = nheads // 2
    lane = lax.broadcasted_iota(jnp.int32, (tq, LANES), 1)
    zero = jnp.zeros((tq, LANES), BF16)
    for p in range(npairs):
        q_pair = q_ref[0, :, p * LANES:(p + 1) * LANES]
        qm_ref[p, :tq] = jnp.where(lane < SB_DIM, q_pair, zero)
        qm_ref[p, tq:] = jnp.where(lane < SB_DIM, zero, q_pair)

    def tiles(j, diag):
        ks = pl.multiple_of(j * tq, tq)
        pairs = [slice(p * LANES, (p + 1) * LANES) for p in range(npairs)]
        if diag:
            row = lax.broadcasted_iota(jnp.int32, (2 * tq, tq), 0)
            col = lax.broadcasted_iota(jnp.int32, (2 * tq, tq), 1)
            before = col < jnp.where(row < tq, row, row - tq)
        zs = [_dot_nt(qm_ref[p], k_ref[0, pl.ds(ks, tq), pairs[p]]) for p in range(npairs)]
        log_betas, log_nots, splits = [], [], []
        for z in zs:
            log_beta = jnp.minimum(z, 0.0) - jnp.log(1.0 + jnp.exp2(-jnp.abs(z))) * LOG2E
            log_not = log_beta - z
            if diag:
                log_not = jnp.where(before, log_not, 0.0)
            hi = log_not.astype(BF16)
            lo = (log_not - hi.astype(F32)).astype(BF16)
            log_betas.append(log_beta)
            log_nots.append(log_not)
            splits.append(jnp.concatenate([hi, lo], axis=1))
        withins = [_dot(hilo, u_ref[...]) for hilo in splits]
        weights = []
        for p in range(npairs):
            c = c_ref[p]
            a = jnp.exp2(log_betas[p] + withins[p] + _rep_lanes(c, tq // LANES))
            if diag:
                a = jnp.where(before, a, 0.0)
            weights.append(a.astype(BF16))
            c_ref[p] = c + jnp.sum(log_nots[p], axis=1, keepdims=True)
        for p in range(npairs):
            acc_ref[p] += _dot(weights[p], v_ref[0, pl.ds(ks, tq), pairs[p]])

    c_ref[...] = jnp.zeros(c_ref.shape, F32)
    acc_ref[...] = jnp.zeros(acc_ref.shape, F32)
    tiles(i, True)

    def live():
        return (jnp.max(c_ref[...]) > SB_SKIP_LOG2).astype(jnp.int32)

    def cond(carry):
        jj, go = carry
        return jnp.logical_and(jj < i, go > 0)

    def body(carry):
        jj, _ = carry
        tiles(i - 1 - jj, False)
        return jj + 1, live()

    lax.while_loop(cond, body, (jnp.int32(0), live()))
    for p in range(npairs):
        o_ref[0, :, p * LANES:(p + 1) * LANES] = jnp.where(
            lane < SB_DIM, acc_ref[p, :tq], acc_ref[p, tq:]).astype(o_ref.dtype)


def _sb_attn(proj_bf, tq, nheads):
    b, s, _ = proj_bf.shape
    assert tq == SB_SUB
    ngroup = SB_HEADS // nheads
    width = nheads * SB_DIM
    idx = np.arange(SB_SUB)
    tri = (idx[:, None] > idx[None, :]).astype(np.float32)
    u = jnp.asarray(np.concatenate([tri, tri], axis=0), dtype=BF16)
    return pl.pallas_call(
        functools.partial(_sb_attn_kernel, tq=tq, nheads=nheads),
        grid=(b, ngroup, s // tq),
        in_specs=[pl.BlockSpec((1, tq, width), lambda bi, p, i: (bi, i, p)),
                  pl.BlockSpec((1, s, width), lambda bi, p, i: (bi, 0, ngroup + p)),
                  pl.BlockSpec((1, s, width), lambda bi, p, i: (bi, 0, 2 * ngroup + p)),
                  pl.BlockSpec(u.shape, lambda bi, p, i: (0, 0))],
        out_specs=pl.BlockSpec((1, tq, width), lambda bi, p, i: (bi, i, p)),
        out_shape=jax.ShapeDtypeStruct((b, s, SB_HEADS * SB_DIM), BF16),
        scratch_shapes=[pltpu.VMEM((nheads // 2, 2 * tq, LANES), BF16),
                        pltpu.VMEM((nheads // 2, 2 * tq, LANES), F32),
                        pltpu.VMEM((nheads // 2, 2 * tq, LANES), F32)],
        compiler_params=_params(("parallel", "parallel", "arbitrary")),
        name="sb_attn",
    )(proj_bf, proj_bf, proj_bf, u)


def _retention_kernel(q_ref, k_ref, g_ref, v_ref, cos_ref, sin_ref, dec_ref, zeta_ref, xi_ref,
                      cd_ref, bd_ref, avg_ref, gn_ref, o_ref, state_ref, *, nchunk):
    @pl.when(pl.program_id(1) == 0)
    def _():
        state_ref[...] = jnp.zeros(state_ref.shape, F32)

    rows = nchunk * CHUNK
    width = RET_HEADS * RET_DK
    npair = RET_HEADS // 2
    half = RET_DK // 2
    lane_w = lax.broadcasted_iota(jnp.int32, (rows, width), 1)
    first_half = (lane_w % RET_DK) < half
    head0 = (lane_w % LANES) < RET_DK
    cos_t = _rep_lanes(cos_ref[0], width // LANES)
    sin_t = _rep_lanes(sin_ref[0], width // LANES)

    def rope(x):
        swapped = jnp.where(first_half, pltpu.roll(x, width - half, 1), pltpu.roll(x, half, 1))
        return x * cos_t + swapped * sin_t

    q = rope(q_ref[0])
    k = rope(k_ref[0])
    kz = (k * zeta_ref[...]).astype(BF16)
    qb = q.astype(BF16)
    kb = k.astype(BF16)
    zero = jnp.zeros_like(qb)
    q0 = jnp.where(head0, qb, zero)
    q1 = jnp.where(head0, zero, qb)
    v = v_ref[0]
    units = [(c, p) for c in range(nchunk) for p in range(npair)]
    rs = lambda c: slice(c * CHUNK, (c + 1) * CHUNK)
    ls = lambda p: slice(p * LANES, (p + 1) * LANES)
    scores = [_dot_nt(jnp.concatenate([q0[rs(c), ls(p)], q1[rs(c), ls(p)]], axis=0), kb[rs(c), ls(p)])
              for c, p in units]
    scores = [(s * dec_ref[p]).astype(BF16) for s, (c, p) in zip(scores, units)]
    intra = [_dot(s, v[rs(c), ls(p)]) for s, (c, p) in zip(scores, units)]
    kvs = [lax.dot_general(kz[rs(c), ls(p)], v[rs(c), ls(p)], TN_DIMS, preferred_element_type=F32)
           for c, p in units]
    lane = lax.broadcasted_iota(jnp.int32, (CHUNK, LANES), 1)
    chunks = []
    for c in range(nchunk):
        ys = []
        for p in range(npair):
            u = c * npair + p
            state = state_ref[p]
            y_cross = _dot(qb[rs(c), ls(p)], state.astype(BF16)) * xi_ref[:, ls(p)]
            state_ref[p] = state * cd_ref[p] + kvs[u] * bd_ref[...]
            y_intra = jnp.where(lane < RET_DV, intra[u][:CHUNK], intra[u][CHUNK:])
            ys.append(y_intra + y_cross)
        chunks.append(jnp.concatenate(ys, axis=1))
    y = jnp.concatenate(chunks, axis=0)
    avg = avg_ref[...]
    d = y - _dot_hilo(y, avg)
    var = _dot_hilo(d * d, avg)
    yn = d * lax.rsqrt(var + EPS) * gn_ref[...]
    g = g_ref[0]
    o_ref[0] = (g * jax.nn.sigmoid(g) * yn).astype(o_ref.dtype)


def _retention_consts(nchunk):
    h = np.arange(RET_HEADS, dtype=np.float64)
    log_gamma = np.log1p(-np.exp2(-5.0 - h))
    idx = np.arange(CHUNK, dtype=np.float64)
    diff = idx[:, None] - idx[None, :]
    dec = np.where(diff[None] >= 0, np.exp(np.maximum(diff, 0.0)[None] * log_gamma[:, None, None]), 0.0)
    npair = RET_HEADS // 2
    dec = dec.reshape(npair, 2 * CHUNK, CHUNK)
    zeta = np.exp((CHUNK - 1 - idx)[:, None] * log_gamma[None, :])
    xi = np.exp((idx + 1.0)[:, None] * log_gamma[None, :])
    zeta_l = np.tile(np.repeat(zeta, RET_DK, axis=1), (nchunk, 1))
    xi_l = np.repeat(xi, RET_DV, axis=1)
    chunk_decay = np.exp(CHUNK * log_gamma)
    head_of = np.arange(LANES) // RET_DK
    bd = (head_of[:, None] == head_of[None, :]).astype(np.float64)
    cd = np.stack([bd * chunk_decay[2 * p + head_of][:, None] for p in range(npair)])
    gh = np.arange(RET_HEADS * RET_DV) // RET_DV
    avg = (gh[:, None] == gh[None, :]).astype(np.float64) / RET_DV
    f = lambda a: jnp.asarray(a, dtype=F32)
    return f(dec), f(zeta_l), f(xi_l), f(cd), f(bd), jnp.asarray(avg, dtype=BF16)


def _retention(proj_ret, proj_bf, cos_t, sin_t, gn, nchunk):
    b, s, _ = proj_ret.shape
    width = RET_HEADS * RET_DK
    rows = nchunk * CHUNK
    dec, zeta_l, xi_l, cd, bd, avg = _retention_consts(nchunk)
    full = lambda a: pl.BlockSpec(a.shape, lambda bi, n: (0,) * a.ndim)
    blk = lambda c: pl.BlockSpec((1, rows, width), lambda bi, n: (bi, n, c))
    tab = pl.BlockSpec((1, rows, LANES), lambda bi, n: (bi, n, 0))
    return pl.pallas_call(
        functools.partial(_retention_kernel, nchunk=nchunk),
        grid=(b, s // rows),
        in_specs=[blk(0), blk(1), blk(2), blk(3), tab, tab,
                  full(dec), full(zeta_l), full(xi_l), full(cd), full(bd), full(avg), full(gn)],
        out_specs=blk(0),
        out_shape=jax.ShapeDtypeStruct((b, s, RET_HEADS * RET_DV), BF16),
        scratch_shapes=[pltpu.VMEM((RET_HEADS // 2, LANES, LANES), F32)],
        compiler_params=_params(("parallel", "arbitrary")),
        name="retention",
    )(proj_ret, proj_ret, proj_ret, proj_bf, cos_t, sin_t, dec, zeta_l, xi_l, cd, bd, avg, gn)


def _merge_kernel(x_ref, g_ref, ya_ref, yb_ref, yc_ref, wg_ref, wb_ref, wo_ref, o_ref):
    x = x_ref[...]
    h = _rms(x, g_ref[...]).astype(BF16)
    merged = None
    for n, y_ref in enumerate((ya_ref, yb_ref, yc_ref)):
        gate = jax.nn.sigmoid(_dot(h, wg_ref[:, n * D_MODEL:(n + 1) * D_MODEL]))
        term = gate * _dot(y_ref[...], wb_ref[n])
        merged = term if merged is None else merged + term
    o_ref[...] = x + _dot(merged.astype(BF16), wo_ref[...])


def _merge(x2d, g, ya, yb, yc, wg, wb, wo, tm):
    t = x2d.shape[0]
    full = lambda a: pl.BlockSpec(a.shape, lambda i: (0,) * a.ndim, pipeline_mode=pl.Buffered(1))
    row = lambda w: pl.BlockSpec((tm, w), lambda i: (i, 0))
    return pl.pallas_call(
        _merge_kernel,
        grid=(t // tm,),
        in_specs=[row(D_MODEL), full(g), row(BRANCH_WIDTH), row(BRANCH_WIDTH), row(BRANCH_WIDTH),
                  full(wg), full(wb), full(wo)],
        out_specs=row(D_MODEL),
        out_shape=jax.ShapeDtypeStruct((t, D_MODEL), F32),
        compiler_params=_params(("parallel",)),
        name="merge",
    )(x2d, g, ya, yb, yc, wg, wb, wo)


MLP_CHUNK = 1024


def _mlp_kernel(x_ref, g_ref, wu_ref, wd_ref, gf_ref, o_ref, *, final_norm):
    x = x_ref[...]
    h = _rms(x, g_ref[...]).astype(BF16)
    acc = None
    for c0 in range(0, D_FF, MLP_CHUNK):
        u = jnp.maximum(_dot(h, wu_ref[:, c0:c0 + MLP_CHUNK]), 0.0)
        part = _dot((u * u).astype(BF16), wd_ref[c0:c0 + MLP_CHUNK, :])
        acc = part if acc is None else acc + part
    y = x + acc
    o_ref[...] = _rms(y, gf_ref[...]) if final_norm else y


def _mlp(x2d, g, wu, wd, gf, final_norm, tm):
    t, d = x2d.shape
    full = lambda a: pl.BlockSpec(a.shape, lambda i: (0,) * a.ndim, pipeline_mode=pl.Buffered(1))
    row = pl.BlockSpec((tm, d), lambda i: (i, 0))
    return pl.pallas_call(
        functools.partial(_mlp_kernel, final_norm=final_norm),
        grid=(t // tm,),
        in_specs=[row, full(g), full(wu), full(wd), full(gf)],
        out_specs=row,
        out_shape=jax.ShapeDtypeStruct((t, d), F32),
        compiler_params=_params(("parallel",)),
        name="mlp",
    )(x2d, g, wu, wd, gf)


def _pick(n, pref):
    t = min(pref, n)
    while n % t:
        t //= 2
    return t


def _rope_tables(positions):
    pos = positions.astype(F32)[..., None]
    b, s = positions.shape
    lane = np.arange(LANES)

    def cs(dim, freq_of_lane):
        inv_freq = ROPE_BASE ** (-jnp.arange(0, dim, 2, dtype=F32) / dim)
        ang = pos * inv_freq[freq_of_lane]
        return jnp.cos(ang), jnp.sin(ang)

    hm = MLA_ROPE // 2
    is_rope = (lane >= MLA_NOPE) & (lane < MLA_NOPE + MLA_ROPE)
    cm, sm = cs(MLA_ROPE, np.where(is_rope, (lane - MLA_NOPE) % hm, 0))
    sign_m = np.where(lane < MLA_NOPE + hm, -1.0, 1.0).astype(np.float32)
    mla_cos = jnp.where(lane < MLA_NOPE, 1.0, jnp.where(is_rope, cm, 0.0)).reshape(b * s, LANES)
    mla_sin = jnp.where(is_rope, sign_m * sm, 0.0).reshape(b * s, LANES)
    hr = RET_DK // 2
    ret_cos, sr = cs(RET_DK, lane % hr)
    ret_sin = np.where(lane % RET_DK < hr, -1.0, 1.0).astype(np.float32) * sr
    return mla_cos, mla_sin, ret_cos, ret_sin


def _prep_weights(w_in, mla_w_uq, mla_w_ukv, w_branch, w_out, w_up, w_down):
    depth, d, _ = w_in.shape
    offs = [int(o) for o in np.cumsum((0, MLA_Q_RANK, MLA_KV_RANK, MLA_ROPE) + (BRANCH_WIDTH,) * 7
                                      + (N_BRANCH * D_MODEL,))]
    col_scale = np.ones(offs[-1], np.float32)
    col_scale[offs[3]:offs[4]] = SB_DIM ** -0.5 * LOG2E
    col_scale[offs[7]:offs[8]] = RET_DK ** -0.5
    w = (w_in * col_scale).astype(BF16)
    seg = lambda n: w[:, :, offs[n]:offs[n + 1]]
    c_q, c_kv, k_pe, sb_q, sb_k, sb_v, r_q, r_k, r_v, r_g, gates = (seg(n) for n in range(11))
    zeros = lambda *shape: jnp.zeros(shape, BF16)
    hr = MLA_ROPE // 2
    pad = LANES - MLA_NOPE - MLA_ROPE
    w_proj = jnp.concatenate(
        [c_q, c_kv,
         zeros(depth, d, MLA_NOPE), k_pe, zeros(depth, d, pad),
         zeros(depth, d, MLA_NOPE), k_pe[..., hr:], k_pe[..., :hr], zeros(depth, d, pad),
         r_q, r_k, r_g,
         sb_q, sb_k, sb_v, r_v], axis=-1)

    uq = mla_w_uq.astype(BF16).reshape(depth, MLA_Q_RANK, MLA_HEADS, MLA_NOPE + MLA_ROPE)
    zq = lambda width: jnp.zeros((depth, MLA_Q_RANK, MLA_HEADS, width), BF16)
    nope, rope = uq[..., :MLA_NOPE], uq[..., MLA_NOPE:]
    wqa = jnp.concatenate([nope, rope, zq(pad)], axis=-1)
    wqb = jnp.concatenate([zq(MLA_NOPE), rope[..., hr:], rope[..., :hr], zq(pad)], axis=-1)
    ukv = mla_w_ukv.astype(BF16).reshape(depth, MLA_KV_RANK, MLA_HEADS, MLA_NOPE + MLA_V)
    wk = jnp.concatenate([ukv[..., :MLA_NOPE],
                          jnp.zeros((depth, MLA_KV_RANK, MLA_HEADS, LANES - MLA_NOPE), BF16)], axis=-1)
    wv = ukv[..., MLA_NOPE:]
    flat = lambda a: a.reshape(depth, a.shape[1], -1)
    bf = lambda a: a.astype(BF16)
    return dict(w_in=w_proj, w_gate=gates,
                wqa=flat(wqa), wqb=flat(wqb), wk_t=jnp.swapaxes(flat(wk), 1, 2), wv=flat(wv),
                wb=bf(w_branch), wo=bf(w_out), wu=bf(w_up), wd=bf(w_down))


def kernel(x, positions, norm_mix_g, w_in, mla_q_norm_g, mla_w_uq, mla_kv_norm_g, mla_w_ukv, ret_norm_g, w_branch, w_out, norm_mlp_g, w_up, w_down, final_norm_g):
    b, s, d = x.shape
    depth = w_in.shape[0]
    t = b * s
    assert d == D_MODEL and s % CHUNK == 0
    tm_row = _pick(t, 512)
    t_mla = _pick(s, 1024)
    t_sb = max(_pick(s, 256), SB_SUB)
    w = _prep_weights(w_in, mla_w_uq, mla_w_ukv, w_branch, w_out, w_up, w_down)
    mla_cos, mla_sin, ret_cos, ret_sin = _rope_tables(positions)
    vec = lambda a: a.reshape(1, -1)

    x2d = x.reshape(t, d)
    for l in range(depth):
        g_mix = vec(norm_mix_g[l])
        lat, proj_ret, proj_bf = _in_proj(x2d, g_mix, w["w_in"][l], tm_row)
        proj_bf = proj_bf.reshape(b, s, -1)

        q_a, k_t, v_a = _mla_prep(lat, mla_cos, mla_sin, vec(mla_q_norm_g[l]), vec(mla_kv_norm_g[l]),
                                  w["wqa"][l], w["wqb"][l], w["wk_t"][l], w["wv"][l], b, tm_row)
        y_a = _mla_attn(q_a.reshape(b, s, -1), k_t, v_a.reshape(b, s, -1), t_mla, 4)
        y_b = _sb_attn(proj_bf, t_sb, 8)
        y_c = _retention(proj_ret.reshape(b, s, -1), proj_bf, ret_cos, ret_sin, vec(ret_norm_g[l]),
                         _pick(s // CHUNK, 4))

        x2d = _merge(x2d, g_mix, y_a.reshape(t, -1), y_b.reshape(t, -1), y_c.reshape(t, -1),
                     w["w_gate"][l], w["wb"][l], w["wo"][l], tm_row)
        x2d = _mlp(x2d, vec(norm_mlp_g[l]), w["wu"][l], w["wd"][l], vec(final_norm_g),
                   l == depth - 1, tm_row)
    return x2d.reshape(b, s, d)
```

```python
import functools

import numpy as np
import jax
import jax.numpy as jnp
from jax import lax
from jax.experimental import pallas as pl
from jax.experimental.pallas import tpu as pltpu

F32 = jnp.float32
BF16 = jnp.bfloat16

D_MODEL = 1024
EPS = 1e-6
ROPE_BASE = 10000.0
CHUNK = 128

MLA_HEADS = 8
MLA_NOPE = 64
MLA_ROPE = 32
MLA_V = 64
MLA_Q_RANK = 384
MLA_KV_RANK = 256
SB_HEADS = 8
SB_DIM = 64
RET_HEADS = 8
RET_DK = 64
RET_DV = 64
BRANCH_WIDTH = 512
N_BRANCH = 3
D_FF = 4 * D_MODEL

LANES = 128
LAT_WIDTH = MLA_Q_RANK + MLA_KV_RANK + 2 * LANES
VMEM_LIMIT = 56 * 1024 * 1024
LOG2E = 1.4426950408889634
MLA_Q_SCALE = (MLA_NOPE + MLA_ROPE) ** -0.5 * LOG2E
SIGN_BIT = np.uint32(0x80000000)
SB_MASKED_LOGIT = -1e30
SB_SUB = 256
SB_SKIP_LOG2 = -160.0

NT_DIMS = (((1,), (1,)), ((), ()))
TN_DIMS = (((0,), (0,)), ((), ()))


def _dot(a, b):
    return jnp.dot(a, b, preferred_element_type=F32)


def _dot_nt(a, b):
    return lax.dot_general(a, b, NT_DIMS, preferred_element_type=F32)


def _dot_hilo(x, m_bf16):
    hi = x.astype(BF16)
    lo = (x - hi.astype(F32)).astype(BF16)
    return _dot(hi, m_bf16) + _dot(lo, m_bf16)


def _rep_lanes(x, n):
    return x if n == 1 else jnp.concatenate([x] * n, axis=1)


def _rms(x, g):
    return x * lax.rsqrt(jnp.mean(x * x, axis=-1, keepdims=True) + EPS) * g


def _layer_spec(stacked, l):
    nd = stacked.ndim - 1
    return pl.BlockSpec((None,) + stacked.shape[1:], lambda *_: (l,) + (0,) * nd, pipeline_mode=pl.Buffered(1))


def _params(sem):
    return pltpu.CompilerParams(dimension_semantics=sem, vmem_limit_bytes=VMEM_LIMIT)


IN_GROUPS = ((LAT_WIDTH, F32),
             (3 * BRANCH_WIDTH, F32),
             (4 * BRANCH_WIDTH, BF16))
IN_CHUNK = 512


def _in_proj_kernel(x_ref, g_ref, w_ref, *o_refs):
    h = _rms(x_ref[...], g_ref[...]).astype(BF16)
    col = 0
    for o_ref, (width, _) in zip(o_refs, IN_GROUPS):
        for c0 in range(0, width, IN_CHUNK):
            c1 = min(c0 + IN_CHUNK, width)
            o_ref[:, c0:c1] = _dot(h, w_ref[:, col + c0:col + c1]).astype(o_ref.dtype)
        col += width


def _in_proj(x2d, g, w, l, tm):
    t, d = x2d.shape
    return pl.pallas_call(
        _in_proj_kernel,
        grid=(t // tm,),
        in_specs=[pl.BlockSpec((tm, d), lambda i: (i, 0)),
                  _layer_spec(g, l), _layer_spec(w, l)],
        out_specs=[pl.BlockSpec((tm, width), lambda i: (i, 0)) for width, _ in IN_GROUPS],
        out_shape=[jax.ShapeDtypeStruct((t, width), dt) for width, dt in IN_GROUPS],
        compiler_params=_params(("parallel",)),
        name="in_proj",
    )(x2d, g, w)


def _mla_prep_kernel(lat_ref, cq_ref, sq_ref, gq_ref, gkv_ref, wqa_ref, wqb_ref, wk_ref, wv_ref,
                     q_ref, k_ref, v_ref):
    lat = lat_ref[...]
    c_q = lat[:, :MLA_Q_RANK]
    c_kv = lat[:, MLA_Q_RANK:MLA_Q_RANK + MLA_KV_RANK]
    kpe_a = lat[:, MLA_Q_RANK + MLA_KV_RANK:MLA_Q_RANK + MLA_KV_RANK + LANES]
    kpe_b = lat[:, MLA_Q_RANK + MLA_KV_RANK + LANES:]
    cos_t = cq_ref[...]
    sin_t = sq_ref[...]
    nq = _rms(c_q, gq_ref[...]).astype(BF16)
    nkv = _rms(c_kv, gkv_ref[...]).astype(BF16)
    q = (_dot(nq, wqa_ref[...]) * _rep_lanes(cos_t, MLA_HEADS)
         + _dot(nq, wqb_ref[...]) * _rep_lanes(sin_t, MLA_HEADS))
    q_ref[...] = (q * MLA_Q_SCALE).astype(BF16)
    k_rot_t = jnp.transpose(kpe_a * cos_t + kpe_b * sin_t)
    k_t = _dot_nt(wk_ref[...], nkv) + jnp.concatenate([k_rot_t] * MLA_HEADS, axis=0)
    k_ref[0] = k_t.astype(BF16)
    v_ref[...] = _dot(nkv, wv_ref[...]).astype(BF16)


def _mla_prep(lat, cos_t, sin_t, gq, gkv, wqa, wqb, wk_t, wv, l, b, tm):
    t = lat.shape[0]
    hw = MLA_HEADS * LANES
    nsb = t // b // tm
    full = lambda a: _layer_spec(a, l)
    row = lambda w: pl.BlockSpec((tm, w), lambda i: (i, 0))
    return pl.pallas_call(
        _mla_prep_kernel,
        grid=(t // tm,),
        in_specs=[row(LAT_WIDTH), row(LANES), row(LANES), full(gq), full(gkv),
                  full(wqa), full(wqb), full(wk_t), full(wv)],
        out_specs=[row(hw), pl.BlockSpec((1, hw, tm), lambda i: (i // nsb, 0, i % nsb)),
                   row(MLA_HEADS * MLA_V)],
        out_shape=[jax.ShapeDtypeStruct((t, hw), BF16), jax.ShapeDtypeStruct((b, hw, t // b), BF16),
                   jax.ShapeDtypeStruct((t, MLA_HEADS * MLA_V), BF16)],
        compiler_params=_params(("parallel",)),
        name="mla_prep",
    )(lat, cos_t, sin_t, gq, gkv, wqa, wqb, wk_t, wv)


def _mla_attn_kernel(q_ref, kt_ref, v_ref, o_ref, m_ref, l_ref, acc_ref, *, tq, tk, nheads):
    i = pl.program_id(2)
    npairs = nheads // 2
    heads = [slice(e * LANES, (e + 1) * LANES) for e in range(nheads)]

    def tiles(j, r0, nr, masked):
        ks = pl.multiple_of(j * tk, tk)
        rows = slice(r0, r0 + nr)
        scores = [_dot(q_ref[0, rows, heads[e]], kt_ref[0, heads[e], pl.ds(ks, tk)]) for e in range(nheads)]
        if masked:
            q_pos = i * tq + r0 + lax.broadcasted_iota(jnp.int32, (nr, tk), 0)
            k_pos = ks + lax.broadcasted_iota(jnp.int32, (nr, tk), 1)
            scores = [jnp.where(k_pos <= q_pos, s, -jnp.inf) for s in scores]
        probs, alphas = [], []
        for e, s in enumerate(scores):
            m_prev = m_ref[e, rows]
            m_next = jnp.maximum(m_prev, jnp.max(s, axis=1, keepdims=True))
            alpha = jnp.exp2(m_prev - m_next)
            p = jnp.exp2(s - _rep_lanes(m_next, tk // LANES))
            partial = functools.reduce(jnp.add, [p[:, c:c + LANES] for c in range(0, tk, LANES)])
            l_ref[e, rows] = alpha * l_ref[e, rows] + partial
            m_ref[e, rows] = m_next
            probs.append(p.astype(BF16))
            alphas.append(alpha)
        for p in range(npairs):
            pv = _dot(jnp.concatenate([probs[2 * p], probs[2 * p + 1]], axis=0), v_ref[0, pl.ds(ks, tk), heads[p]])
            for e in range(2):
                arows = slice(e * tq + r0, e * tq + r0 + nr)
                acc_ref[p, arows] = alphas[2 * p + e] * acc_ref[p, arows] + pv[e * nr:(e + 1) * nr]

    m_ref[...] = jnp.full(m_ref.shape, -jnp.inf, F32)
    l_ref[...] = jnp.zeros(l_ref.shape, F32)
    acc_ref[...] = jnp.zeros(acc_ref.shape, F32)

    def body(j, carry):
        tiles(j, 0, tq, False)
        return carry

    lax.fori_loop(0, 2 * i, body, 0)
    tiles(2 * i, 0, tq, True)
    tiles(2 * i + 1, tk, tq - tk, True)
    lane = lax.broadcasted_iota(jnp.int32, (tq, LANES), 1)
    denom = [jnp.sum(l_ref[e], axis=1, keepdims=True) for e in range(nheads)]
    for p in range(npairs):
        o_ref[0, :, heads[p]] = jnp.where(
            lane < MLA_V, acc_ref[p, :tq] / denom[2 * p], acc_ref[p, tq:] / denom[2 * p + 1]).astype(o_ref.dtype)


def _mla_attn(q, k_t, v, tq, nheads):
    b, s, _ = q.shape
    tk = tq // 2
    return pl.pallas_call(
        functools.partial(_mla_attn_kernel, tq=tq, tk=tk, nheads=nheads),
        grid=(b, MLA_HEADS // nheads, s // tq),
        in_specs=[pl.BlockSpec((1, tq, nheads * LANES), lambda bi, p, i: (bi, i, p)),
                  pl.BlockSpec((1, nheads * LANES, s), lambda bi, p, i: (bi, p, 0)),
                  pl.BlockSpec((1, s, nheads * MLA_V), lambda bi, p, i: (bi, 0, p))],
        out_specs=pl.BlockSpec((1, tq, nheads * MLA_V), lambda bi, p, i: (bi, i, p)),
        out_shape=jax.ShapeDtypeStruct((b, s, MLA_HEADS * MLA_V), BF16),
        scratch_shapes=[pltpu.VMEM((nheads, tq, LANES), F32), pltpu.VMEM((nheads, tq, LANES), F32),
                        pltpu.VMEM((nheads // 2, 2 * tq, LANES), F32)],
        compiler_params=_params(("parallel", "parallel", "arbitrary")),
        name="mla_attn",
    )(q, k_t, v)


def _sb_attn_kernel(q_ref, k_ref, v_ref, u_ref, o_ref, qm_ref, c_ref, acc_ref, *, tq, nheads):
    i = pl.program_id(2)
    npairs = nheads // 2
    lane = lax.broadcasted_iota(jnp.int32, (tq, LANES), 1)
    zero = jnp.zeros((tq, LANES), BF16)
    for p in range(npairs):
        q_pair = q_ref[0, :, p * LANES:(p + 1) * LANES]
        qm_ref[p, :tq] = jnp.where(lane < SB_DIM, q_pair, zero)
        qm_ref[p, tq:] = jnp.where(lane < SB_DIM, zero, q_pair)

    hq = tq // 2
    pairs = [slice(p * LANES, (p + 1) * LANES) for p in range(npairs)]
    row_sets = {"all": ((0, 2 * tq),), "upper": ((0, hq), (tq, hq)), "lower": ((hq, hq), (tq + hq, hq))}

    def load(ref, p, sel):
        parts = [ref[p, s:s + n] for s, n in row_sets[sel]]
        return parts[0] if len(parts) == 1 else jnp.concatenate(parts, axis=0)

    def store(ref, p, sel, val):
        off = 0
        for s, n in row_sets[sel]:
            ref[p, s:s + n] = val[off:off + n]
            off += n

    has_left = i > 0
    row = lax.broadcasted_iota(jnp.int32, (2 * tq, tq), 0)
    col = lax.broadcasted_iota(jnp.int32, (2 * tq, tq), 1)
    before = col < jnp.where(row < tq, row, row - tq)

    def run(jobs):
        starts = [pl.multiple_of(j * tq, tq) for _, j, _, _ in jobs]
        zs = [_dot_nt(load(qm_ref, p, sel), k_ref[0, pl.ds(ks, tq), pairs[p]])
              for (p, _, sel, _), ks in zip(jobs, starts)]
        stats = []
        for z, (_, _, _, kind) in zip(zs, jobs):
            if kind == "diag":
                z = jnp.where(before, z, SB_MASKED_LOGIT)
            neg_abs = lax.bitcast_convert_type(lax.bitcast_convert_type(z, jnp.uint32) | SIGN_BIT, F32)
            log_beta = jnp.minimum(z, 0.0) - jnp.log(1.0 + jnp.exp2(neg_abs)) * LOG2E
            log_not = log_beta - z
            hi = log_not.astype(BF16)
            lo = (log_not - hi.astype(F32)).astype(BF16)
            stats.append((log_beta, log_not, jnp.concatenate([hi, lo], axis=1)))
        withins = [_dot(hilo, u_ref[...]) for _, _, hilo in stats]
        weights = []
        for (p, _, sel, kind), (log_beta, log_not, _), within in zip(jobs, stats, withins):
            c = load(c_ref, p, sel)
            if kind == "left":
                c = jnp.where(has_left, c, -jnp.inf)
            a = jnp.exp2(log_beta + within + _rep_lanes(c, tq // LANES))
            weights.append(a.astype(BF16))
            store(c_ref, p, sel, c + jnp.sum(log_not, axis=1, keepdims=True))
        for (p, _, sel, _), ks, a in zip(jobs, starts, weights):
            store(acc_ref, p, sel, load(acc_ref, p, sel) + _dot(a, v_ref[0, pl.ds(ks, tq), pairs[p]]))

    def live(sel):
        worst = functools.reduce(jnp.maximum, [jnp.max(load(c_ref, p, sel)) for p in range(npairs)])
        return worst > SB_SKIP_LOG2

    c_ref[...] = jnp.zeros(c_ref.shape, F32)
    acc_ref[...] = jnp.zeros(acc_ref.shape, F32)
    left = jnp.maximum(i - 1, 0)
    run([(p, i, "all", "diag") for p in range(npairs)] + [(p, left, "all", "left") for p in range(npairs)])

    def cond(carry):
        jj, go = carry
        return jnp.logical_and(jj < i, go > 0)

    def body(carry):
        jj, _ = carry
        run([(p, i - 1 - jj, "all", "plain") for p in range(npairs)])
        return jj + 1, live("all").astype(jnp.int32)

    lax.while_loop(cond, body, (jnp.int32(1), live("all").astype(jnp.int32)))
    for p in range(npairs):
        o_ref[0, :, p * LANES:(p + 1) * LANES] = jnp.where(
            lane < SB_DIM, acc_ref[p, :tq], acc_ref[p, tq:]).astype(o_ref.dtype)


def _sb_attn(proj_bf, tq, nheads):
    b, s, _ = proj_bf.shape
    assert tq == SB_SUB
    ngroup = SB_HEADS // nheads
    width = nheads * SB_DIM
    idx = np.arange(SB_SUB)
    tri = (idx[:, None] > idx[None, :]).astype(np.float32)
    u = jnp.asarray(np.concatenate([tri, tri], axis=0), dtype=BF16)
    return pl.pallas_call(
        functools.partial(_sb_attn_kernel, tq=tq, nheads=nheads),
        grid=(b, ngroup, s // tq),
        in_specs=[pl.BlockSpec((1, tq, width), lambda bi, p, i: (bi, i, p)),
                  pl.BlockSpec((1, s, width), lambda bi, p, i: (bi, 0, ngroup + p)),
                  pl.BlockSpec((1, s, width), lambda bi, p, i: (bi, 0, 2 * ngroup + p)),
                  pl.BlockSpec(u.shape, lambda bi, p, i: (0, 0))],
        out_specs=pl.BlockSpec((1, tq, width), lambda bi, p, i: (bi, i, p)),
        out_shape=jax.ShapeDtypeStruct((b, s, SB_HEADS * SB_DIM), BF16),
        scratch_shapes=[pltpu.VMEM((nheads // 2, 2 * tq, LANES), BF16),
                        pltpu.VMEM((nheads // 2, 2 * tq, LANES), F32),
                        pltpu.VMEM((nheads // 2, 2 * tq, LANES), F32)],
        compiler_params=_params(("parallel", "parallel", "arbitrary")),
        name="sb_attn",
    )(proj_bf, proj_bf, proj_bf, u)


def _retention_kernel(q_ref, k_ref, g_ref, v_ref, cos_ref, sin_ref, dec_ref, zeta_ref, xi_ref,
                      cd_ref, bd_ref, avg_ref, gn_ref, o_ref, state_ref, *, nchunk):
    @pl.when(pl.program_id(1) == 0)
    def _():
        state_ref[...] = jnp.zeros(state_ref.shape, F32)

    rows = nchunk * CHUNK
    width = RET_HEADS * RET_DK
    npair = RET_HEADS // 2
    half = RET_DK // 2
    lane_w = lax.broadcasted_iota(jnp.int32, (rows, width), 1)
    first_half = (lane_w % RET_DK) < half
    head0 = (lane_w % LANES) < RET_DK
    cos_t = _rep_lanes(cos_ref[0], width // LANES)
    sin_t = _rep_lanes(sin_ref[0], width // LANES)

    def rope(x):
        swapped = jnp.where(first_half, pltpu.roll(x, width - half, 1), pltpu.roll(x, half, 1))
        return x * cos_t + swapped * sin_t

    q = rope(q_ref[0])
    k = rope(k_ref[0])
    kz = (k * zeta_ref[...]).astype(BF16)
    qb = q.astype(BF16)
    kb = k.astype(BF16)
    zero = jnp.zeros_like(qb)
    q0 = jnp.where(head0, qb, zero)
    q1 = jnp.where(head0, zero, qb)
    v = v_ref[0]
    units = [(c, p) for c in range(nchunk) for p in range(npair)]
    rs = lambda c: slice(c * CHUNK, (c + 1) * CHUNK)
    ls = lambda p: slice(p * LANES, (p + 1) * LANES)
    scores = [_dot_nt(jnp.concatenate([q0[rs(c), ls(p)], q1[rs(c), ls(p)]], axis=0), kb[rs(c), ls(p)])
              for c, p in units]
    scores = [(s * dec_ref[p]).astype(BF16) for s, (c, p) in zip(scores, units)]
    intra = [_dot(s, v[rs(c), ls(p)]) for s, (c, p) in zip(scores, units)]
    kvs = [lax.dot_general(kz[rs(c), ls(p)], v[rs(c), ls(p)], TN_DIMS, preferred_element_type=F32)
           for c, p in units]
    lane = lax.broadcasted_iota(jnp.int32, (CHUNK, LANES), 1)
    chunks = []
    for c in range(nchunk):
        ys = []
        for p in range(npair):
            u = c * npair + p
            state = state_ref[p]
            y_cross = _dot(qb[rs(c), ls(p)], state.astype(BF16)) * xi_ref[:, ls(p)]
            state_ref[p] = state * cd_ref[p] + kvs[u] * bd_ref[...]
            y_intra = jnp.where(lane < RET_DV, intra[u][:CHUNK], intra[u][CHUNK:])
            ys.append(y_intra + y_cross)
        chunks.append(jnp.concatenate(ys, axis=1))
    y = jnp.concatenate(chunks, axis=0)
    avg = avg_ref[...]
    d = y - _dot_hilo(y, avg)
    var = _dot_hilo(d * d, avg)
    yn = d * lax.rsqrt(var + EPS) * gn_ref[...]
    g = g_ref[0]
    o_ref[0] = (g * jax.nn.sigmoid(g) * yn).astype(o_ref.dtype)


def _retention_consts(nchunk):
    h = np.arange(RET_HEADS, dtype=np.float64)
    log_gamma = np.log1p(-np.exp2(-5.0 - h))
    idx = np.arange(CHUNK, dtype=np.float64)
    diff = idx[:, None] - idx[None, :]
    dec = np.where(diff[None] >= 0, np.exp(np.maximum(diff, 0.0)[None] * log_gamma[:, None, None]), 0.0)
    npair = RET_HEADS // 2
    dec = dec.reshape(npair, 2 * CHUNK, CHUNK)
    zeta = np.exp((CHUNK - 1 - idx)[:, None] * log_gamma[None, :])
    xi = np.exp((idx + 1.0)[:, None] * log_gamma[None, :])
    zeta_l = np.tile(np.repeat(zeta, RET_DK, axis=1), (nchunk, 1))
    xi_l = np.repeat(xi, RET_DV, axis=1)
    chunk_decay = np.exp(CHUNK * log_gamma)
    head_of = np.arange(LANES) // RET_DK
    bd = (head_of[:, None] == head_of[None, :]).astype(np.float64)
    cd = np.stack([bd * chunk_decay[2 * p + head_of][:, None] for p in range(npair)])
    gh = np.arange(RET_HEADS * RET_DV) // RET_DV
    avg = (gh[:, None] == gh[None, :]).astype(np.float64) / RET_DV
    f = lambda a: jnp.asarray(a, dtype=F32)
    return f(dec), f(zeta_l), f(xi_l), f(cd), f(bd), jnp.asarray(avg, dtype=BF16)


def _retention(proj_ret, proj_bf, cos_t, sin_t, gn, nchunk):
    b, s, _ = proj_ret.shape
    width = RET_HEADS * RET_DK
    rows = nchunk * CHUNK
    dec, zeta_l, xi_l, cd, bd, avg = _retention_consts(nchunk)
    full = lambda a: pl.BlockSpec(a.shape, lambda bi, n: (0,) * a.ndim)
    blk = lambda c: pl.BlockSpec((1, rows, width), lambda bi, n: (bi, n, c))
    tab = pl.BlockSpec((1, rows, LANES), lambda bi, n: (bi, n, 0))
    return pl.pallas_call(
        functools.partial(_retention_kernel, nchunk=nchunk),
        grid=(b, s // rows),
        in_specs=[blk(0), blk(1), blk(2), blk(3), tab, tab,
                  full(dec), full(zeta_l), full(xi_l), full(cd), full(bd), full(avg), full(gn)],
        out_specs=blk(0),
        out_shape=jax.ShapeDtypeStruct((b, s, RET_HEADS * RET_DV), BF16),
        scratch_shapes=[pltpu.VMEM((RET_HEADS // 2, LANES, LANES), F32)],
        compiler_params=_params(("parallel", "arbitrary")),
        name="retention",
    )(proj_ret, proj_ret, proj_ret, proj_bf, cos_t, sin_t, dec, zeta_l, xi_l, cd, bd, avg, gn)


def _merge_kernel(x_ref, g_ref, ya_ref, yb_ref, yc_ref, wg_ref, wb_ref, wo_ref, o_ref):
    x = x_ref[...]
    h = _rms(x, g_ref[...]).astype(BF16)
    merged = None
    for n, y_ref in enumerate((ya_ref, yb_ref, yc_ref)):
        gate = jax.nn.sigmoid(_dot(h, wg_ref[:, n * D_MODEL:(n + 1) * D_MODEL]))
        term = gate * _dot(y_ref[...], wb_ref[n])
        merged = term if merged is None else merged + term
    o_ref[...] = x + _dot(merged.astype(BF16), wo_ref[...])


def _merge(x2d, g, ya, yb, yc, wg, wb, wo, l, tm):
    t = x2d.shape[0]
    full = lambda a: _layer_spec(a, l)
    row = lambda w: pl.BlockSpec((tm, w), lambda i: (i, 0))
    return pl.pallas_call(
        _merge_kernel,
        grid=(t // tm,),
        in_specs=[row(D_MODEL), full(g), row(BRANCH_WIDTH), row(BRANCH_WIDTH), row(BRANCH_WIDTH),
                  full(wg), full(wb), full(wo)],
        out_specs=row(D_MODEL),
        out_shape=jax.ShapeDtypeStruct((t, D_MODEL), F32),
        compiler_params=_params(("parallel",)),
        name="merge",
    )(x2d, g, ya, yb, yc, wg, wb, wo)


MLP_CHUNK = 1024


def _mlp_kernel(x_ref, g_ref, wu_ref, wd_ref, gf_ref, o_ref, *, final_norm):
    x = x_ref[...]
    h = _rms(x, g_ref[...]).astype(BF16)
    acc = None
    for c0 in range(0, D_FF, MLP_CHUNK):
        u = jnp.maximum(_dot(h, wu_ref[:, c0:c0 + MLP_CHUNK]), 0.0)
        part = _dot((u * u).astype(BF16), wd_ref[c0:c0 + MLP_CHUNK, :])
        acc = part if acc is None else acc + part
    y = x + acc
    o_ref[...] = _rms(y, gf_ref[...]) if final_norm else y


def _mlp(x2d, g, wu, wd, gf, l, final_norm, tm):
    t, d = x2d.shape
    full = lambda a: _layer_spec(a, l)
    row = pl.BlockSpec((tm, d), lambda i: (i, 0))
    return pl.pallas_call(
        functools.partial(_mlp_kernel, final_norm=final_norm),
        grid=(t // tm,),
        in_specs=[row, full(g), full(wu), full(wd), pl.BlockSpec(gf.shape, lambda i: (0, 0))],
        out_specs=row,
        out_shape=jax.ShapeDtypeStruct((t, d), F32),
        compiler_params=_params(("parallel",)),
        name="mlp",
    )(x2d, g, wu, wd, gf)


def _pick(n, pref):
    t = min(pref, n)
    while n % t:
        t //= 2
    return t


def _rope_tables(positions):
    pos = positions.astype(F32)[..., None]
    b, s = positions.shape
    lane = np.arange(LANES)

    def cs(dim, freq_of_lane):
        inv_freq = ROPE_BASE ** (-jnp.arange(0, dim, 2, dtype=F32) / dim)
        ang = pos * inv_freq[freq_of_lane]
        return jnp.cos(ang), jnp.sin(ang)

    hm = MLA_ROPE // 2
    is_rope = (lane >= MLA_NOPE) & (lane < MLA_NOPE + MLA_ROPE)
    cm, sm = cs(MLA_ROPE, np.where(is_rope, (lane - MLA_NOPE) % hm, 0))
    sign_m = np.where(lane < MLA_NOPE + hm, -1.0, 1.0).astype(np.float32)
    mla_cos = jnp.where(lane < MLA_NOPE, 1.0, jnp.where(is_rope, cm, 0.0)).reshape(b * s, LANES)
    mla_sin = jnp.where(is_rope, sign_m * sm, 0.0).reshape(b * s, LANES)
    hr = RET_DK // 2
    ret_cos, sr = cs(RET_DK, lane % hr)
    ret_sin = np.where(lane % RET_DK < hr, -1.0, 1.0).astype(np.float32) * sr
    return mla_cos, mla_sin, ret_cos, ret_sin


def _prep_weights(w_in, mla_w_uq, mla_w_ukv, w_branch, w_out, w_up, w_down):
    depth, d, _ = w_in.shape
    offs = [int(o) for o in np.cumsum((0, MLA_Q_RANK, MLA_KV_RANK, MLA_ROPE) + (BRANCH_WIDTH,) * 7
                                      + (N_BRANCH * D_MODEL,))]
    col_scale = np.ones(offs[-1], np.float32)
    col_scale[offs[3]:offs[4]] = SB_DIM ** -0.5 * LOG2E
    col_scale[offs[7]:offs[8]] = RET_DK ** -0.5
    w = (w_in * col_scale).astype(BF16)
    seg = lambda n: w[:, :, offs[n]:offs[n + 1]]
    c_q, c_kv, k_pe, sb_q, sb_k, sb_v, r_q, r_k, r_v, r_g, gates = (seg(n) for n in range(11))
    zeros = lambda *shape: jnp.zeros(shape, BF16)
    hr = MLA_ROPE // 2
    pad = LANES - MLA_NOPE - MLA_ROPE
    w_proj = jnp.concatenate(
        [c_q, c_kv,
         zeros(depth, d, MLA_NOPE), k_pe, zeros(depth, d, pad),
         zeros(depth, d, MLA_NOPE), k_pe[..., hr:], k_pe[..., :hr], zeros(depth, d, pad),
         r_q, r_k, r_g,
         sb_q, sb_k, sb_v, r_v], axis=-1)

    uq = mla_w_uq.astype(BF16).reshape(depth, MLA_Q_RANK, MLA_HEADS, MLA_NOPE + MLA_ROPE)
    zq = lambda width: jnp.zeros((depth, MLA_Q_RANK, MLA_HEADS, width), BF16)
    nope, rope = uq[..., :MLA_NOPE], uq[..., MLA_NOPE:]
    wqa = jnp.concatenate([nope, rope, zq(pad)], axis=-1)
    wqb = jnp.concatenate([zq(MLA_NOPE), rope[..., hr:], rope[..., :hr], zq(pad)], axis=-1)
    ukv = mla_w_ukv.astype(BF16).reshape(depth, MLA_KV_RANK, MLA_HEADS, MLA_NOPE + MLA_V)
    wk = jnp.concatenate([ukv[..., :MLA_NOPE],
                          jnp.zeros((depth, MLA_KV_RANK, MLA_HEADS, LANES - MLA_NOPE), BF16)], axis=-1)
    wv = ukv[..., MLA_NOPE:]
    flat = lambda a: a.reshape(depth, a.shape[1], -1)
    bf = lambda a: a.astype(BF16)
    return dict(w_in=w_proj, w_gate=gates,
                wqa=flat(wqa), wqb=flat(wqb), wk_t=jnp.swapaxes(flat(wk), 1, 2), wv=flat(wv),
                wb=bf(w_branch), wo=bf(w_out), wu=bf(w_up), wd=bf(w_down))


def kernel(x, positions, norm_mix_g, w_in, mla_q_norm_g, mla_w_uq, mla_kv_norm_g, mla_w_ukv, ret_norm_g, w_branch, w_out, norm_mlp_g, w_up, w_down, final_norm_g):
    b, s, d = x.shape
    depth = w_in.shape[0]
    t = b * s
    assert d == D_MODEL and s % CHUNK == 0
    tm_row = _pick(t, 512)
    t_mla = _pick(s, 1024)
    t_sb = max(_pick(s, 256), SB_SUB)
    w = _prep_weights(w_in, mla_w_uq, mla_w_ukv, w_branch, w_out, w_up, w_down)
    mla_cos, mla_sin, ret_cos, ret_sin = _rope_tables(positions)
    vec = lambda a: a.reshape(1, -1)
    gains = lambda a: a.reshape(depth, 1, -1)
    g_mix, g_q, g_kv, g_mlp = gains(norm_mix_g), gains(mla_q_norm_g), gains(mla_kv_norm_g), gains(norm_mlp_g)

    x2d = x.reshape(t, d)
    for l in range(depth):
        lat, proj_ret, proj_bf = _in_proj(x2d, g_mix, w["w_in"], l, tm_row)
        proj_bf = proj_bf.reshape(b, s, -1)

        q_a, k_t, v_a = _mla_prep(lat, mla_cos, mla_sin, g_q, g_kv,
                                  w["wqa"], w["wqb"], w["wk_t"], w["wv"], l, b, tm_row)
        y_a = _mla_attn(q_a.reshape(b, s, -1), k_t, v_a.reshape(b, s, -1), t_mla, 4)
        y_b = _sb_attn(proj_bf, t_sb, 8)
        y_c = _retention(proj_ret.reshape(b, s, -1), proj_bf, ret_cos, ret_sin, vec(ret_norm_g[l]),
                         _pick(s // CHUNK, 4))

        x2d = _merge(x2d, g_mix, y_a.reshape(t, -1), y_b.reshape(t, -1), y_c.reshape(t, -1),
                     w["w_gate"], w["wb"], w["wo"], l, tm_row)
        x2d = _mlp(x2d, g_mlp, w["wu"], w["wd"], vec(final_norm_g), l, l == depth - 1, tm_row)
    return x2d.reshape(b, s, d)
```

```python
import functools

import numpy as np
import jax
import jax.numpy as jnp
from jax import lax
from jax.experimental import pallas as pl
from jax.experimental.pallas import tpu as pltpu

F32 = jnp.float32
BF16 = jnp.bfloat16

D_MODEL = 1024
EPS = 1e-6
ROPE_BASE = 10000.0
CHUNK = 128

MLA_HEADS = 8
MLA_NOPE = 64
MLA_ROPE = 32
MLA_V = 64
MLA_Q_RANK = 384
MLA_KV_RANK = 256
SB_HEADS = 8
SB_DIM = 64
RET_HEADS = 8
RET_DK = 64
RET_DV = 64
BRANCH_WIDTH = 512
N_BRANCH = 3
D_FF = 4 * D_MODEL

LANES = 128
LAT_WIDTH = MLA_Q_RANK + MLA_KV_RANK + 2 * LANES
VMEM_LIMIT = 56 * 1024 * 1024
LOG2E = 1.4426950408889634
MLA_Q_SCALE = (MLA_NOPE + MLA_ROPE) ** -0.5 * LOG2E
SIGN_BIT = np.uint32(0x80000000)
SB_MASKED_LOGIT = -1e30
SB_SUB = 256
SB_SKIP_LOG2 = -135.0

NT_DIMS = (((1,), (1,)), ((), ()))
TN_DIMS = (((0,), (0,)), ((), ()))


def _dot(a, b):
    return jnp.dot(a, b, preferred_element_type=F32)


def _dot_nt(a, b):
    return lax.dot_general(a, b, NT_DIMS, preferred_element_type=F32)


def _dot_hilo(x, m_bf16):
    hi = x.astype(BF16)
    lo = (x - hi.astype(F32)).astype(BF16)
    return _dot(hi, m_bf16) + _dot(lo, m_bf16)


def _rep_lanes(x, n):
    return x if n == 1 else jnp.concatenate([x] * n, axis=1)


def _rms(x, g):
    return x * lax.rsqrt(jnp.mean(x * x, axis=-1, keepdims=True) + EPS) * g


def _layer_spec(stacked, l):
    nd = stacked.ndim - 1
    return pl.BlockSpec((None,) + stacked.shape[1:], lambda *_: (l,) + (0,) * nd, pipeline_mode=pl.Buffered(1))


def _params(sem):
    return pltpu.CompilerParams(dimension_semantics=sem, vmem_limit_bytes=VMEM_LIMIT)


IN_GROUPS = ((3 * BRANCH_WIDTH, F32),
             (4 * BRANCH_WIDTH, BF16))
IN_CHUNK = 512


def _mla_prep(lat, cos_t, sin_t, gq, gkv, wqa, wqb, wk_t, wv, q_ref, k_ref, v_ref):
    c_q = lat[:, :MLA_Q_RANK]
    c_kv = lat[:, MLA_Q_RANK:MLA_Q_RANK + MLA_KV_RANK]
    kpe_a = lat[:, MLA_Q_RANK + MLA_KV_RANK:MLA_Q_RANK + MLA_KV_RANK + LANES]
    kpe_b = lat[:, MLA_Q_RANK + MLA_KV_RANK + LANES:]
    nq = _rms(c_q, gq).astype(BF16)
    nkv = _rms(c_kv, gkv).astype(BF16)
    q = _dot(nq, wqa) * _rep_lanes(cos_t, MLA_HEADS) + _dot(nq, wqb) * _rep_lanes(sin_t, MLA_HEADS)
    q_ref[...] = (q * MLA_Q_SCALE).astype(BF16)
    k_rot_t = jnp.transpose(kpe_a * cos_t + kpe_b * sin_t)
    k_t = _dot_nt(wk_t, nkv) + jnp.concatenate([k_rot_t] * MLA_HEADS, axis=0)
    k_ref[0] = k_t.astype(BF16)
    v_ref[...] = _dot(nkv, wv).astype(BF16)


def _in_proj_kernel(x_ref, g_ref, w_ref, cos_ref, sin_ref, gq_ref, gkv_ref, wqa_ref, wqb_ref, wk_ref, wv_ref,
                    ret_ref, bf_ref, q_ref, k_ref, v_ref):
    h = _rms(x_ref[...], g_ref[...]).astype(BF16)
    lat = jnp.concatenate([_dot(h, w_ref[:, c0:min(c0 + IN_CHUNK, LAT_WIDTH)])
                           for c0 in range(0, LAT_WIDTH, IN_CHUNK)], axis=1)
    _mla_prep(lat, cos_ref[...], sin_ref[...], gq_ref[...], gkv_ref[...], wqa_ref[...], wqb_ref[...],
              wk_ref[...], wv_ref[...], q_ref, k_ref, v_ref)
    col = LAT_WIDTH
    for o_ref, (width, _) in zip((ret_ref, bf_ref), IN_GROUPS):
        for c0 in range(0, width, IN_CHUNK):
            o_ref[:, c0:c0 + IN_CHUNK] = _dot(h, w_ref[:, col + c0:col + c0 + IN_CHUNK]).astype(o_ref.dtype)
        col += width


def _in_proj(x2d, g, w, cos_t, sin_t, gq, gkv, wqa, wqb, wk_t, wv, l, b, tm):
    t, d = x2d.shape
    hw = MLA_HEADS * LANES
    nsb = t // b // tm
    full = lambda a: _layer_spec(a, l)
    row = lambda width: pl.BlockSpec((tm, width), lambda i: (i, 0))
    return pl.pallas_call(
        _in_proj_kernel,
        grid=(t // tm,),
        in_specs=[row(d), full(g), full(w), row(LANES), row(LANES), full(gq), full(gkv),
                  full(wqa), full(wqb), full(wk_t), full(wv)],
        out_specs=[row(IN_GROUPS[0][0]), row(IN_GROUPS[1][0]), row(hw),
                   pl.BlockSpec((1, hw, tm), lambda i: (i // nsb, 0, i % nsb)), row(MLA_HEADS * MLA_V)],
        out_shape=[jax.ShapeDtypeStruct((t, IN_GROUPS[0][0]), IN_GROUPS[0][1]),
                   jax.ShapeDtypeStruct((t, IN_GROUPS[1][0]), IN_GROUPS[1][1]),
                   jax.ShapeDtypeStruct((t, hw), BF16), jax.ShapeDtypeStruct((b, hw, t // b), BF16),
                   jax.ShapeDtypeStruct((t, MLA_HEADS * MLA_V), BF16)],
        compiler_params=_params(("parallel",)),
        name="in_proj",
    )(x2d, g, w, cos_t, sin_t, gq, gkv, wqa, wqb, wk_t, wv)


def _mla_attn_kernel(q_ref, kt_ref, v_ref, o_ref, m_ref, l_ref, acc_ref, *, tq, tk, nheads):
    i = pl.program_id(2)
    npairs = nheads // 2
    heads = [slice(e * LANES, (e + 1) * LANES) for e in range(nheads)]

    def tiles(j, r0, nr, masked):
        ks = pl.multiple_of(j * tk, tk)
        rows = slice(r0, r0 + nr)
        scores = [_dot(q_ref[0, rows, heads[e]], kt_ref[0, heads[e], pl.ds(ks, tk)]) for e in range(nheads)]
        if masked:
            q_pos = i * tq + r0 + lax.broadcasted_iota(jnp.int32, (nr, tk), 0)
            k_pos = ks + lax.broadcasted_iota(jnp.int32, (nr, tk), 1)
            scores = [jnp.where(k_pos <= q_pos, s, -jnp.inf) for s in scores]
        probs, alphas = [], []
        for e, s in enumerate(scores):
            m_prev = m_ref[e, rows]
            m_next = jnp.maximum(m_prev, jnp.max(s, axis=1, keepdims=True))
            alpha = jnp.exp2(m_prev - m_next)
            p = jnp.exp2(s - _rep_lanes(m_next, tk // LANES))
            partial = functools.reduce(jnp.add, [p[:, c:c + LANES] for c in range(0, tk, LANES)])
            l_ref[e, rows] = alpha * l_ref[e, rows] + partial
            m_ref[e, rows] = m_next
            probs.append(p.astype(BF16))
            alphas.append(alpha)
        for p in range(npairs):
            pv = _dot(jnp.concatenate([probs[2 * p], probs[2 * p + 1]], axis=0), v_ref[0, pl.ds(ks, tk), heads[p]])
            for e in range(2):
                arows = slice(e * tq + r0, e * tq + r0 + nr)
                acc_ref[p, arows] = alphas[2 * p + e] * acc_ref[p, arows] + pv[e * nr:(e + 1) * nr]

    m_ref[...] = jnp.full(m_ref.shape, -jnp.inf, F32)
    l_ref[...] = jnp.zeros(l_ref.shape, F32)
    acc_ref[...] = jnp.zeros(acc_ref.shape, F32)

    def body(j, carry):
        tiles(j, 0, tq, False)
        return carry

    lax.fori_loop(0, 2 * i, body, 0)
    tiles(2 * i, 0, tq, True)
    tiles(2 * i + 1, tk, tq - tk, True)
    lane = lax.broadcasted_iota(jnp.int32, (tq, LANES), 1)
    denom = [jnp.sum(l_ref[e], axis=1, keepdims=True) for e in range(nheads)]
    for p in range(npairs):
        o_ref[0, :, heads[p]] = jnp.where(
            lane < MLA_V, acc_ref[p, :tq] / denom[2 * p], acc_ref[p, tq:] / denom[2 * p + 1]).astype(o_ref.dtype)


def _mla_attn(q, k_t, v, tq, nheads):
    b, s, _ = q.shape
    tk = tq // 2
    return pl.pallas_call(
        functools.partial(_mla_attn_kernel, tq=tq, tk=tk, nheads=nheads),
        grid=(b, MLA_HEADS // nheads, s // tq),
        in_specs=[pl.BlockSpec((1, tq, nheads * LANES), lambda bi, p, i: (bi, i, p)),
                  pl.BlockSpec((1, nheads * LANES, s), lambda bi, p, i: (bi, p, 0)),
                  pl.BlockSpec((1, s, nheads * MLA_V), lambda bi, p, i: (bi, 0, p))],
        out_specs=pl.BlockSpec((1, tq, nheads * MLA_V), lambda bi, p, i: (bi, i, p)),
        out_shape=jax.ShapeDtypeStruct((b, s, MLA_HEADS * MLA_V), BF16),
        scratch_shapes=[pltpu.VMEM((nheads, tq, LANES), F32), pltpu.VMEM((nheads, tq, LANES), F32),
                        pltpu.VMEM((nheads // 2, 2 * tq, LANES), F32)],
        compiler_params=_params(("parallel", "parallel", "arbitrary")),
        name="mla_attn",
    )(q, k_t, v)


def _sb_attn_kernel(q_ref, k_ref, v_ref, u_ref, o_ref, qm_ref, c_ref, acc_ref, *, tq, nheads):
    i = pl.program_id(2)
    npairs = nheads // 2
    lane = lax.broadcasted_iota(jnp.int32, (tq, LANES), 1)
    zero = jnp.zeros((tq, LANES), BF16)
    for p in range(npairs):
        q_pair = q_ref[0, :, p * LANES:(p + 1) * LANES]
        qm_ref[p, :tq] = jnp.where(lane < SB_DIM, q_pair, zero)
        qm_ref[p, tq:] = jnp.where(lane < SB_DIM, zero, q_pair)

    hq = tq // 2
    pairs = [slice(p * LANES, (p + 1) * LANES) for p in range(npairs)]
    row_sets = {"all": ((0, 2 * tq),), "upper": ((0, hq), (tq, hq)), "lower": ((hq, hq), (tq + hq, hq))}

    def load(ref, p, sel):
        parts = [ref[p, s:s + n] for s, n in row_sets[sel]]
        return parts[0] if len(parts) == 1 else jnp.concatenate(parts, axis=0)

    def store(ref, p, sel, val):
        off = 0
        for s, n in row_sets[sel]:
            ref[p, s:s + n] = val[off:off + n]
            off += n

    has_left = i > 0
    row = lax.broadcasted_iota(jnp.int32, (2 * tq, tq), 0)
    col = lax.broadcasted_iota(jnp.int32, (2 * tq, tq), 1)
    before = col < jnp.where(row < tq, row, row - tq)

    def run(jobs):
        starts = [pl.multiple_of(j * tq, tq) for _, j, _, _ in jobs]
        zs = [_dot_nt(load(qm_ref, p, sel), k_ref[0, pl.ds(ks, tq), pairs[p]])
              for (p, _, sel, _), ks in zip(jobs, starts)]
        stats = []
        for z, (_, _, _, kind) in zip(zs, jobs):
            if kind == "diag":
                z = jnp.where(before, z, SB_MASKED_LOGIT)
            neg_abs = lax.bitcast_convert_type(lax.bitcast_convert_type(z, jnp.uint32) | SIGN_BIT, F32)
            log_beta = jnp.minimum(z, 0.0) - jnp.log(1.0 + jnp.exp2(neg_abs)) * LOG2E
            log_not = log_beta - z
            hi = log_not.astype(BF16)
            lo = (log_not - hi.astype(F32)).astype(BF16)
            stats.append((log_beta, log_not, jnp.concatenate([hi, lo], axis=1)))
        withins = [_dot(hilo, u_ref[...]) for _, _, hilo in stats]
        weights = []
        for (p, _, sel, kind), (log_beta, log_not, _), within in zip(jobs, stats, withins):
            c = load(c_ref, p, sel)
            if kind == "left":
                c = jnp.where(has_left, c, -jnp.inf)
            a = jnp.exp2(log_beta + within + _rep_lanes(c, tq // LANES))
            weights.append(a.astype(BF16))
            store(c_ref, p, sel, c + jnp.sum(log_not, axis=1, keepdims=True))
        for (p, _, sel, _), ks, a in zip(jobs, starts, weights):
            store(acc_ref, p, sel, load(acc_ref, p, sel) + _dot(a, v_ref[0, pl.ds(ks, tq), pairs[p]]))

    def live(sel):
        worst = functools.reduce(jnp.maximum, [jnp.max(load(c_ref, p, sel)) for p in range(npairs)])
        return worst > SB_SKIP_LOG2

    c_ref[...] = jnp.zeros(c_ref.shape, F32)
    acc_ref[...] = jnp.zeros(acc_ref.shape, F32)
    left = jnp.maximum(i - 1, 0)
    run([(p, i, "all", "diag") for p in range(npairs)] + [(p, left, "all", "left") for p in range(npairs)])

    def cond(carry):
        jj, go = carry
        return jnp.logical_and(jj < i, go > 0)

    def body(carry):
        jj, _ = carry
        run([(p, i - 1 - jj, "all", "plain") for p in range(npairs)])
        return jj + 1, live("all").astype(jnp.int32)

    lax.while_loop(cond, body, (jnp.int32(1), live("all").astype(jnp.int32)))
    for p in range(npairs):
        o_ref[0, :, p * LANES:(p + 1) * LANES] = jnp.where(
            lane < SB_DIM, acc_ref[p, :tq], acc_ref[p, tq:]).astype(o_ref.dtype)


def _sb_attn(proj_bf, tq, nheads):
    b, s, _ = proj_bf.shape
    assert tq == SB_SUB
    ngroup = SB_HEADS // nheads
    width = nheads * SB_DIM
    idx = np.arange(SB_SUB)
    tri = (idx[:, None] > idx[None, :]).astype(np.float32)
    u = jnp.asarray(np.concatenate([tri, tri], axis=0), dtype=BF16)
    return pl.pallas_call(
        functools.partial(_sb_attn_kernel, tq=tq, nheads=nheads),
        grid=(b, ngroup, s // tq),
        in_specs=[pl.BlockSpec((1, tq, width), lambda bi, p, i: (bi, i, p)),
                  pl.BlockSpec((1, s, width), lambda bi, p, i: (bi, 0, ngroup + p)),
                  pl.BlockSpec((1, s, width), lambda bi, p, i: (bi, 0, 2 * ngroup + p)),
                  pl.BlockSpec(u.shape, lambda bi, p, i: (0, 0))],
        out_specs=pl.BlockSpec((1, tq, width), lambda bi, p, i: (bi, i, p)),
        out_shape=jax.ShapeDtypeStruct((b, s, SB_HEADS * SB_DIM), BF16),
        scratch_shapes=[pltpu.VMEM((nheads // 2, 2 * tq, LANES), BF16),
                        pltpu.VMEM((nheads // 2, 2 * tq, LANES), F32),
                        pltpu.VMEM((nheads // 2, 2 * tq, LANES), F32)],
        compiler_params=_params(("parallel", "parallel", "arbitrary")),
        name="sb_attn",
    )(proj_bf, proj_bf, proj_bf, u)


def _retention_kernel(q_ref, k_ref, g_ref, v_ref, cos_ref, sin_ref, dec_ref, zeta_ref, xi_ref,
                      cd_ref, bd_ref, avg_ref, gn_ref, o_ref, state_ref, *, nchunk):
    @pl.when(pl.program_id(1) == 0)
    def _():
        state_ref[...] = jnp.zeros(state_ref.shape, F32)

    rows = nchunk * CHUNK
    width = RET_HEADS * RET_DK
    npair = RET_HEADS // 2
    half = RET_DK // 2
    lane_w = lax.broadcasted_iota(jnp.int32, (rows, width), 1)
    first_half = (lane_w % RET_DK) < half
    head0 = (lane_w % LANES) < RET_DK
    cos_t = _rep_lanes(cos_ref[0], width // LANES)
    sin_t = _rep_lanes(sin_ref[0], width // LANES)

    def rope(x):
        swapped = jnp.where(first_half, pltpu.roll(x, width - half, 1), pltpu.roll(x, half, 1))
        return x * cos_t + swapped * sin_t

    q = rope(q_ref[0])
    k = rope(k_ref[0])
    kz = (k * zeta_ref[...]).astype(BF16)
    qb = q.astype(BF16)
    kb = k.astype(BF16)
    zero = jnp.zeros_like(qb)
    q0 = jnp.where(head0, qb, zero)
    q1 = jnp.where(head0, zero, qb)
    v = v_ref[0]
    units = [(c, p) for c in range(nchunk) for p in range(npair)]
    rs = lambda c: slice(c * CHUNK, (c + 1) * CHUNK)
    ls = lambda p: slice(p * LANES, (p + 1) * LANES)
    scores = [_dot_nt(jnp.concatenate([q0[rs(c), ls(p)], q1[rs(c), ls(p)]], axis=0), kb[rs(c), ls(p)])
              for c, p in units]
    scores = [(s * dec_ref[p]).astype(BF16) for s, (c, p) in zip(scores, units)]
    intra = [_dot(s, v[rs(c), ls(p)]) for s, (c, p) in zip(scores, units)]
    kvs = [lax.dot_general(kz[rs(c), ls(p)], v[rs(c), ls(p)], TN_DIMS, preferred_element_type=F32)
           for c, p in units]
    lane = lax.broadcasted_iota(jnp.int32, (CHUNK, LANES), 1)
    chunks = []
    for c in range(nchunk):
        ys = []
        for p in range(npair):
            u = c * npair + p
            state = state_ref[p]
            y_cross = _dot(qb[rs(c), ls(p)], state.astype(BF16)) * xi_ref[:, ls(p)]
            state_ref[p] = state * cd_ref[p] + kvs[u] * bd_ref[...]
            y_intra = jnp.where(lane < RET_DV, intra[u][:CHUNK], intra[u][CHUNK:])
            ys.append(y_intra + y_cross)
        chunks.append(jnp.concatenate(ys, axis=1))
    y = jnp.concatenate(chunks, axis=0)
    avg = avg_ref[...]
    d = y - _dot_hilo(y, avg)
    var = _dot_hilo(d * d, avg)
    yn = d * lax.rsqrt(var + EPS) * gn_ref[...]
    g = g_ref[0]
    o_ref[0] = (g * jax.nn.sigmoid(g) * yn).astype(o_ref.dtype)


def _retention_consts(nchunk):
    h = np.arange(RET_HEADS, dtype=np.float64)
    log_gamma = np.log1p(-np.exp2(-5.0 - h))
    idx = np.arange(CHUNK, dtype=np.float64)
    diff = idx[:, None] - idx[None, :]
    dec = np.where(diff[None] >= 0, np.exp(np.maximum(diff, 0.0)[None] * log_gamma[:, None, None]), 0.0)
    npair = RET_HEADS // 2
    dec = dec.reshape(npair, 2 * CHUNK, CHUNK)
    zeta = np.exp((CHUNK - 1 - idx)[:, None] * log_gamma[None, :])
    xi = np.exp((idx + 1.0)[:, None] * log_gamma[None, :])
    zeta_l = np.tile(np.repeat(zeta, RET_DK, axis=1), (nchunk, 1))
    xi_l = np.repeat(xi, RET_DV, axis=1)
    chunk_decay = np.exp(CHUNK * log_gamma)
    head_of = np.arange(LANES) // RET_DK
    bd = (head_of[:, None] == head_of[None, :]).astype(np.float64)
    cd = np.stack([bd * chunk_decay[2 * p + head_of][:, None] for p in range(npair)])
    gh = np.arange(RET_HEADS * RET_DV) // RET_DV
    avg = (gh[:, None] == gh[None, :]).astype(np.float64) / RET_DV
    f = lambda a: jnp.asarray(a, dtype=F32)
    return f(dec), f(zeta_l), f(xi_l), f(cd), f(bd), jnp.asarray(avg, dtype=BF16)


def _retention(proj_ret, proj_bf, cos_t, sin_t, gn, nchunk):
    b, s, _ = proj_ret.shape
    width = RET_HEADS * RET_DK
    rows = nchunk * CHUNK
    dec, zeta_l, xi_l, cd, bd, avg = _retention_consts(nchunk)
    full = lambda a: pl.BlockSpec(a.shape, lambda bi, n: (0,) * a.ndim)
    blk = lambda c: pl.BlockSpec((1, rows, width), lambda bi, n: (bi, n, c))
    tab = pl.BlockSpec((1, rows, LANES), lambda bi, n: (bi, n, 0))
    return pl.pallas_call(
        functools.partial(_retention_kernel, nchunk=nchunk),
        grid=(b, s // rows),
        in_specs=[blk(0), blk(1), blk(2), blk(3), tab, tab,
                  full(dec), full(zeta_l), full(xi_l), full(cd), full(bd), full(avg), full(gn)],
        out_specs=blk(0),
        out_shape=jax.ShapeDtypeStruct((b, s, RET_HEADS * RET_DV), BF16),
        scratch_shapes=[pltpu.VMEM((RET_HEADS // 2, LANES, LANES), F32)],
        compiler_params=_params(("parallel", "arbitrary")),
        name="retention",
    )(proj_ret, proj_ret, proj_ret, proj_bf, cos_t, sin_t, dec, zeta_l, xi_l, cd, bd, avg, gn)


def _merge_kernel(x_ref, g_ref, ya_ref, yb_ref, yc_ref, wg_ref, wb_ref, wo_ref, o_ref):
    x = x_ref[...]
    h = _rms(x, g_ref[...]).astype(BF16)
    merged = None
    for n, y_ref in enumerate((ya_ref, yb_ref, yc_ref)):
        gate = jax.nn.sigmoid(_dot(h, wg_ref[:, n * D_MODEL:(n + 1) * D_MODEL]))
        term = gate * _dot(y_ref[...], wb_ref[n])
        merged = term if merged is None else merged + term
    o_ref[...] = x + _dot(merged.astype(BF16), wo_ref[...])


def _merge(x2d, g, ya, yb, yc, wg, wb, wo, l, tm):
    t = x2d.shape[0]
    full = lambda a: _layer_spec(a, l)
    row = lambda w: pl.BlockSpec((tm, w), lambda i: (i, 0))
    return pl.pallas_call(
        _merge_kernel,
        grid=(t // tm,),
        in_specs=[row(D_MODEL), full(g), row(BRANCH_WIDTH), row(BRANCH_WIDTH), row(BRANCH_WIDTH),
                  full(wg), full(wb), full(wo)],
        out_specs=row(D_MODEL),
        out_shape=jax.ShapeDtypeStruct((t, D_MODEL), F32),
        compiler_params=_params(("parallel",)),
        name="merge",
    )(x2d, g, ya, yb, yc, wg, wb, wo)


MLP_CHUNK = 1024


def _mlp_kernel(x_ref, g_ref, wu_ref, wd_ref, gf_ref, o_ref, *, final_norm):
    x = x_ref[...]
    h = _rms(x, g_ref[...]).astype(BF16)
    acc = None
    for c0 in range(0, D_FF, MLP_CHUNK):
        u = jnp.maximum(_dot(h, wu_ref[:, c0:c0 + MLP_CHUNK]), 0.0)
        part = _dot((u * u).astype(BF16), wd_ref[c0:c0 + MLP_CHUNK, :])
        acc = part if acc is None else acc + part
    y = x + acc
    o_ref[...] = _rms(y, gf_ref[...]) if final_norm else y


def _mlp(x2d, g, wu, wd, gf, l, final_norm, tm):
    t, d = x2d.shape
    full = lambda a: _layer_spec(a, l)
    row = pl.BlockSpec((tm, d), lambda i: (i, 0))
    return pl.pallas_call(
        functools.partial(_mlp_kernel, final_norm=final_norm),
        grid=(t // tm,),
        in_specs=[row, full(g), full(wu), full(wd), pl.BlockSpec(gf.shape, lambda i: (0, 0))],
        out_specs=row,
        out_shape=jax.ShapeDtypeStruct((t, d), F32),
        compiler_params=_params(("parallel",)),
        name="mlp",
    )(x2d, g, wu, wd, gf)


def _pick(n, pref):
    t = min(pref, n)
    while n % t:
        t //= 2
    return t


def _rope_tables(positions):
    pos = positions.astype(F32)[..., None]
    b, s = positions.shape
    lane = np.arange(LANES)

    def cs(dim, freq_of_lane):
        inv_freq = ROPE_BASE ** (-jnp.arange(0, dim, 2, dtype=F32) / dim)
        ang = pos * inv_freq[freq_of_lane]
        return jnp.cos(ang), jnp.sin(ang)

    hm = MLA_ROPE // 2
    is_rope = (lane >= MLA_NOPE) & (lane < MLA_NOPE + MLA_ROPE)
    cm, sm = cs(MLA_ROPE, np.where(is_rope, (lane - MLA_NOPE) % hm, 0))
    sign_m = np.where(lane < MLA_NOPE + hm, -1.0, 1.0).astype(np.float32)
    mla_cos = jnp.where(lane < MLA_NOPE, 1.0, jnp.where(is_rope, cm, 0.0)).reshape(b * s, LANES)
    mla_sin = jnp.where(is_rope, sign_m * sm, 0.0).reshape(b * s, LANES)
    hr = RET_DK // 2
    ret_cos, sr = cs(RET_DK, lane % hr)
    ret_sin = np.where(lane % RET_DK < hr, -1.0, 1.0).astype(np.float32) * sr
    return mla_cos, mla_sin, ret_cos, ret_sin


def _prep_weights(w_in, mla_w_uq, mla_w_ukv, w_branch, w_out, w_up, w_down):
    depth, d, _ = w_in.shape
    offs = [int(o) for o in np.cumsum((0, MLA_Q_RANK, MLA_KV_RANK, MLA_ROPE) + (BRANCH_WIDTH,) * 7
                                      + (N_BRANCH * D_MODEL,))]
    col_scale = np.ones(offs[-1], np.float32)
    col_scale[offs[3]:offs[4]] = SB_DIM ** -0.5 * LOG2E
    col_scale[offs[7]:offs[8]] = RET_DK ** -0.5
    w = (w_in * col_scale).astype(BF16)
    seg = lambda n: w[:, :, offs[n]:offs[n + 1]]
    c_q, c_kv, k_pe, sb_q, sb_k, sb_v, r_q, r_k, r_v, r_g, gates = (seg(n) for n in range(11))
    zeros = lambda *shape: jnp.zeros(shape, BF16)
    hr = MLA_ROPE // 2
    pad = LANES - MLA_NOPE - MLA_ROPE
    w_proj = jnp.concatenate(
        [c_q, c_kv,
         zeros(depth, d, MLA_NOPE), k_pe, zeros(depth, d, pad),
         zeros(depth, d, MLA_NOPE), k_pe[..., hr:], k_pe[..., :hr], zeros(depth, d, pad),
         r_q, r_k, r_g,
         sb_q, sb_k, sb_v, r_v], axis=-1)

    uq = mla_w_uq.astype(BF16).reshape(depth, MLA_Q_RANK, MLA_HEADS, MLA_NOPE + MLA_ROPE)
    zq = lambda width: jnp.zeros((depth, MLA_Q_RANK, MLA_HEADS, width), BF16)
    nope, rope = uq[..., :MLA_NOPE], uq[..., MLA_NOPE:]
    wqa = jnp.concatenate([nope, rope, zq(pad)], axis=-1)
    wqb = jnp.concatenate([zq(MLA_NOPE), rope[..., hr:], rope[..., :hr], zq(pad)], axis=-1)
    ukv = mla_w_ukv.astype(BF16).reshape(depth, MLA_KV_RANK, MLA_HEADS, MLA_NOPE + MLA_V)
    wk = jnp.concatenate([ukv[..., :MLA_NOPE],
                          jnp.zeros((depth, MLA_KV_RANK, MLA_HEADS, LANES - MLA_NOPE), BF16)], axis=-1)
    wv = ukv[..., MLA_NOPE:]
    flat = lambda a: a.reshape(depth, a.shape[1], -1)
    bf = lambda a: a.astype(BF16)
    return dict(w_in=w_proj, w_gate=gates,
                wqa=flat(wqa), wqb=flat(wqb), wk_t=jnp.swapaxes(flat(wk), 1, 2), wv=flat(wv),
                wb=bf(w_branch), wo=bf(w_out), wu=bf(w_up), wd=bf(w_down))


def kernel(x, positions, norm_mix_g, w_in, mla_q_norm_g, mla_w_uq, mla_kv_norm_g, mla_w_ukv, ret_norm_g, w_branch, w_out, norm_mlp_g, w_up, w_down, final_norm_g):
    b, s, d = x.shape
    depth = w_in.shape[0]
    t = b * s
    assert d == D_MODEL and s % CHUNK == 0
    tm_row = _pick(t, 512)
    t_mla = _pick(s, 1024)
    t_sb = max(_pick(s, 256), SB_SUB)
    w = _prep_weights(w_in, mla_w_uq, mla_w_ukv, w_branch, w_out, w_up, w_down)
    mla_cos, mla_sin, ret_cos, ret_sin = _rope_tables(positions)
    vec = lambda a: a.reshape(1, -1)
    gains = lambda a: a.reshape(depth, 1, -1)
    g_mix, g_q, g_kv, g_mlp = gains(norm_mix_g), gains(mla_q_norm_g), gains(mla_kv_norm_g), gains(norm_mlp_g)

    x2d = x.reshape(t, d)
    for l in range(depth):
        proj_ret, proj_bf, q_a, k_t, v_a = _in_proj(x2d, g_mix, w["w_in"], mla_cos, mla_sin, g_q, g_kv,
                                                    w["wqa"], w["wqb"], w["wk_t"], w["wv"], l, b, tm_row)
        proj_bf = proj_bf.reshape(b, s, -1)

        y_a = _mla_attn(q_a.reshape(b, s, -1), k_t, v_a.reshape(b, s, -1), t_mla, 4)
        y_b = _sb_attn(proj_bf, t_sb, 8)
        y_c = _retention(proj_ret.reshape(b, s, -1), proj_bf, ret_cos, ret_sin, vec(ret_norm_g[l]),
                         _pick(s // CHUNK, 4))

        x2d = _merge(x2d, g_mix, y_a.reshape(t, -1), y_b.reshape(t, -1), y_c.reshape(t, -1),
                     w["w_gate"], w["wb"], w["wo"], l, tm_row)
        x2d = _mlp(x2d, g_mlp, w["wu"], w["wd"], vec(final_norm_g), l, l == depth - 1, tm_row)
    return x2d.reshape(b, s, d)
```

```python
import functools

import numpy as np
import jax
import jax.numpy as jnp
from jax import lax
from jax.experimental import pallas as pl
from jax.experimental.pallas import tpu as pltpu

F32 = jnp.float32
BF16 = jnp.bfloat16

D_MODEL = 1024
EPS = 1e-6
ROPE_BASE = 10000.0
CHUNK = 128

MLA_HEADS = 8
MLA_NOPE = 64
MLA_ROPE = 32
MLA_V = 64
MLA_Q_RANK = 384
MLA_KV_RANK = 256
SB_HEADS = 8
SB_DIM = 64
RET_HEADS = 8
RET_DK = 64
RET_DV = 64
BRANCH_WIDTH = 512
N_BRANCH = 3
D_FF = 4 * D_MODEL

LANES = 128
LAT_WIDTH = MLA_Q_RANK + MLA_KV_RANK + 2 * LANES
VMEM_LIMIT = 56 * 1024 * 1024
LOG2E = 1.4426950408889634
MLA_Q_SCALE = (MLA_NOPE + MLA_ROPE) ** -0.5 * LOG2E
SIGN_BIT = np.uint32(0x80000000)
SB_MASKED_LOGIT = -1e30
SB_SUB = 256
SB_SKIP_LOG2 = -135.0

NT_DIMS = (((1,), (1,)), ((), ()))
TN_DIMS = (((0,), (0,)), ((), ()))


def _dot(a, b):
    return jnp.dot(a, b, preferred_element_type=F32)


def _dot_nt(a, b):
    return lax.dot_general(a, b, NT_DIMS, preferred_element_type=F32)


def _dot_hilo(x, m_bf16):
    hi = x.astype(BF16)
    lo = (x - hi.astype(F32)).astype(BF16)
    return _dot(hi, m_bf16) + _dot(lo, m_bf16)


def _rep_lanes(x, n):
    return x if n == 1 else jnp.concatenate([x] * n, axis=1)


def _rms(x, g):
    return x * lax.rsqrt(jnp.mean(x * x, axis=-1, keepdims=True) + EPS) * g


def _layer_spec(stacked, l):
    nd = stacked.ndim - 1
    return pl.BlockSpec((None,) + stacked.shape[1:], lambda *_: (l,) + (0,) * nd, pipeline_mode=pl.Buffered(1))


def _params(sem):
    return pltpu.CompilerParams(dimension_semantics=sem, vmem_limit_bytes=VMEM_LIMIT)


IN_GROUPS = ((3 * BRANCH_WIDTH, F32),
             (4 * BRANCH_WIDTH, BF16))
IN_CHUNK = 512


def _mla_prep(lat, cos_t, sin_t, gq, gkv, wqa, wqb, wk_t, wv, q_ref, k_ref, v_ref):
    c_q = lat[:, :MLA_Q_RANK]
    c_kv = lat[:, MLA_Q_RANK:MLA_Q_RANK + MLA_KV_RANK]
    kpe_a = lat[:, MLA_Q_RANK + MLA_KV_RANK:MLA_Q_RANK + MLA_KV_RANK + LANES]
    kpe_b = lat[:, MLA_Q_RANK + MLA_KV_RANK + LANES:]
    nq = _rms(c_q, gq).astype(BF16)
    nkv = _rms(c_kv, gkv).astype(BF16)
    q = _dot(nq, wqa) * _rep_lanes(cos_t, MLA_HEADS) + _dot(nq, wqb) * _rep_lanes(sin_t, MLA_HEADS)
    q_ref[...] = (q * MLA_Q_SCALE).astype(BF16)
    k_rot_t = jnp.transpose(kpe_a * cos_t + kpe_b * sin_t)
    k_t = _dot_nt(wk_t, nkv) + jnp.concatenate([k_rot_t] * MLA_HEADS, axis=0)
    k_ref[0] = k_t.astype(BF16)
    v_ref[...] = _dot(nkv, wv).astype(BF16)


def _in_proj_kernel(x_ref, g_ref, w_ref, cos_ref, sin_ref, gq_ref, gkv_ref, wqa_ref, wqb_ref, wk_ref, wv_ref,
                    ret_ref, bf_ref, q_ref, k_ref, v_ref):
    h = _rms(x_ref[...], g_ref[...]).astype(BF16)
    lat = jnp.concatenate([_dot(h, w_ref[:, c0:min(c0 + IN_CHUNK, LAT_WIDTH)])
                           for c0 in range(0, LAT_WIDTH, IN_CHUNK)], axis=1)
    _mla_prep(lat, cos_ref[...], sin_ref[...], gq_ref[...], gkv_ref[...], wqa_ref[...], wqb_ref[...],
              wk_ref[...], wv_ref[...], q_ref, k_ref, v_ref)
    col = LAT_WIDTH
    for o_ref, (width, _) in zip((ret_ref, bf_ref), IN_GROUPS):
        for c0 in range(0, width, IN_CHUNK):
            o_ref[:, c0:c0 + IN_CHUNK] = _dot(h, w_ref[:, col + c0:col + c0 + IN_CHUNK]).astype(o_ref.dtype)
        col += width


def _in_proj(x2d, g, w, cos_t, sin_t, gq, gkv, wqa, wqb, wk_t, wv, l, b, tm):
    t, d = x2d.shape
    hw = MLA_HEADS * LANES
    nsb = t // b // tm
    full = lambda a: _layer_spec(a, l)
    row = lambda width: pl.BlockSpec((tm, width), lambda i: (i, 0))
    return pl.pallas_call(
        _in_proj_kernel,
        grid=(t // tm,),
        in_specs=[row(d), full(g), full(w), row(LANES), row(LANES), full(gq), full(gkv),
                  full(wqa), full(wqb), full(wk_t), full(wv)],
        out_specs=[row(IN_GROUPS[0][0]), row(IN_GROUPS[1][0]), row(hw),
                   pl.BlockSpec((1, hw, tm), lambda i: (i // nsb, 0, i % nsb)), row(MLA_HEADS * MLA_V)],
        out_shape=[jax.ShapeDtypeStruct((t, IN_GROUPS[0][0]), IN_GROUPS[0][1]),
                   jax.ShapeDtypeStruct((t, IN_GROUPS[1][0]), IN_GROUPS[1][1]),
                   jax.ShapeDtypeStruct((t, hw), BF16), jax.ShapeDtypeStruct((b, hw, t // b), BF16),
                   jax.ShapeDtypeStruct((t, MLA_HEADS * MLA_V), BF16)],
        compiler_params=_params(("parallel",)),
        name="in_proj",
    )(x2d, g, w, cos_t, sin_t, gq, gkv, wqa, wqb, wk_t, wv)


def _mla_attn_kernel(q_ref, kt_ref, v_ref, o_ref, m_ref, l_ref, acc_ref, *, tq, tk, nheads):
    i = pl.program_id(2)
    npairs = nheads // 2
    heads = [slice(e * LANES, (e + 1) * LANES) for e in range(nheads)]

    def tiles(j, r0, nr, masked):
        ks = pl.multiple_of(j * tk, tk)
        rows = slice(r0, r0 + nr)
        scores = [_dot(q_ref[0, rows, heads[e]], kt_ref[0, heads[e], pl.ds(ks, tk)]) for e in range(nheads)]
        if masked:
            q_pos = i * tq + r0 + lax.broadcasted_iota(jnp.int32, (nr, tk), 0)
            k_pos = ks + lax.broadcasted_iota(jnp.int32, (nr, tk), 1)
            scores = [jnp.where(k_pos <= q_pos, s, -jnp.inf) for s in scores]
        probs, alphas = [], []
        for e, s in enumerate(scores):
            m_prev = m_ref[e, rows]
            m_next = jnp.maximum(m_prev, jnp.max(s, axis=1, keepdims=True))
            alpha = jnp.exp2(m_prev - m_next)
            p = jnp.exp2(s - _rep_lanes(m_next, tk // LANES))
            partial = functools.reduce(jnp.add, [p[:, c:c + LANES] for c in range(0, tk, LANES)])
            l_ref[e, rows] = alpha * l_ref[e, rows] + partial
            m_ref[e, rows] = m_next
            probs.append(p.astype(BF16))
            alphas.append(alpha)
        for p in range(npairs):
            pv = _dot(jnp.concatenate([probs[2 * p], probs[2 * p + 1]], axis=0), v_ref[0, pl.ds(ks, tk), heads[p]])
            for e in range(2):
                arows = slice(e * tq + r0, e * tq + r0 + nr)
                acc_ref[p, arows] = alphas[2 * p + e] * acc_ref[p, arows] + pv[e * nr:(e + 1) * nr]

    m_ref[...] = jnp.full(m_ref.shape, -jnp.inf, F32)
    l_ref[...] = jnp.zeros(l_ref.shape, F32)
    acc_ref[...] = jnp.zeros(acc_ref.shape, F32)

    def body(j, carry):
        tiles(j, 0, tq, False)
        return carry

    lax.fori_loop(0, 2 * i, body, 0)
    tiles(2 * i, 0, tq, True)
    tiles(2 * i + 1, tk, tq - tk, True)
    lane = lax.broadcasted_iota(jnp.int32, (tq, LANES), 1)
    denom = [jnp.sum(l_ref[e], axis=1, keepdims=True) for e in range(nheads)]
    for p in range(npairs):
        o_ref[0, :, heads[p]] = jnp.where(
            lane < MLA_V, acc_ref[p, :tq] / denom[2 * p], acc_ref[p, tq:] / denom[2 * p + 1]).astype(o_ref.dtype)


def _mla_attn(q, k_t, v, tq, nheads):
    b, s, _ = q.shape
    tk = tq // 2
    return pl.pallas_call(
        functools.partial(_mla_attn_kernel, tq=tq, tk=tk, nheads=nheads),
        grid=(b, MLA_HEADS // nheads, s // tq),
        in_specs=[pl.BlockSpec((1, tq, nheads * LANES), lambda bi, p, i: (bi, i, p)),
                  pl.BlockSpec((1, nheads * LANES, s), lambda bi, p, i: (bi, p, 0)),
                  pl.BlockSpec((1, s, nheads * MLA_V), lambda bi, p, i: (bi, 0, p))],
        out_specs=pl.BlockSpec((1, tq, nheads * MLA_V), lambda bi, p, i: (bi, i, p)),
        out_shape=jax.ShapeDtypeStruct((b, s, MLA_HEADS * MLA_V), BF16),
        scratch_shapes=[pltpu.VMEM((nheads, tq, LANES), F32), pltpu.VMEM((nheads, tq, LANES), F32),
                        pltpu.VMEM((nheads // 2, 2 * tq, LANES), F32)],
        compiler_params=_params(("parallel", "parallel", "arbitrary")),
        name="mla_attn",
    )(q, k_t, v)


def _sb_attn_kernel(q_ref, k_ref, v_ref, u_ref, o_ref, qm_ref, c_ref, acc_ref, *, tq, nheads):
    i = pl.program_id(2)
    npairs = nheads // 2
    lane = lax.broadcasted_iota(jnp.int32, (tq, LANES), 1)
    zero = jnp.zeros((tq, LANES), BF16)
    for p in range(npairs):
        q_pair = q_ref[0, :, p * LANES:(p + 1) * LANES]
        qm_ref[p, :tq] = jnp.where(lane < SB_DIM, q_pair, zero)
        qm_ref[p, tq:] = jnp.where(lane < SB_DIM, zero, q_pair)

    hq = tq // 2
    pairs = [slice(p * LANES, (p + 1) * LANES) for p in range(npairs)]
    row_sets = {"all": ((0, 2 * tq),), "upper": ((0, hq), (tq, hq)), "lower": ((hq, hq), (tq + hq, hq))}

    def load(ref, p, sel):
        parts = [ref[p, s:s + n] for s, n in row_sets[sel]]
        return parts[0] if len(parts) == 1 else jnp.concatenate(parts, axis=0)

    def store(ref, p, sel, val):
        off = 0
        for s, n in row_sets[sel]:
            ref[p, s:s + n] = val[off:off + n]
            off += n

    has_left = i > 0
    row = lax.broadcasted_iota(jnp.int32, (2 * tq, tq), 0)
    col = lax.broadcasted_iota(jnp.int32, (2 * tq, tq), 1)
    before = col < jnp.where(row < tq, row, row - tq)

    def run(jobs):
        starts = [pl.multiple_of(j * tq, tq) for _, j, _, _ in jobs]
        zs = [_dot_nt(load(qm_ref, p, sel), k_ref[0, pl.ds(ks, tq), pairs[p]])
              for (p, _, sel, _), ks in zip(jobs, starts)]
        stats = []
        for z, (_, _, _, kind) in zip(zs, jobs):
            if kind == "diag":
                z = jnp.where(before, z, SB_MASKED_LOGIT)
            neg_abs = lax.bitcast_convert_type(lax.bitcast_convert_type(z, jnp.uint32) | SIGN_BIT, F32)
            log_beta = jnp.minimum(z, 0.0) - jnp.log(1.0 + jnp.exp2(neg_abs)) * LOG2E
            log_not = log_beta - z
            hi = log_not.astype(BF16)
            lo = (log_not - hi.astype(F32)).astype(BF16)
            stats.append((log_beta, log_not, jnp.concatenate([hi, lo], axis=1)))
        withins = [_dot(hilo, u_ref[...]) for _, _, hilo in stats]
        weights = []
        for (p, _, sel, kind), (log_beta, log_not, _), within in zip(jobs, stats, withins):
            c = load(c_ref, p, sel)
            if kind == "left":
                c = jnp.where(has_left, c, -jnp.inf)
            a = jnp.exp2(log_beta + within + _rep_lanes(c, tq // LANES))
            weights.append(a.astype(BF16))
            store(c_ref, p, sel, c + jnp.sum(log_not, axis=1, keepdims=True))
        for (p, _, sel, _), ks, a in zip(jobs, starts, weights):
            store(acc_ref, p, sel, load(acc_ref, p, sel) + _dot(a, v_ref[0, pl.ds(ks, tq), pairs[p]]))

    def live(sel):
        worst = functools.reduce(jnp.maximum, [jnp.max(load(c_ref, p, sel)) for p in range(npairs)])
        return worst > SB_SKIP_LOG2

    c_ref[...] = jnp.zeros(c_ref.shape, F32)
    acc_ref[...] = jnp.zeros(acc_ref.shape, F32)
    left = jnp.maximum(i - 1, 0)
    run([(p, i, "all", "diag") for p in range(npairs)] + [(p, left, "all", "left") for p in range(npairs)])

    def cond(carry):
        jj, go = carry
        return jnp.logical_and(jj < i, go > 0)

    def body(carry):
        jj, _ = carry
        run([(p, i - 1 - jj, "all", "plain") for p in range(npairs)])
        return jj + 1, live("all").astype(jnp.int32)

    lax.while_loop(cond, body, (jnp.int32(1), live("all").astype(jnp.int32)))
    for p in range(npairs):
        o_ref[0, :, p * LANES:(p + 1) * LANES] = jnp.where(
            lane < SB_DIM, acc_ref[p, :tq], acc_ref[p, tq:]).astype(o_ref.dtype)


def _sb_attn(proj_bf, tq, nheads):
    b, s, _ = proj_bf.shape
    assert tq == SB_SUB
    ngroup = SB_HEADS // nheads
    width = nheads * SB_DIM
    idx = np.arange(SB_SUB)
    tri = (idx[:, None] > idx[None, :]).astype(np.float32)
    u = jnp.asarray(np.concatenate([tri, tri], axis=0), dtype=BF16)
    return pl.pallas_call(
        functools.partial(_sb_attn_kernel, tq=tq, nheads=nheads),
        grid=(b, ngroup, s // tq),
        in_specs=[pl.BlockSpec((1, tq, width), lambda bi, p, i: (bi, i, p)),
                  pl.BlockSpec((1, s, width), lambda bi, p, i: (bi, 0, ngroup + p)),
                  pl.BlockSpec((1, s, width), lambda bi, p, i: (bi, 0, 2 * ngroup + p)),
                  pl.BlockSpec(u.shape, lambda bi, p, i: (0, 0))],
        out_specs=pl.BlockSpec((1, tq, width), lambda bi, p, i: (bi, i, p)),
        out_shape=jax.ShapeDtypeStruct((b, s, SB_HEADS * SB_DIM), BF16),
        scratch_shapes=[pltpu.VMEM((nheads // 2, 2 * tq, LANES), BF16),
                        pltpu.VMEM((nheads // 2, 2 * tq, LANES), F32),
                        pltpu.VMEM((nheads // 2, 2 * tq, LANES), F32)],
        compiler_params=_params(("parallel", "parallel", "arbitrary")),
        name="sb_attn",
    )(proj_bf, proj_bf, proj_bf, u)


def _alternate(main, side, slots):
    done = object()
    stage = 0
    while True:
        if stage in slots:
            next(side, done)
        if next(main, done) is done:
            break
        stage += 1
    for _ in side:
        pass


def _sb_merge_kernel(x_ref, g_ref, ya_ref, yc_ref, wg_ref, wb_ref, wo_ref, q_ref, k_ref, v_ref, u_ref,
                     o_ref, qm_ref, c_ref, acc_ref, *, tq, nsb):
    i0 = 2 * (pl.program_id(0) % nsb)
    npairs = SB_HEADS // 2
    pairs = [slice(p * LANES, (p + 1) * LANES) for p in range(npairs)]
    lane = lax.broadcasted_iota(jnp.int32, (tq, LANES), 1)
    zero = jnp.zeros((tq, LANES), BF16)
    for t in range(2):
        for p in range(npairs):
            q_pair = q_ref[0, t * tq:(t + 1) * tq, pairs[p]]
            qm_ref[t, p, :tq] = jnp.where(lane < SB_DIM, q_pair, zero)
            qm_ref[t, p, tq:] = jnp.where(lane < SB_DIM, zero, q_pair)
    c_ref[...] = jnp.zeros(c_ref.shape, F32)
    acc_ref[...] = jnp.zeros(acc_ref.shape, F32)

    has_left = i0 > 0
    row = lax.broadcasted_iota(jnp.int32, (2 * tq, tq), 0)
    col = lax.broadcasted_iota(jnp.int32, (2 * tq, tq), 1)
    before = col < jnp.where(row < tq, row, row - tq)

    def run(jobs):
        starts = [pl.multiple_of(j * tq, tq) for _, _, j, _ in jobs]
        zs = [_dot_nt(qm_ref[t, p], k_ref[0, pl.ds(ks, tq), pairs[p]]) for (t, p, _, _), ks in zip(jobs, starts)]
        yield
        stats = []
        for z, (_, _, _, kind) in zip(zs, jobs):
            if kind == "diag":
                z = jnp.where(before, z, SB_MASKED_LOGIT)
            neg_abs = lax.bitcast_convert_type(lax.bitcast_convert_type(z, jnp.uint32) | SIGN_BIT, F32)
            log_beta = jnp.minimum(z, 0.0) - jnp.log(1.0 + jnp.exp2(neg_abs)) * LOG2E
            log_not = log_beta - z
            hi = log_not.astype(BF16)
            lo = (log_not - hi.astype(F32)).astype(BF16)
            stats.append((log_beta, log_not, jnp.concatenate([hi, lo], axis=1)))
        yield
        withins = [_dot(hilo, u_ref[...]) for _, _, hilo in stats]
        yield
        weights = []
        for (t, p, _, kind), (log_beta, log_not, _), within in zip(jobs, stats, withins):
            c = c_ref[t, p]
            if kind == "left":
                c = jnp.where(has_left, c, -jnp.inf)
            weights.append(jnp.exp2(log_beta + within + _rep_lanes(c, tq // LANES)).astype(BF16))
            c_ref[t, p] = c + jnp.sum(log_not, axis=1, keepdims=True)
        yield
        for (t, p, _, _), ks, a in zip(jobs, starts, weights):
            acc_ref[t, p] += _dot(a, v_ref[0, pl.ds(ks, tq), pairs[p]])
        yield

    def first_steps():
        for t in range(2):
            left, kind = (jnp.maximum(i0 - 1, 0), "left") if t == 0 else (i0, "plain")
            for p0 in range(0, npairs, 2):
                group = range(p0, p0 + 2)
                yield from run([(t, p, i0 + t, "diag") for p in group] + [(t, p, left, kind) for p in group])

    merge = {}

    def merge_side():
        x = x_ref[...]
        h = _rms(x, g_ref[...]).astype(BF16)
        yield
        gate = lambda n: jax.nn.sigmoid(_dot(h, wg_ref[:, n * D_MODEL:(n + 1) * D_MODEL]))
        gate_a = gate(0)
        yield
        part = gate_a * _dot(ya_ref[...], wb_ref[0])
        yield
        gate_c = gate(2)
        yield
        part = part + gate_c * _dot(yc_ref[...], wb_ref[2])
        yield
        merge.update(x=x, gate_b=gate(1), part=part)
        yield

    _alternate(first_steps(), merge_side(), slots={0, 1, 3, 6, 8, 11})

    def live(t):
        return (jnp.max(c_ref[t]) > SB_SKIP_LOG2).astype(jnp.int32)

    for t in range(2):
        it = i0 + t

        def cond(carry, it=it):
            jj, go = carry
            return jnp.logical_and(jj < it, go > 0)

        def body(carry, t=t, it=it):
            jj, _ = carry
            for _ in run([(t, p, it - 1 - jj, "plain") for p in range(npairs)]):
                pass
            return jj + 1, live(t)

        lax.while_loop(cond, body, (jnp.int32(1), live(t)))

    y_b = jnp.concatenate(
        [jnp.concatenate([jnp.where(lane < SB_DIM, acc_ref[t, p, :tq], acc_ref[t, p, tq:]) for p in range(npairs)],
                         axis=1) for t in range(2)], axis=0).astype(BF16)
    merged = merge["part"] + merge["gate_b"] * _dot(y_b, wb_ref[1])
    o_ref[...] = merge["x"] + _dot(merged.astype(BF16), wo_ref[...])


def _sb_merge(x2d, g, ya, yc, wg, wb, wo, proj_bf, l, tq):
    assert tq == SB_SUB
    t = x2d.shape[0]
    b, s, _ = proj_bf.shape
    tm = 2 * tq
    nsb = s // tm
    width = SB_HEADS * SB_DIM
    idx = np.arange(SB_SUB)
    tri = (idx[:, None] > idx[None, :]).astype(np.float32)
    u = jnp.asarray(np.concatenate([tri, tri], axis=0), dtype=BF16)
    full = lambda a: _layer_spec(a, l)
    row = lambda w: pl.BlockSpec((tm, w), lambda j: (j, 0))
    keys = lambda c: pl.BlockSpec((1, s, width), lambda j: (j // nsb, 0, c), pipeline_mode=pl.Buffered(1))
    scratch = lambda dt: pltpu.VMEM((2, SB_HEADS // 2, 2 * tq, LANES), dt)
    return pl.pallas_call(
        functools.partial(_sb_merge_kernel, tq=tq, nsb=nsb),
        grid=(t // tm,),
        in_specs=[row(D_MODEL), full(g), row(BRANCH_WIDTH), row(BRANCH_WIDTH), full(wg), full(wb), full(wo),
                  pl.BlockSpec((1, tm, width), lambda j: (j // nsb, j % nsb, 0)), keys(1), keys(2),
                  pl.BlockSpec(u.shape, lambda j: (0, 0))],
        out_specs=row(D_MODEL),
        out_shape=jax.ShapeDtypeStruct((t, D_MODEL), F32),
        scratch_shapes=[scratch(BF16), scratch(F32), scratch(F32)],
        compiler_params=_params(("parallel",)),
        name="sb_merge",
    )(x2d, g, ya, yc, wg, wb, wo, proj_bf, proj_bf, proj_bf, u)


def _retention_kernel(q_ref, k_ref, g_ref, v_ref, cos_ref, sin_ref, dec_ref, zeta_ref, xi_ref,
                      cd_ref, bd_ref, avg_ref, gn_ref, o_ref, state_ref, *, nchunk):
    @pl.when(pl.program_id(1) == 0)
    def _():
        state_ref[...] = jnp.zeros(state_ref.shape, F32)

    rows = nchunk * CHUNK
    width = RET_HEADS * RET_DK
    npair = RET_HEADS // 2
    half = RET_DK // 2
    lane_w = lax.broadcasted_iota(jnp.int32, (rows, width), 1)
    first_half = (lane_w % RET_DK) < half
    head0 = (lane_w % LANES) < RET_DK
    cos_t = _rep_lanes(cos_ref[0], width // LANES)
    sin_t = _rep_lanes(sin_ref[0], width // LANES)

    def rope(x):
        swapped = jnp.where(first_half, pltpu.roll(x, width - half, 1), pltpu.roll(x, half, 1))
        return x * cos_t + swapped * sin_t

    q = rope(q_ref[0])
    k = rope(k_ref[0])
    kz = (k * zeta_ref[...]).astype(BF16)
    qb = q.astype(BF16)
    kb = k.astype(BF16)
    zero = jnp.zeros_like(qb)
    q0 = jnp.where(head0, qb, zero)
    q1 = jnp.where(head0, zero, qb)
    v = v_ref[0]
    units = [(c, p) for c in range(nchunk) for p in range(npair)]
    rs = lambda c: slice(c * CHUNK, (c + 1) * CHUNK)
    ls = lambda p: slice(p * LANES, (p + 1) * LANES)
    scores = [_dot_nt(jnp.concatenate([q0[rs(c), ls(p)], q1[rs(c), ls(p)]], axis=0), kb[rs(c), ls(p)])
              for c, p in units]
    scores = [(s * dec_ref[p]).astype(BF16) for s, (c, p) in zip(scores, units)]
    intra = [_dot(s, v[rs(c), ls(p)]) for s, (c, p) in zip(scores, units)]
    kvs = [lax.dot_general(kz[rs(c), ls(p)], v[rs(c), ls(p)], TN_DIMS, preferred_element_type=F32)
           for c, p in units]
    lane = lax.broadcasted_iota(jnp.int32, (CHUNK, LANES), 1)
    chunks = []
    for c in range(nchunk):
        ys = []
        for p in range(npair):
            u = c * npair + p
            state = state_ref[p]
            y_cross = _dot(qb[rs(c), ls(p)], state.astype(BF16)) * xi_ref[:, ls(p)]
            state_ref[p] = state * cd_ref[p] + kvs[u] * bd_ref[...]
            y_intra = jnp.where(lane < RET_DV, intra[u][:CHUNK], intra[u][CHUNK:])
            ys.append(y_intra + y_cross)
        chunks.append(jnp.concatenate(ys, axis=1))
    y = jnp.concatenate(chunks, axis=0)
    avg = avg_ref[...]
    d = y - _dot_hilo(y, avg)
    var = _dot_hilo(d * d, avg)
    yn = d * lax.rsqrt(var + EPS) * gn_ref[...]
    g = g_ref[0]
    o_ref[0] = (g * jax.nn.sigmoid(g) * yn).astype(o_ref.dtype)


def _retention_consts(nchunk):
    h = np.arange(RET_HEADS, dtype=np.float64)
    log_gamma = np.log1p(-np.exp2(-5.0 - h))
    idx = np.arange(CHUNK, dtype=np.float64)
    diff = idx[:, None] - idx[None, :]
    dec = np.where(diff[None] >= 0, np.exp(np.maximum(diff, 0.0)[None] * log_gamma[:, None, None]), 0.0)
    npair = RET_HEADS // 2
    dec = dec.reshape(npair, 2 * CHUNK, CHUNK)
    zeta = np.exp((CHUNK - 1 - idx)[:, None] * log_gamma[None, :])
    xi = np.exp((idx + 1.0)[:, None] * log_gamma[None, :])
    zeta_l = np.tile(np.repeat(zeta, RET_DK, axis=1), (nchunk, 1))
    xi_l = np.repeat(xi, RET_DV, axis=1)
    chunk_decay = np.exp(CHUNK * log_gamma)
    head_of = np.arange(LANES) // RET_DK
    bd = (head_of[:, None] == head_of[None, :]).astype(np.float64)
    cd = np.stack([bd * chunk_decay[2 * p + head_of][:, None] for p in range(npair)])
    gh = np.arange(RET_HEADS * RET_DV) // RET_DV
    avg = (gh[:, None] == gh[None, :]).astype(np.float64) / RET_DV
    f = lambda a: jnp.asarray(a, dtype=F32)
    return f(dec), f(zeta_l), f(xi_l), f(cd), f(bd), jnp.asarray(avg, dtype=BF16)


def _retention(proj_ret, proj_bf, cos_t, sin_t, gn, nchunk):
    b, s, _ = proj_ret.shape
    width = RET_HEADS * RET_DK
    rows = nchunk * CHUNK
    dec, zeta_l, xi_l, cd, bd, avg = _retention_consts(nchunk)
    full = lambda a: pl.BlockSpec(a.shape, lambda bi, n: (0,) * a.ndim)
    blk = lambda c: pl.BlockSpec((1, rows, width), lambda bi, n: (bi, n, c))
    tab = pl.BlockSpec((1, rows, LANES), lambda bi, n: (bi, n, 0))
    return pl.pallas_call(
        functools.partial(_retention_kernel, nchunk=nchunk),
        grid=(b, s // rows),
        in_specs=[blk(0), blk(1), blk(2), blk(3), tab, tab,
                  full(dec), full(zeta_l), full(xi_l), full(cd), full(bd), full(avg), full(gn)],
        out_specs=blk(0),
        out_shape=jax.ShapeDtypeStruct((b, s, RET_HEADS * RET_DV), BF16),
        scratch_shapes=[pltpu.VMEM((RET_HEADS // 2, LANES, LANES), F32)],
        compiler_params=_params(("parallel", "arbitrary")),
        name="retention",
    )(proj_ret, proj_ret, proj_ret, proj_bf, cos_t, sin_t, dec, zeta_l, xi_l, cd, bd, avg, gn)


def _merge_kernel(x_ref, g_ref, ya_ref, yb_ref, yc_ref, wg_ref, wb_ref, wo_ref, o_ref):
    x = x_ref[...]
    h = _rms(x, g_ref[...]).astype(BF16)
    merged = None
    for n, y_ref in enumerate((ya_ref, yb_ref, yc_ref)):
        gate = jax.nn.sigmoid(_dot(h, wg_ref[:, n * D_MODEL:(n + 1) * D_MODEL]))
        term = gate * _dot(y_ref[...], wb_ref[n])
        merged = term if merged is None else merged + term
    o_ref[...] = x + _dot(merged.astype(BF16), wo_ref[...])


def _merge(x2d, g, ya, yb, yc, wg, wb, wo, l, tm):
    t = x2d.shape[0]
    full = lambda a: _layer_spec(a, l)
    row = lambda w: pl.BlockSpec((tm, w), lambda i: (i, 0))
    return pl.pallas_call(
        _merge_kernel,
        grid=(t // tm,),
        in_specs=[row(D_MODEL), full(g), row(BRANCH_WIDTH), row(BRANCH_WIDTH), row(BRANCH_WIDTH),
                  full(wg), full(wb), full(wo)],
        out_specs=row(D_MODEL),
        out_shape=jax.ShapeDtypeStruct((t, D_MODEL), F32),
        compiler_params=_params(("parallel",)),
        name="merge",
    )(x2d, g, ya, yb, yc, wg, wb, wo)


MLP_CHUNK = 1024


def _mlp_kernel(x_ref, g_ref, wu_ref, wd_ref, gf_ref, o_ref, *, final_norm):
    x = x_ref[...]
    h = _rms(x, g_ref[...]).astype(BF16)
    acc = None
    for c0 in range(0, D_FF, MLP_CHUNK):
        u = jnp.maximum(_dot(h, wu_ref[:, c0:c0 + MLP_CHUNK]), 0.0)
        part = _dot((u * u).astype(BF16), wd_ref[c0:c0 + MLP_CHUNK, :])
        acc = part if acc is None else acc + part
    y = x + acc
    o_ref[...] = _rms(y, gf_ref[...]) if final_norm else y


def _mlp(x2d, g, wu, wd, gf, l, final_norm, tm):
    t, d = x2d.shape
    full = lambda a: _layer_spec(a, l)
    row = pl.BlockSpec((tm, d), lambda i: (i, 0))
    return pl.pallas_call(
        functools.partial(_mlp_kernel, final_norm=final_norm),
        grid=(t // tm,),
        in_specs=[row, full(g), full(wu), full(wd), pl.BlockSpec(gf.shape, lambda i: (0, 0))],
        out_specs=row,
        out_shape=jax.ShapeDtypeStruct((t, d), F32),
        compiler_params=_params(("parallel",)),
        name="mlp",
    )(x2d, g, wu, wd, gf)


def _pick(n, pref):
    t = min(pref, n)
    while n % t:
        t //= 2
    return t


def _rope_tables(positions):
    pos = positions.astype(F32)[..., None]
    b, s = positions.shape
    lane = np.arange(LANES)

    def cs(dim, freq_of_lane):
        inv_freq = ROPE_BASE ** (-jnp.arange(0, dim, 2, dtype=F32) / dim)
        ang = pos * inv_freq[freq_of_lane]
        return jnp.cos(ang), jnp.sin(ang)

    hm = MLA_ROPE // 2
    is_rope = (lane >= MLA_NOPE) & (lane < MLA_NOPE + MLA_ROPE)
    cm, sm = cs(MLA_ROPE, np.where(is_rope, (lane - MLA_NOPE) % hm, 0))
    sign_m = np.where(lane < MLA_NOPE + hm, -1.0, 1.0).astype(np.float32)
    mla_cos = jnp.where(lane < MLA_NOPE, 1.0, jnp.where(is_rope, cm, 0.0)).reshape(b * s, LANES)
    mla_sin = jnp.where(is_rope, sign_m * sm, 0.0).reshape(b * s, LANES)
    hr = RET_DK // 2
    ret_cos, sr = cs(RET_DK, lane % hr)
    ret_sin = np.where(lane % RET_DK < hr, -1.0, 1.0).astype(np.float32) * sr
    return mla_cos, mla_sin, ret_cos, ret_sin


def _prep_weights(w_in, mla_w_uq, mla_w_ukv, w_branch, w_out, w_up, w_down):
    depth, d, _ = w_in.shape
    offs = [int(o) for o in np.cumsum((0, MLA_Q_RANK, MLA_KV_RANK, MLA_ROPE) + (BRANCH_WIDTH,) * 7
                                      + (N_BRANCH * D_MODEL,))]
    col_scale = np.ones(offs[-1], np.float32)
    col_scale[offs[3]:offs[4]] = SB_DIM ** -0.5 * LOG2E
    col_scale[offs[7]:offs[8]] = RET_DK ** -0.5
    w = (w_in * col_scale).astype(BF16)
    seg = lambda n: w[:, :, offs[n]:offs[n + 1]]
    c_q, c_kv, k_pe, sb_q, sb_k, sb_v, r_q, r_k, r_v, r_g, gates = (seg(n) for n in range(11))
    zeros = lambda *shape: jnp.zeros(shape, BF16)
    hr = MLA_ROPE // 2
    pad = LANES - MLA_NOPE - MLA_ROPE
    w_proj = jnp.concatenate(
        [c_q, c_kv,
         zeros(depth, d, MLA_NOPE), k_pe, zeros(depth, d, pad),
         zeros(depth, d, MLA_NOPE), k_pe[..., hr:], k_pe[..., :hr], zeros(depth, d, pad),
         r_q, r_k, r_g,
         sb_q, sb_k, sb_v, r_v], axis=-1)

    uq = mla_w_uq.astype(BF16).reshape(depth, MLA_Q_RANK, MLA_HEADS, MLA_NOPE + MLA_ROPE)
    zq = lambda width: jnp.zeros((depth, MLA_Q_RANK, MLA_HEADS, width), BF16)
    nope, rope = uq[..., :MLA_NOPE], uq[..., MLA_NOPE:]
    wqa = jnp.concatenate([nope, rope, zq(pad)], axis=-1)
    wqb = jnp.concatenate([zq(MLA_NOPE), rope[..., hr:], rope[..., :hr], zq(pad)], axis=-1)
    ukv = mla_w_ukv.astype(BF16).reshape(depth, MLA_KV_RANK, MLA_HEADS, MLA_NOPE + MLA_V)
    wk = jnp.concatenate([ukv[..., :MLA_NOPE],
                          jnp.zeros((depth, MLA_KV_RANK, MLA_HEADS, LANES - MLA_NOPE), BF16)], axis=-1)
    wv = ukv[..., MLA_NOPE:]
    flat = lambda a: a.reshape(depth, a.shape[1], -1)
    bf = lambda a: a.astype(BF16)
    return dict(w_in=w_proj, w_gate=gates,
                wqa=flat(wqa), wqb=flat(wqb), wk_t=jnp.swapaxes(flat(wk), 1, 2), wv=flat(wv),
                wb=bf(w_branch), wo=bf(w_out), wu=bf(w_up), wd=bf(w_down))


def kernel(x, positions, norm_mix_g, w_in, mla_q_norm_g, mla_w_uq, mla_kv_norm_g, mla_w_ukv, ret_norm_g, w_branch, w_out, norm_mlp_g, w_up, w_down, final_norm_g):
    b, s, d = x.shape
    depth = w_in.shape[0]
    t = b * s
    assert d == D_MODEL and s % CHUNK == 0
    tm_row = _pick(t, 512)
    t_mla = _pick(s, 1024)
    t_sb = max(_pick(s, 256), SB_SUB)
    w = _prep_weights(w_in, mla_w_uq, mla_w_ukv, w_branch, w_out, w_up, w_down)
    mla_cos, mla_sin, ret_cos, ret_sin = _rope_tables(positions)
    vec = lambda a: a.reshape(1, -1)
    gains = lambda a: a.reshape(depth, 1, -1)
    g_mix, g_q, g_kv, g_mlp = gains(norm_mix_g), gains(mla_q_norm_g), gains(mla_kv_norm_g), gains(norm_mlp_g)

    x2d = x.reshape(t, d)
    for l in range(depth):
        proj_ret, proj_bf, q_a, k_t, v_a = _in_proj(x2d, g_mix, w["w_in"], mla_cos, mla_sin, g_q, g_kv,
                                                    w["wqa"], w["wqb"], w["wk_t"], w["wv"], l, b, tm_row)
        proj_bf = proj_bf.reshape(b, s, -1)

        y_a = _mla_attn(q_a.reshape(b, s, -1), k_t, v_a.reshape(b, s, -1), t_mla, 4)
        y_c = _retention(proj_ret.reshape(b, s, -1), proj_bf, ret_cos, ret_sin, vec(ret_norm_g[l]),
                         _pick(s // CHUNK, 4))

        x2d = _sb_merge(x2d, g_mix, y_a.reshape(t, -1), y_c.reshape(t, -1), w["w_gate"], w["wb"], w["wo"],
                        proj_bf, l, t_sb)
        x2d = _mlp(x2d, g_mlp, w["wu"], w["wd"], vec(final_norm_g), l, l == depth - 1, tm_row)
    return x2d.reshape(b, s, d)
```

```python
import functools

import numpy as np
import jax
import jax.numpy as jnp
from jax import lax
from jax.experimental import pallas as pl
from jax.experimental.pallas import tpu as pltpu

F32 = jnp.float32
BF16 = jnp.bfloat16

D_MODEL = 1024
EPS = 1e-6
ROPE_BASE = 10000.0
CHUNK = 128

MLA_HEADS = 8
MLA_NOPE = 64
MLA_ROPE = 32
MLA_V = 64
MLA_Q_RANK = 384
MLA_KV_RANK = 256
SB_HEADS = 8
SB_DIM = 64
RET_HEADS = 8
RET_DK = 64
RET_DV = 64
BRANCH_WIDTH = 512
N_BRANCH = 3
D_FF = 4 * D_MODEL

LANES = 128
LAT_WIDTH = MLA_Q_RANK + MLA_KV_RANK + 2 * LANES
VMEM_LIMIT = 56 * 1024 * 1024
LOG2E = 1.4426950408889634
MLA_Q_SCALE = (MLA_NOPE + MLA_ROPE) ** -0.5 * LOG2E
SIGN_BIT = np.uint32(0x80000000)
SB_MASKED_LOGIT = -1e30
SB_SUB = 256
SB_SKIP_LOG2 = -135.0

NT_DIMS = (((1,), (1,)), ((), ()))
TN_DIMS = (((0,), (0,)), ((), ()))


def _dot(a, b):
    return jnp.dot(a, b, preferred_element_type=F32)


def _dot_nt(a, b):
    return lax.dot_general(a, b, NT_DIMS, preferred_element_type=F32)


def _dot_hilo(x, m_bf16):
    hi = x.astype(BF16)
    lo = (x - hi.astype(F32)).astype(BF16)
    return _dot(hi, m_bf16) + _dot(lo, m_bf16)


def _rep_lanes(x, n):
    return x if n == 1 else jnp.concatenate([x] * n, axis=1)


def _rms(x, g):
    return x * lax.rsqrt(jnp.mean(x * x, axis=-1, keepdims=True) + EPS) * g


def _layer_spec(stacked, l):
    nd = stacked.ndim - 1
    return pl.BlockSpec((None,) + stacked.shape[1:], lambda *_: (l,) + (0,) * nd, pipeline_mode=pl.Buffered(1))


def _params(sem):
    return pltpu.CompilerParams(dimension_semantics=sem, vmem_limit_bytes=VMEM_LIMIT)


IN_GROUPS = ((3 * BRANCH_WIDTH, F32),
             (4 * BRANCH_WIDTH, BF16))
IN_CHUNK = 512


def _mla_prep(lat, cos_t, sin_t, gq, gkv, wqa, wqb, wk_t, wv, q_ref, k_ref, v_ref):
    c_q = lat[:, :MLA_Q_RANK]
    c_kv = lat[:, MLA_Q_RANK:MLA_Q_RANK + MLA_KV_RANK]
    kpe_a = lat[:, MLA_Q_RANK + MLA_KV_RANK:MLA_Q_RANK + MLA_KV_RANK + LANES]
    kpe_b = lat[:, MLA_Q_RANK + MLA_KV_RANK + LANES:]
    nq = _rms(c_q, gq).astype(BF16)
    nkv = _rms(c_kv, gkv).astype(BF16)
    q = _dot(nq, wqa) * _rep_lanes(cos_t, MLA_HEADS) + _dot(nq, wqb) * _rep_lanes(sin_t, MLA_HEADS)
    q_ref[...] = (q * MLA_Q_SCALE).astype(BF16)
    k_rot_t = jnp.transpose(kpe_a * cos_t + kpe_b * sin_t)
    k_t = _dot_nt(wk_t, nkv) + jnp.concatenate([k_rot_t] * MLA_HEADS, axis=0)
    k_ref[0] = k_t.astype(BF16)
    v_ref[...] = _dot(nkv, wv).astype(BF16)


def _in_proj_kernel(x_ref, g_ref, w_ref, cos_ref, sin_ref, gq_ref, gkv_ref, wqa_ref, wqb_ref, wk_ref, wv_ref,
                    ret_ref, bf_ref, q_ref, k_ref, v_ref):
    h = _rms(x_ref[...], g_ref[...]).astype(BF16)
    lat = jnp.concatenate([_dot(h, w_ref[:, c0:min(c0 + IN_CHUNK, LAT_WIDTH)])
                           for c0 in range(0, LAT_WIDTH, IN_CHUNK)], axis=1)
    _mla_prep(lat, cos_ref[...], sin_ref[...], gq_ref[...], gkv_ref[...], wqa_ref[...], wqb_ref[...],
              wk_ref[...], wv_ref[...], q_ref, k_ref, v_ref)
    col = LAT_WIDTH
    for o_ref, (width, _) in zip((ret_ref, bf_ref), IN_GROUPS):
        for c0 in range(0, width, IN_CHUNK):
            o_ref[:, c0:c0 + IN_CHUNK] = _dot(h, w_ref[:, col + c0:col + c0 + IN_CHUNK]).astype(o_ref.dtype)
        col += width


def _in_proj(x2d, g, w, cos_t, sin_t, gq, gkv, wqa, wqb, wk_t, wv, l, b, tm):
    t, d = x2d.shape
    hw = MLA_HEADS * LANES
    nsb = t // b // tm
    full = lambda a: _layer_spec(a, l)
    row = lambda width: pl.BlockSpec((tm, width), lambda i: (i, 0))
    return pl.pallas_call(
        _in_proj_kernel,
        grid=(t // tm,),
        in_specs=[row(d), full(g), full(w), row(LANES), row(LANES), full(gq), full(gkv),
                  full(wqa), full(wqb), full(wk_t), full(wv)],
        out_specs=[row(IN_GROUPS[0][0]), row(IN_GROUPS[1][0]), row(hw),
                   pl.BlockSpec((1, hw, tm), lambda i: (i // nsb, 0, i % nsb)), row(MLA_HEADS * MLA_V)],
        out_shape=[jax.ShapeDtypeStruct((t, IN_GROUPS[0][0]), IN_GROUPS[0][1]),
                   jax.ShapeDtypeStruct((t, IN_GROUPS[1][0]), IN_GROUPS[1][1]),
                   jax.ShapeDtypeStruct((t, hw), BF16), jax.ShapeDtypeStruct((b, hw, t // b), BF16),
                   jax.ShapeDtypeStruct((t, MLA_HEADS * MLA_V), BF16)],
        compiler_params=_params(("parallel",)),
        name="in_proj",
    )(x2d, g, w, cos_t, sin_t, gq, gkv, wqa, wqb, wk_t, wv)


def _mla_attn_kernel(q_ref, kt_ref, v_ref, o_ref, m_ref, l_ref, acc_ref, *, tq, tk, nheads):
    i = pl.program_id(2)
    npairs = nheads // 2
    heads = [slice(e * LANES, (e + 1) * LANES) for e in range(nheads)]

    def tiles(ks, nk, r0, nr, masked):
        rows = slice(r0, r0 + nr)
        scores = [_dot(q_ref[0, rows, heads[e]], kt_ref[0, heads[e], pl.ds(ks, nk)]) for e in range(nheads)]
        if masked:
            q_pos = i * tq + r0 + lax.broadcasted_iota(jnp.int32, (nr, nk), 0)
            k_pos = ks + lax.broadcasted_iota(jnp.int32, (nr, nk), 1)
            scores = [jnp.where(k_pos <= q_pos, s, -jnp.inf) for s in scores]
        probs, alphas = [], []
        for e, s in enumerate(scores):
            m_prev = m_ref[e, rows]
            m_next = jnp.maximum(m_prev, jnp.max(s, axis=1, keepdims=True))
            alpha = jnp.exp2(m_prev - m_next)
            p = jnp.exp2(s - _rep_lanes(m_next, nk // LANES))
            partial = functools.reduce(jnp.add, [p[:, c:c + LANES] for c in range(0, nk, LANES)])
            l_ref[e, rows] = alpha * l_ref[e, rows] + partial
            m_ref[e, rows] = m_next
            probs.append(p.astype(BF16))
            alphas.append(alpha)
        for p in range(npairs):
            pv = _dot(jnp.concatenate([probs[2 * p], probs[2 * p + 1]], axis=0), v_ref[0, pl.ds(ks, nk), heads[p]])
            for e in range(2):
                arows = slice(e * tq + r0, e * tq + r0 + nr)
                acc_ref[p, arows] = alphas[2 * p + e] * acc_ref[p, arows] + pv[e * nr:(e + 1) * nr]

    m_ref[...] = jnp.full(m_ref.shape, -jnp.inf, F32)
    l_ref[...] = jnp.zeros(l_ref.shape, F32)
    acc_ref[...] = jnp.zeros(acc_ref.shape, F32)

    def body(j, carry):
        tiles(pl.multiple_of(j * tq, tq), tq, 0, tq, False)
        return carry

    lax.fori_loop(0, i, body, 0)
    diag = pl.multiple_of(i * tq, tq)
    tiles(diag, tk, 0, tq, True)
    tiles(diag + tk, tk, tk, tq - tk, True)
    lane = lax.broadcasted_iota(jnp.int32, (tq, LANES), 1)
    denom = [jnp.sum(l_ref[e], axis=1, keepdims=True) for e in range(nheads)]
    for p in range(npairs):
        o_ref[0, :, heads[p]] = jnp.where(
            lane < MLA_V, acc_ref[p, :tq] / denom[2 * p], acc_ref[p, tq:] / denom[2 * p + 1]).astype(o_ref.dtype)


def _mla_attn(q, k_t, v, tq, nheads):
    b, s, _ = q.shape
    tk = tq // 2
    return pl.pallas_call(
        functools.partial(_mla_attn_kernel, tq=tq, tk=tk, nheads=nheads),
        grid=(b, MLA_HEADS // nheads, s // tq),
        in_specs=[pl.BlockSpec((1, tq, nheads * LANES), lambda bi, p, i: (bi, i, p)),
                  pl.BlockSpec((1, nheads * LANES, s), lambda bi, p, i: (bi, p, 0)),
                  pl.BlockSpec((1, s, nheads * MLA_V), lambda bi, p, i: (bi, 0, p))],
        out_specs=pl.BlockSpec((1, tq, nheads * MLA_V), lambda bi, p, i: (bi, i, p)),
        out_shape=jax.ShapeDtypeStruct((b, s, MLA_HEADS * MLA_V), BF16),
        scratch_shapes=[pltpu.VMEM((nheads, tq, LANES), F32), pltpu.VMEM((nheads, tq, LANES), F32),
                        pltpu.VMEM((nheads // 2, 2 * tq, LANES), F32)],
        compiler_params=_params(("parallel", "parallel", "arbitrary")),
        name="mla_attn",
    )(q, k_t, v)


def _sb_attn_kernel(q_ref, k_ref, v_ref, u_ref, o_ref, qm_ref, c_ref, acc_ref, *, tq, nheads):
    i = pl.program_id(2)
    npairs = nheads // 2
    lane = lax.broadcasted_iota(jnp.int32, (tq, LANES), 1)
    zero = jnp.zeros((tq, LANES), BF16)
    for p in range(npairs):
        q_pair = q_ref[0, :, p * LANES:(p + 1) * LANES]
        qm_ref[p, :tq] = jnp.where(lane < SB_DIM, q_pair, zero)
        qm_ref[p, tq:] = jnp.where(lane < SB_DIM, zero, q_pair)

    pairs = [slice(p * LANES, (p + 1) * LANES) for p in range(npairs)]
    has_left = i > 0
    row = lax.broadcasted_iota(jnp.int32, (2 * tq, tq), 0)
    col = lax.broadcasted_iota(jnp.int32, (2 * tq, tq), 1)
    before = col < jnp.where(row < tq, row, row - tq)

    def run(jobs):
        starts = [pl.multiple_of(j * tq, tq) for _, j, _ in jobs]
        zs = [_dot_nt(qm_ref[p], k_ref[0, pl.ds(ks, tq), pairs[p]]) for (p, _, _), ks in zip(jobs, starts)]
        stats = []
        for z, (_, _, kind) in zip(zs, jobs):
            if kind == "diag":
                z = jnp.where(before, z, SB_MASKED_LOGIT)
            neg_abs = lax.bitcast_convert_type(lax.bitcast_convert_type(z, jnp.uint32) | SIGN_BIT, F32)
            log_beta = jnp.minimum(z, 0.0) - jnp.log(1.0 + jnp.exp2(neg_abs)) * LOG2E
            log_not = log_beta - z
            hi = log_not.astype(BF16)
            lo = (log_not - hi.astype(F32)).astype(BF16)
            stats.append((log_beta, log_not, jnp.concatenate([hi, lo], axis=1)))
        withins = [_dot(hilo, u_ref[...]) for _, _, hilo in stats]
        weights = []
        for (p, _, kind), (log_beta, log_not, _), within in zip(jobs, stats, withins):
            c = c_ref[p]
            if kind == "left":
                c = jnp.where(has_left, c, -jnp.inf)
            a = jnp.exp2(log_beta + within + _rep_lanes(c, tq // LANES))
            weights.append(a.astype(BF16))
            c_ref[p] = c + jnp.sum(log_not, axis=1, keepdims=True)
        for (p, _, _), ks, a in zip(jobs, starts, weights):
            acc_ref[p] += _dot(a, v_ref[0, pl.ds(ks, tq), pairs[p]])

    def live():
        return (jnp.max(c_ref[...]) > SB_SKIP_LOG2).astype(jnp.int32)

    c_ref[...] = jnp.zeros(c_ref.shape, F32)
    acc_ref[...] = jnp.zeros(acc_ref.shape, F32)
    left = jnp.maximum(i - 1, 0)
    run([(p, i, "diag") for p in range(npairs)] + [(p, left, "left") for p in range(npairs)])

    def cond(carry):
        jj, go = carry
        return jnp.logical_and(jj < i, go > 0)

    def body(carry):
        jj, _ = carry
        run([(p, i - 1 - jj, "plain") for p in range(npairs)])
        return jj + 1, live()

    lax.while_loop(cond, body, (jnp.int32(1), live()))
    for p in range(npairs):
        o_ref[0, :, p * LANES:(p + 1) * LANES] = jnp.where(
            lane < SB_DIM, acc_ref[p, :tq], acc_ref[p, tq:]).astype(o_ref.dtype)


def _sb_attn(proj_bf, tq, nheads):
    b, s, _ = proj_bf.shape
    assert tq == SB_SUB
    ngroup = SB_HEADS // nheads
    width = nheads * SB_DIM
    idx = np.arange(SB_SUB)
    tri = (idx[:, None] > idx[None, :]).astype(np.float32)
    u = jnp.asarray(np.concatenate([tri, tri], axis=0), dtype=BF16)
    return pl.pallas_call(
        functools.partial(_sb_attn_kernel, tq=tq, nheads=nheads),
        grid=(b, ngroup, s // tq),
        in_specs=[pl.BlockSpec((1, tq, width), lambda bi, p, i: (bi, i, p)),
                  pl.BlockSpec((1, s, width), lambda bi, p, i: (bi, 0, ngroup + p)),
                  pl.BlockSpec((1, s, width), lambda bi, p, i: (bi, 0, 2 * ngroup + p)),
                  pl.BlockSpec(u.shape, lambda bi, p, i: (0, 0))],
        out_specs=pl.BlockSpec((1, tq, width), lambda bi, p, i: (bi, i, p)),
        out_shape=jax.ShapeDtypeStruct((b, s, SB_HEADS * SB_DIM), BF16),
        scratch_shapes=[pltpu.VMEM((nheads // 2, 2 * tq, LANES), BF16),
                        pltpu.VMEM((nheads // 2, 2 * tq, LANES), F32),
                        pltpu.VMEM((nheads // 2, 2 * tq, LANES), F32)],
        compiler_params=_params(("parallel", "parallel", "arbitrary")),
        name="sb_attn",
    )(proj_bf, proj_bf, proj_bf, u)


def _retention_kernel(q_ref, k_ref, g_ref, v_ref, cos_ref, sin_ref, dec_ref, zeta_ref, xi_ref,
                      cd_ref, bd_ref, avg_ref, gn_ref, o_ref, state_ref, *, nchunk):
    @pl.when(pl.program_id(1) == 0)
    def _():
        state_ref[...] = jnp.zeros(state_ref.shape, F32)

    rows = nchunk * CHUNK
    width = RET_HEADS * RET_DK
    npair = RET_HEADS // 2
    half = RET_DK // 2
    lane_w = lax.broadcasted_iota(jnp.int32, (rows, width), 1)
    first_half = (lane_w % RET_DK) < half
    head0 = (lane_w % LANES) < RET_DK
    cos_t = _rep_lanes(cos_ref[0], width // LANES)
    sin_t = _rep_lanes(sin_ref[0], width // LANES)

    def rope(x):
        swapped = jnp.where(first_half, pltpu.roll(x, width - half, 1), pltpu.roll(x, half, 1))
        return x * cos_t + swapped * sin_t

    q = rope(q_ref[0])
    k = rope(k_ref[0])
    kz = (k * zeta_ref[...]).astype(BF16)
    qb = q.astype(BF16)
    kb = k.astype(BF16)
    zero = jnp.zeros_like(qb)
    q0 = jnp.where(head0, qb, zero)
    q1 = jnp.where(head0, zero, qb)
    v = v_ref[0]
    units = [(c, p) for c in range(nchunk) for p in range(npair)]
    rs = lambda c: slice(c * CHUNK, (c + 1) * CHUNK)
    ls = lambda p: slice(p * LANES, (p + 1) * LANES)
    scores = [_dot_nt(jnp.concatenate([q0[rs(c), ls(p)], q1[rs(c), ls(p)]], axis=0), kb[rs(c), ls(p)])
              for c, p in units]
    scores = [(s * dec_ref[p]).astype(BF16) for s, (c, p) in zip(scores, units)]
    intra = [_dot(s, v[rs(c), ls(p)]) for s, (c, p) in zip(scores, units)]
    kvs = [lax.dot_general(kz[rs(c), ls(p)], v[rs(c), ls(p)], TN_DIMS, preferred_element_type=F32)
           for c, p in units]
    lane = lax.broadcasted_iota(jnp.int32, (CHUNK, LANES), 1)
    chunks = []
    for c in range(nchunk):
        ys = []
        for p in range(npair):
            u = c * npair + p
            state = state_ref[p]
            y_cross = _dot(qb[rs(c), ls(p)], state.astype(BF16)) * xi_ref[:, ls(p)]
            state_ref[p] = state * cd_ref[p] + kvs[u] * bd_ref[...]
            y_intra = jnp.where(lane < RET_DV, intra[u][:CHUNK], intra[u][CHUNK:])
            ys.append(y_intra + y_cross)
        chunks.append(jnp.concatenate(ys, axis=1))
    y = jnp.concatenate(chunks, axis=0)
    avg = avg_ref[...]
    d = y - _dot_hilo(y, avg)
    var = _dot_hilo(d * d, avg)
    yn = d * lax.rsqrt(var + EPS) * gn_ref[...]
    g = g_ref[0]
    o_ref[0] = (g * jax.nn.sigmoid(g) * yn).astype(o_ref.dtype)


def _retention_consts(nchunk):
    h = np.arange(RET_HEADS, dtype=np.float64)
    log_gamma = np.log1p(-np.exp2(-5.0 - h))
    idx = np.arange(CHUNK, dtype=np.float64)
    diff = idx[:, None] - idx[None, :]
    dec = np.where(diff[None] >= 0, np.exp(np.maximum(diff, 0.0)[None] * log_gamma[:, None, None]), 0.0)
    npair = RET_HEADS // 2
    dec = dec.reshape(npair, 2 * CHUNK, CHUNK)
    zeta = np.exp((CHUNK - 1 - idx)[:, None] * log_gamma[None, :])
    xi = np.exp((idx + 1.0)[:, None] * log_gamma[None, :])
    zeta_l = np.tile(np.repeat(zeta, RET_DK, axis=1), (nchunk, 1))
    xi_l = np.repeat(xi, RET_DV, axis=1)
    chunk_decay = np.exp(CHUNK * log_gamma)
    head_of = np.arange(LANES) // RET_DK
    bd = (head_of[:, None] == head_of[None, :]).astype(np.float64)
    cd = np.stack([bd * chunk_decay[2 * p + head_of][:, None] for p in range(npair)])
    gh = np.arange(RET_HEADS * RET_DV) // RET_DV
    avg = (gh[:, None] == gh[None, :]).astype(np.float64) / RET_DV
    f = lambda a: jnp.asarray(a, dtype=F32)
    return f(dec), f(zeta_l), f(xi_l), f(cd), f(bd), jnp.asarray(avg, dtype=BF16)


def _retention(proj_ret, proj_bf, cos_t, sin_t, gn, nchunk):
    b, s, _ = proj_ret.shape
    width = RET_HEADS * RET_DK
    rows = nchunk * CHUNK
    dec, zeta_l, xi_l, cd, bd, avg = _retention_consts(nchunk)
    full = lambda a: pl.BlockSpec(a.shape, lambda bi, n: (0,) * a.ndim)
    blk = lambda c: pl.BlockSpec((1, rows, width), lambda bi, n: (bi, n, c))
    tab = pl.BlockSpec((1, rows, LANES), lambda bi, n: (bi, n, 0))
    return pl.pallas_call(
        functools.partial(_retention_kernel, nchunk=nchunk),
        grid=(b, s // rows),
        in_specs=[blk(0), blk(1), blk(2), blk(3), tab, tab,
                  full(dec), full(zeta_l), full(xi_l), full(cd), full(bd), full(avg), full(gn)],
        out_specs=blk(0),
        out_shape=jax.ShapeDtypeStruct((b, s, RET_HEADS * RET_DV), BF16),
        scratch_shapes=[pltpu.VMEM((RET_HEADS // 2, LANES, LANES), F32)],
        compiler_params=_params(("parallel", "arbitrary")),
        name="retention",
    )(proj_ret, proj_ret, proj_ret, proj_bf, cos_t, sin_t, dec, zeta_l, xi_l, cd, bd, avg, gn)


def _merge_kernel(x_ref, g_ref, ya_ref, yb_ref, yc_ref, wg_ref, wb_ref, wo_ref, o_ref):
    x = x_ref[...]
    h = _rms(x, g_ref[...]).astype(BF16)
    merged = None
    for n, y_ref in enumerate((ya_ref, yb_ref, yc_ref)):
        gate = jax.nn.sigmoid(_dot(h, wg_ref[:, n * D_MODEL:(n + 1) * D_MODEL]))
        term = gate * _dot(y_ref[...], wb_ref[n])
        merged = term if merged is None else merged + term
    o_ref[...] = x + _dot(merged.astype(BF16), wo_ref[...])


def _merge(x2d, g, ya, yb, yc, wg, wb, wo, l, tm):
    t = x2d.shape[0]
    full = lambda a: _layer_spec(a, l)
    row = lambda w: pl.BlockSpec((tm, w), lambda i: (i, 0))
    return pl.pallas_call(
        _merge_kernel,
        grid=(t // tm,),
        in_specs=[row(D_MODEL), full(g), row(BRANCH_WIDTH), row(BRANCH_WIDTH), row(BRANCH_WIDTH),
                  full(wg), full(wb), full(wo)],
        out_specs=row(D_MODEL),
        out_shape=jax.ShapeDtypeStruct((t, D_MODEL), F32),
        compiler_params=_params(("parallel",)),
        name="merge",
    )(x2d, g, ya, yb, yc, wg, wb, wo)


MLP_CHUNK = 1024


def _mlp_kernel(x_ref, g_ref, wu_ref, wd_ref, gf_ref, o_ref, *, final_norm):
    x = x_ref[...]
    h = _rms(x, g_ref[...]).astype(BF16)
    acc = None
    for c0 in range(0, D_FF, MLP_CHUNK):
        u = jnp.maximum(_dot(h, wu_ref[:, c0:c0 + MLP_CHUNK]), 0.0)
        part = _dot((u * u).astype(BF16), wd_ref[c0:c0 + MLP_CHUNK, :])
        acc = part if acc is None else acc + part
    y = x + acc
    o_ref[...] = _rms(y, gf_ref[...]) if final_norm else y


def _mlp(x2d, g, wu, wd, gf, l, final_norm, tm):
    t, d = x2d.shape
    full = lambda a: _layer_spec(a, l)
    row = pl.BlockSpec((tm, d), lambda i: (i, 0))
    return pl.pallas_call(
        functools.partial(_mlp_kernel, final_norm=final_norm),
        grid=(t // tm,),
        in_specs=[row, full(g), full(wu), full(wd), pl.BlockSpec(gf.shape, lambda i: (0, 0))],
        out_specs=row,
        out_shape=jax.ShapeDtypeStruct((t, d), F32),
        compiler_params=_params(("parallel",)),
        name="mlp",
    )(x2d, g, wu, wd, gf)


def _pick(n, pref):
    t = min(pref, n)
    while n % t:
        t //= 2
    return t


def _rope_tables(positions):
    pos = positions.astype(F32)[..., None]
    b, s = positions.shape
    lane = np.arange(LANES)

    def cs(dim, freq_of_lane):
        inv_freq = ROPE_BASE ** (-jnp.arange(0, dim, 2, dtype=F32) / dim)
        ang = pos * inv_freq[freq_of_lane]
        return jnp.cos(ang), jnp.sin(ang)

    hm = MLA_ROPE // 2
    is_rope = (lane >= MLA_NOPE) & (lane < MLA_NOPE + MLA_ROPE)
    cm, sm = cs(MLA_ROPE, np.where(is_rope, (lane - MLA_NOPE) % hm, 0))
    sign_m = np.where(lane < MLA_NOPE + hm, -1.0, 1.0).astype(np.float32)
    mla_cos = jnp.where(lane < MLA_NOPE, 1.0, jnp.where(is_rope, cm, 0.0)).reshape(b * s, LANES)
    mla_sin = jnp.where(is_rope, sign_m * sm, 0.0).reshape(b * s, LANES)
    hr = RET_DK // 2
    ret_cos, sr = cs(RET_DK, lane % hr)
    ret_sin = np.where(lane % RET_DK < hr, -1.0, 1.0).astype(np.float32) * sr
    return mla_cos, mla_sin, ret_cos, ret_sin


def _prep_weights(w_in, mla_w_uq, mla_w_ukv, w_branch, w_out, w_up, w_down):
    depth, d, _ = w_in.shape
    offs = [int(o) for o in np.cumsum((0, MLA_Q_RANK, MLA_KV_RANK, MLA_ROPE) + (BRANCH_WIDTH,) * 7
                                      + (N_BRANCH * D_MODEL,))]
    col_scale = np.ones(offs[-1], np.float32)
    col_scale[offs[3]:offs[4]] = SB_DIM ** -0.5 * LOG2E
    col_scale[offs[7]:offs[8]] = RET_DK ** -0.5
    w = (w_in * col_scale).astype(BF16)
    seg = lambda n: w[:, :, offs[n]:offs[n + 1]]
    c_q, c_kv, k_pe, sb_q, sb_k, sb_v, r_q, r_k, r_v, r_g, gates = (seg(n) for n in range(11))
    zeros = lambda *shape: jnp.zeros(shape, BF16)
    hr = MLA_ROPE // 2
    pad = LANES - MLA_NOPE - MLA_ROPE
    w_proj = jnp.concatenate(
        [c_q, c_kv,
         zeros(depth, d, MLA_NOPE), k_pe, zeros(depth, d, pad),
         zeros(depth, d, MLA_NOPE), k_pe[..., hr:], k_pe[..., :hr], zeros(depth, d, pad),
         r_q, r_k, r_g,
         sb_q, sb_k, sb_v, r_v], axis=-1)

    uq = mla_w_uq.astype(BF16).reshape(depth, MLA_Q_RANK, MLA_HEADS, MLA_NOPE + MLA_ROPE)
    zq = lambda width: jnp.zeros((depth, MLA_Q_RANK, MLA_HEADS, width), BF16)
    nope, rope = uq[..., :MLA_NOPE], uq[..., MLA_NOPE:]
    wqa = jnp.concatenate([nope, rope, zq(pad)], axis=-1)
    wqb = jnp.concatenate([zq(MLA_NOPE), rope[..., hr:], rope[..., :hr], zq(pad)], axis=-1)
    ukv = mla_w_ukv.astype(BF16).reshape(depth, MLA_KV_RANK, MLA_HEADS, MLA_NOPE + MLA_V)
    wk = jnp.concatenate([ukv[..., :MLA_NOPE],
                          jnp.zeros((depth, MLA_KV_RANK, MLA_HEADS, LANES - MLA_NOPE), BF16)], axis=-1)
    wv = ukv[..., MLA_NOPE:]
    flat = lambda a: a.reshape(depth, a.shape[1], -1)
    bf = lambda a: a.astype(BF16)
    return dict(w_in=w_proj, w_gate=gates,
                wqa=flat(wqa), wqb=flat(wqb), wk_t=jnp.swapaxes(flat(wk), 1, 2), wv=flat(wv),
                wb=bf(w_branch), wo=bf(w_out), wu=bf(w_up), wd=bf(w_down))


def kernel(x, positions, norm_mix_g, w_in, mla_q_norm_g, mla_w_uq, mla_kv_norm_g, mla_w_ukv, ret_norm_g, w_branch, w_out, norm_mlp_g, w_up, w_down, final_norm_g):
    b, s, d = x.shape
    depth = w_in.shape[0]
    t = b * s
    assert d == D_MODEL and s % CHUNK == 0
    tm_row = _pick(t, 512)
    t_mla = _pick(s, 1024)
    t_sb = max(_pick(s, 256), SB_SUB)
    w = _prep_weights(w_in, mla_w_uq, mla_w_ukv, w_branch, w_out, w_up, w_down)
    mla_cos, mla_sin, ret_cos, ret_sin = _rope_tables(positions)
    vec = lambda a: a.reshape(1, -1)
    gains = lambda a: a.reshape(depth, 1, -1)
    g_mix, g_q, g_kv, g_mlp = gains(norm_mix_g), gains(mla_q_norm_g), gains(mla_kv_norm_g), gains(norm_mlp_g)

    x2d = x.reshape(t, d)
    for l in range(depth):
        proj_ret, proj_bf, q_a, k_t, v_a = _in_proj(x2d, g_mix, w["w_in"], mla_cos, mla_sin, g_q, g_kv,
                                                    w["wqa"], w["wqb"], w["wk_t"], w["wv"], l, b, tm_row)
        proj_bf = proj_bf.reshape(b, s, -1)

        y_a = _mla_attn(q_a.reshape(b, s, -1), k_t, v_a.reshape(b, s, -1), t_mla, 4)
        y_b = _sb_attn(proj_bf, t_sb, 8)
        y_c = _retention(proj_ret.reshape(b, s, -1), proj_bf, ret_cos, ret_sin, vec(ret_norm_g[l]),
                         _pick(s // CHUNK, 8))

        x2d = _merge(x2d, g_mix, y_a.reshape(t, -1), y_b.reshape(t, -1), y_c.reshape(t, -1),
                     w["w_gate"], w["wb"], w["wo"], l, tm_row)
        x2d = _mlp(x2d, g_mlp, w["wu"], w["wd"], vec(final_norm_g), l, l == depth - 1, tm_row)
    return x2d.reshape(b, s, d)
```

```python
import functools

import numpy as np
import jax
import jax.numpy as jnp
from jax import lax
from jax.experimental import pallas as pl
from jax.experimental.pallas import tpu as pltpu

F32 = jnp.float32
BF16 = jnp.bfloat16

D_MODEL = 1024
EPS = 1e-6
ROPE_BASE = 10000.0
CHUNK = 128

MLA_HEADS = 8
MLA_NOPE = 64
MLA_ROPE = 32
MLA_V = 64
MLA_Q_RANK = 384
MLA_KV_RANK = 256
SB_HEADS = 8
SB_DIM = 64
RET_HEADS = 8
RET_DK = 64
RET_DV = 64
BRANCH_WIDTH = 512
N_BRANCH = 3
D_FF = 4 * D_MODEL

LANES = 128
LAT_WIDTH = MLA_Q_RANK + MLA_KV_RANK + LANES
VMEM_LIMIT = 56 * 1024 * 1024
LOG2E = 1.4426950408889634
MLA_Q_SCALE = (MLA_NOPE + MLA_ROPE) ** -0.5 * LOG2E
SIGN_BIT = np.uint32(0x80000000)
SB_MASKED_LOGIT = -1e30
SB_SUB = 256
SB_SKIP_LOG2 = -135.0

NT_DIMS = (((1,), (1,)), ((), ()))
TN_DIMS = (((0,), (0,)), ((), ()))


def _dot(a, b):
    return jnp.dot(a, b, preferred_element_type=F32)


def _dot_nt(a, b):
    return lax.dot_general(a, b, NT_DIMS, preferred_element_type=F32)


def _dot_hilo(x, m_bf16):
    hi = x.astype(BF16)
    lo = (x - hi.astype(F32)).astype(BF16)
    return _dot(hi, m_bf16) + _dot(lo, m_bf16)


def _rep_lanes(x, n):
    return x if n == 1 else jnp.concatenate([x] * n, axis=1)


def _rms(x, g):
    return x * lax.rsqrt(jnp.mean(x * x, axis=-1, keepdims=True) + EPS) * g


def _layer_spec(stacked, l):
    nd = stacked.ndim - 1
    return pl.BlockSpec((None,) + stacked.shape[1:], lambda *_: (l,) + (0,) * nd, pipeline_mode=pl.Buffered(1))


def _params(sem):
    return pltpu.CompilerParams(dimension_semantics=sem, vmem_limit_bytes=VMEM_LIMIT)


IN_GROUPS = ((3 * BRANCH_WIDTH, F32),
             (4 * BRANCH_WIDTH, BF16))
IN_CHUNK = 512


def _swap_rope_halves(x):
    width = x.shape[1]
    half = MLA_ROPE // 2
    lane = lax.broadcasted_iota(jnp.int32, x.shape, 1) % LANES
    return jnp.where(lane < MLA_NOPE + half, pltpu.roll(x, width - half, 1), pltpu.roll(x, half, 1))


def _mla_prep(lat, cos_t, sin_t, gq, gkv, wqa, wk_t, wv, q_ref, k_ref, v_ref):
    c_q = lat[:, :MLA_Q_RANK]
    c_kv = lat[:, MLA_Q_RANK:MLA_Q_RANK + MLA_KV_RANK]
    k_pe = lat[:, MLA_Q_RANK + MLA_KV_RANK:]
    nq = _rms(c_q, gq).astype(BF16)
    nkv = _rms(c_kv, gkv).astype(BF16)
    q = _dot(nq, wqa)
    q = q * _rep_lanes(cos_t, MLA_HEADS) + _swap_rope_halves(q) * _rep_lanes(sin_t, MLA_HEADS)
    q_ref[...] = (q * MLA_Q_SCALE).astype(BF16)
    k_rot_t = jnp.transpose(k_pe * cos_t + _swap_rope_halves(k_pe) * sin_t)
    k_t = _dot_nt(wk_t, nkv) + jnp.concatenate([k_rot_t] * MLA_HEADS, axis=0)
    k_ref[0] = k_t.astype(BF16)
    v_ref[...] = _dot(nkv, wv).astype(BF16)


def _in_proj_kernel(x_ref, g_ref, w_ref, cos_ref, sin_ref, gq_ref, gkv_ref, wqa_ref, wk_ref, wv_ref,
                    ret_ref, bf_ref, q_ref, k_ref, v_ref):
    h = _rms(x_ref[...], g_ref[...]).astype(BF16)
    lat = jnp.concatenate([_dot(h, w_ref[:, c0:min(c0 + IN_CHUNK, LAT_WIDTH)])
                           for c0 in range(0, LAT_WIDTH, IN_CHUNK)], axis=1)
    _mla_prep(lat, cos_ref[...], sin_ref[...], gq_ref[...], gkv_ref[...], wqa_ref[...],
              wk_ref[...], wv_ref[...], q_ref, k_ref, v_ref)
    col = LAT_WIDTH
    for o_ref, (width, _) in zip((ret_ref, bf_ref), IN_GROUPS):
        for c0 in range(0, width, IN_CHUNK):
            o_ref[:, c0:c0 + IN_CHUNK] = _dot(h, w_ref[:, col + c0:col + c0 + IN_CHUNK]).astype(o_ref.dtype)
        col += width


def _in_proj(x2d, g, w, cos_t, sin_t, gq, gkv, wqa, wk_t, wv, l, b, tm):
    t, d = x2d.shape
    hw = MLA_HEADS * LANES
    nsb = t // b // tm
    full = lambda a: _layer_spec(a, l)
    row = lambda width: pl.BlockSpec((tm, width), lambda i: (i, 0))
    return pl.pallas_call(
        _in_proj_kernel,
        grid=(t // tm,),
        in_specs=[row(d), full(g), full(w), row(LANES), row(LANES), full(gq), full(gkv),
                  full(wqa), full(wk_t), full(wv)],
        out_specs=[row(IN_GROUPS[0][0]), row(IN_GROUPS[1][0]), row(hw),
                   pl.BlockSpec((1, hw, tm), lambda i: (i // nsb, 0, i % nsb)), row(MLA_HEADS * MLA_V)],
        out_shape=[jax.ShapeDtypeStruct((t, IN_GROUPS[0][0]), IN_GROUPS[0][1]),
                   jax.ShapeDtypeStruct((t, IN_GROUPS[1][0]), IN_GROUPS[1][1]),
                   jax.ShapeDtypeStruct((t, hw), BF16), jax.ShapeDtypeStruct((b, hw, t // b), BF16),
                   jax.ShapeDtypeStruct((t, MLA_HEADS * MLA_V), BF16)],
        compiler_params=_params(("parallel",)),
        name="in_proj",
    )(x2d, g, w, cos_t, sin_t, gq, gkv, wqa, wk_t, wv)


def _mla_attn_kernel(q_ref, kt_ref, v_ref, o_ref, m_ref, l_ref, acc_ref, *, tq, tk, nheads):
    i = pl.program_id(2)
    npairs = nheads // 2
    heads = [slice(e * LANES, (e + 1) * LANES) for e in range(nheads)]

    def tiles(ks, nk, r0, nr, masked):
        rows = slice(r0, r0 + nr)
        scores = [_dot(q_ref[0, rows, heads[e]], kt_ref[0, heads[e], pl.ds(ks, nk)]) for e in range(nheads)]
        if masked:
            q_pos = i * tq + r0 + lax.broadcasted_iota(jnp.int32, (nr, nk), 0)
            k_pos = ks + lax.broadcasted_iota(jnp.int32, (nr, nk), 1)
            scores = [jnp.where(k_pos <= q_pos, s, -jnp.inf) for s in scores]
        probs, alphas = [], []
        for e, s in enumerate(scores):
            m_prev = m_ref[e, rows]
            m_next = jnp.maximum(m_prev, jnp.max(s, axis=1, keepdims=True))
            alpha = jnp.exp2(m_prev - m_next)
            p = jnp.exp2(s - _rep_lanes(m_next, nk // LANES))
            partial = functools.reduce(jnp.add, [p[:, c:c + LANES] for c in range(0, nk, LANES)])
            l_ref[e, rows] = alpha * l_ref[e, rows] + partial
            m_ref[e, rows] = m_next
            probs.append(p.astype(BF16))
            alphas.append(alpha)
        for p in range(npairs):
            pv = _dot(jnp.concatenate([probs[2 * p], probs[2 * p + 1]], axis=0), v_ref[0, pl.ds(ks, nk), heads[p]])
            for e in range(2):
                arows = slice(e * tq + r0, e * tq + r0 + nr)
                acc_ref[p, arows] = alphas[2 * p + e] * acc_ref[p, arows] + pv[e * nr:(e + 1) * nr]

    m_ref[...] = jnp.full(m_ref.shape, -jnp.inf, F32)
    l_ref[...] = jnp.zeros(l_ref.shape, F32)
    acc_ref[...] = jnp.zeros(acc_ref.shape, F32)

    def body(j, carry):
        tiles(pl.multiple_of(j * tq, tq), tq, 0, tq, False)
        return carry

    lax.fori_loop(0, i, body, 0)
    diag = pl.multiple_of(i * tq, tq)
    tiles(diag, tk, 0, tq, True)
    tiles(diag + tk, tk, tk, tq - tk, True)
    lane = lax.broadcasted_iota(jnp.int32, (tq, LANES), 1)
    denom = [jnp.sum(l_ref[e], axis=1, keepdims=True) for e in range(nheads)]
    for p in range(npairs):
        o_ref[0, :, heads[p]] = jnp.where(
            lane < MLA_V, acc_ref[p, :tq] / denom[2 * p], acc_ref[p, tq:] / denom[2 * p + 1]).astype(o_ref.dtype)


def _mla_attn(q, k_t, v, tq, nheads):
    b, s, _ = q.shape
    tk = tq // 2
    return pl.pallas_call(
        functools.partial(_mla_attn_kernel, tq=tq, tk=tk, nheads=nheads),
        grid=(b, MLA_HEADS // nheads, s // tq),
        in_specs=[pl.BlockSpec((1, tq, nheads * LANES), lambda bi, p, i: (bi, i, p)),
                  pl.BlockSpec((1, nheads * LANES, s), lambda bi, p, i: (bi, p, 0)),
                  pl.BlockSpec((1, s, nheads * MLA_V), lambda bi, p, i: (bi, 0, p))],
        out_specs=pl.BlockSpec((1, tq, nheads * MLA_V), lambda bi, p, i: (bi, i, p)),
        out_shape=jax.ShapeDtypeStruct((b, s, MLA_HEADS * MLA_V), BF16),
        scratch_shapes=[pltpu.VMEM((nheads, tq, LANES), F32), pltpu.VMEM((nheads, tq, LANES), F32),
                        pltpu.VMEM((nheads // 2, 2 * tq, LANES), F32)],
        compiler_params=_params(("parallel", "parallel", "arbitrary")),
        name="mla_attn",
    )(q, k_t, v)


def _sb_attn_kernel(q_ref, k_ref, v_ref, u_ref, o_ref, qm_ref, c_ref, acc_ref, *, tq, nheads):
    i = pl.program_id(2)
    npairs = nheads // 2
    lane = lax.broadcasted_iota(jnp.int32, (tq, LANES), 1)
    zero = jnp.zeros((tq, LANES), BF16)
    for p in range(npairs):
        q_pair = q_ref[0, :, p * LANES:(p + 1) * LANES]
        qm_ref[p, :tq] = jnp.where(lane < SB_DIM, q_pair, zero)
        qm_ref[p, tq:] = jnp.where(lane < SB_DIM, zero, q_pair)

    pairs = [slice(p * LANES, (p + 1) * LANES) for p in range(npairs)]
    has_left = i > 0
    row = lax.broadcasted_iota(jnp.int32, (2 * tq, tq), 0)
    col = lax.broadcasted_iota(jnp.int32, (2 * tq, tq), 1)
    before = col < jnp.where(row < tq, row, row - tq)

    def run(jobs):
        starts = [pl.multiple_of(j * tq, tq) for _, j, _ in jobs]
        zs = [_dot_nt(qm_ref[p], k_ref[0, pl.ds(ks, tq), pairs[p]]) for (p, _, _), ks in zip(jobs, starts)]
        stats = []
        for z, (_, _, kind) in zip(zs, jobs):
            if kind == "diag":
                z = jnp.where(before, z, SB_MASKED_LOGIT)
            neg_abs = lax.bitcast_convert_type(lax.bitcast_convert_type(z, jnp.uint32) | SIGN_BIT, F32)
            log_beta = jnp.minimum(z, 0.0) - jnp.log(1.0 + jnp.exp2(neg_abs)) * LOG2E
            log_not = log_beta - z
            hi = log_not.astype(BF16)
            lo = (log_not - hi.astype(F32)).astype(BF16)
            stats.append((log_beta, log_not, jnp.concatenate([hi, lo], axis=1)))
        withins = [_dot(hilo, u_ref[...]) for _, _, hilo in stats]
        weights = []
        for (p, _, kind), (log_beta, log_not, _), within in zip(jobs, stats, withins):
            c = c_ref[p]
            if kind == "left":
                c = jnp.where(has_left, c, -jnp.inf)
            a = jnp.exp2(log_beta + within + _rep_lanes(c, tq // LANES))
            weights.append(a.astype(BF16))
            c_ref[p] = c + jnp.sum(log_not, axis=1, keepdims=True)
        for (p, _, _), ks, a in zip(jobs, starts, weights):
            acc_ref[p] += _dot(a, v_ref[0, pl.ds(ks, tq), pairs[p]])

    def live():
        return (jnp.max(c_ref[...]) > SB_SKIP_LOG2).astype(jnp.int32)

    c_ref[...] = jnp.zeros(c_ref.shape, F32)
    acc_ref[...] = jnp.zeros(acc_ref.shape, F32)
    left = jnp.maximum(i - 1, 0)
    run([(p, i, "diag") for p in range(npairs)] + [(p, left, "left") for p in range(npairs)])

    def cond(carry):
        jj, go = carry
        return jnp.logical_and(jj < i, go > 0)

    def body(carry):
        jj, _ = carry
        run([(p, i - 1 - jj, "plain") for p in range(npairs)])
        return jj + 1, live()

    lax.while_loop(cond, body, (jnp.int32(1), live()))
    for p in range(npairs):
        o_ref[0, :, p * LANES:(p + 1) * LANES] = jnp.where(
            lane < SB_DIM, acc_ref[p, :tq], acc_ref[p, tq:]).astype(o_ref.dtype)


def _sb_attn(proj_bf, tq, nheads):
    b, s, _ = proj_bf.shape
    assert tq == SB_SUB
    ngroup = SB_HEADS // nheads
    width = nheads * SB_DIM
    idx = np.arange(SB_SUB)
    tri = (idx[:, None] > idx[None, :]).astype(np.float32)
    u = jnp.asarray(np.concatenate([tri, tri], axis=0), dtype=BF16)
    return pl.pallas_call(
        functools.partial(_sb_attn_kernel, tq=tq, nheads=nheads),
        grid=(b, ngroup, s // tq),
        in_specs=[pl.BlockSpec((1, tq, width), lambda bi, p, i: (bi, i, p)),
                  pl.BlockSpec((1, s, width), lambda bi, p, i: (bi, 0, ngroup + p)),
                  pl.BlockSpec((1, s, width), lambda bi, p, i: (bi, 0, 2 * ngroup + p)),
                  pl.BlockSpec(u.shape, lambda bi, p, i: (0, 0))],
        out_specs=pl.BlockSpec((1, tq, width), lambda bi, p, i: (bi, i, p)),
        out_shape=jax.ShapeDtypeStruct((b, s, SB_HEADS * SB_DIM), BF16),
        scratch_shapes=[pltpu.VMEM((nheads // 2, 2 * tq, LANES), BF16),
                        pltpu.VMEM((nheads // 2, 2 * tq, LANES), F32),
                        pltpu.VMEM((nheads // 2, 2 * tq, LANES), F32)],
        compiler_params=_params(("parallel", "parallel", "arbitrary")),
        name="sb_attn",
    )(proj_bf, proj_bf, proj_bf, u)


def _retention_kernel(q_ref, k_ref, g_ref, v_ref, cos_ref, sin_ref, dec_ref, zeta_ref, xi_ref,
                      cd_ref, bd_ref, avg_ref, gn_ref, o_ref, state_ref, *, nchunk):
    @pl.when(pl.program_id(1) == 0)
    def _():
        state_ref[...] = jnp.zeros(state_ref.shape, F32)

    rows = nchunk * CHUNK
    width = RET_HEADS * RET_DK
    npair = RET_HEADS // 2
    half = RET_DK // 2
    lane_w = lax.broadcasted_iota(jnp.int32, (rows, width), 1)
    first_half = (lane_w % RET_DK) < half
    head0 = (lane_w % LANES) < RET_DK
    cos_t = _rep_lanes(cos_ref[0], width // LANES)
    sin_t = _rep_lanes(sin_ref[0], width // LANES)

    def rope(x):
        swapped = jnp.where(first_half, pltpu.roll(x, width - half, 1), pltpu.roll(x, half, 1))
        return x * cos_t + swapped * sin_t

    q = rope(q_ref[0])
    k = rope(k_ref[0])
    kz = (k * zeta_ref[...]).astype(BF16)
    qb = q.astype(BF16)
    kb = k.astype(BF16)
    zero = jnp.zeros_like(qb)
    q0 = jnp.where(head0, qb, zero)
    q1 = jnp.where(head0, zero, qb)
    v = v_ref[0]
    units = [(c, p) for c in range(nchunk) for p in range(npair)]
    rs = lambda c: slice(c * CHUNK, (c + 1) * CHUNK)
    ls = lambda p: slice(p * LANES, (p + 1) * LANES)
    scores = [_dot_nt(jnp.concatenate([q0[rs(c), ls(p)], q1[rs(c), ls(p)]], axis=0), kb[rs(c), ls(p)])
              for c, p in units]
    scores = [(s * dec_ref[p]).astype(BF16) for s, (c, p) in zip(scores, units)]
    intra = [_dot(s, v[rs(c), ls(p)]) for s, (c, p) in zip(scores, units)]
    kvs = [lax.dot_general(kz[rs(c), ls(p)], v[rs(c), ls(p)], TN_DIMS, preferred_element_type=F32)
           for c, p in units]
    lane = lax.broadcasted_iota(jnp.int32, (CHUNK, LANES), 1)
    chunks = []
    for c in range(nchunk):
        ys = []
        for p in range(npair):
            u = c * npair + p
            state = state_ref[p]
            y_cross = _dot(qb[rs(c), ls(p)], state.astype(BF16)) * xi_ref[:, ls(p)]
            state_ref[p] = state * cd_ref[p] + kvs[u] * bd_ref[...]
            y_intra = jnp.where(lane < RET_DV, intra[u][:CHUNK], intra[u][CHUNK:])
            ys.append(y_intra + y_cross)
        chunks.append(jnp.concatenate(ys, axis=1))
    y = jnp.concatenate(chunks, axis=0)
    avg = avg_ref[...]
    d = y - _dot_hilo(y, avg)
    var = _dot_hilo(d * d, avg)
    yn = d * lax.rsqrt(var + EPS) * gn_ref[...]
    g = g_ref[0]
    o_ref[0] = (g * jax.nn.sigmoid(g) * yn).astype(o_ref.dtype)


def _retention_consts(nchunk):
    h = np.arange(RET_HEADS, dtype=np.float64)
    log_gamma = np.log1p(-np.exp2(-5.0 - h))
    idx = np.arange(CHUNK, dtype=np.float64)
    diff = idx[:, None] - idx[None, :]
    dec = np.where(diff[None] >= 0, np.exp(np.maximum(diff, 0.0)[None] * log_gamma[:, None, None]), 0.0)
    npair = RET_HEADS // 2
    dec = dec.reshape(npair, 2 * CHUNK, CHUNK)
    zeta = np.exp((CHUNK - 1 - idx)[:, None] * log_gamma[None, :])
    xi = np.exp((idx + 1.0)[:, None] * log_gamma[None, :])
    zeta_l = np.tile(np.repeat(zeta, RET_DK, axis=1), (nchunk, 1))
    xi_l = np.repeat(xi, RET_DV, axis=1)
    chunk_decay = np.exp(CHUNK * log_gamma)
    head_of = np.arange(LANES) // RET_DK
    bd = (head_of[:, None] == head_of[None, :]).astype(np.float64)
    cd = np.stack([bd * chunk_decay[2 * p + head_of][:, None] for p in range(npair)])
    gh = np.arange(RET_HEADS * RET_DV) // RET_DV
    avg = (gh[:, None] == gh[None, :]).astype(np.float64) / RET_DV
    f = lambda a: jnp.asarray(a, dtype=F32)
    return f(dec), f(zeta_l), f(xi_l), f(cd), f(bd), jnp.asarray(avg, dtype=BF16)


def _retention(proj_ret, proj_bf, cos_t, sin_t, gn, nchunk):
    b, s, _ = proj_ret.shape
    width = RET_HEADS * RET_DK
    rows = nchunk * CHUNK
    dec, zeta_l, xi_l, cd, bd, avg = _retention_consts(nchunk)
    full = lambda a: pl.BlockSpec(a.shape, lambda bi, n: (0,) * a.ndim)
    blk = lambda c: pl.BlockSpec((1, rows, width), lambda bi, n: (bi, n, c))
    tab = pl.BlockSpec((1, rows, LANES), lambda bi, n: (bi, n, 0))
    return pl.pallas_call(
        functools.partial(_retention_kernel, nchunk=nchunk),
        grid=(b, s // rows),
        in_specs=[blk(0), blk(1), blk(2), blk(3), tab, tab,
                  full(dec), full(zeta_l), full(xi_l), full(cd), full(bd), full(avg), full(gn)],
        out_specs=blk(0),
        out_shape=jax.ShapeDtypeStruct((b, s, RET_HEADS * RET_DV), BF16),
        scratch_shapes=[pltpu.VMEM((RET_HEADS // 2, LANES, LANES), F32)],
        compiler_params=_params(("parallel", "arbitrary")),
        name="retention",
    )(proj_ret, proj_ret, proj_ret, proj_bf, cos_t, sin_t, dec, zeta_l, xi_l, cd, bd, avg, gn)


def _merge_kernel(x_ref, g_ref, ya_ref, yb_ref, yc_ref, wg_ref, wb_ref, wo_ref, o_ref):
    x = x_ref[...]
    h = _rms(x, g_ref[...]).astype(BF16)
    merged = None
    for n, y_ref in enumerate((ya_ref, yb_ref, yc_ref)):
        gate = jax.nn.sigmoid(_dot(h, wg_ref[:, n * D_MODEL:(n + 1) * D_MODEL]))
        term = gate * _dot(y_ref[...], wb_ref[n])
        merged = term if merged is None else merged + term
    o_ref[...] = x + _dot(merged.astype(BF16), wo_ref[...])


def _merge(x2d, g, ya, yb, yc, wg, wb, wo, l, tm):
    t = x2d.shape[0]
    full = lambda a: _layer_spec(a, l)
    row = lambda w: pl.BlockSpec((tm, w), lambda i: (i, 0))
    return pl.pallas_call(
        _merge_kernel,
        grid=(t // tm,),
        in_specs=[row(D_MODEL), full(g), row(BRANCH_WIDTH), row(BRANCH_WIDTH), row(BRANCH_WIDTH),
                  full(wg), full(wb), full(wo)],
        out_specs=row(D_MODEL),
        out_shape=jax.ShapeDtypeStruct((t, D_MODEL), F32),
        compiler_params=_params(("parallel",)),
        name="merge",
    )(x2d, g, ya, yb, yc, wg, wb, wo)


MLP_CHUNK = 1024


def _mlp_kernel(x_ref, g_ref, wu_ref, wd_ref, gf_ref, o_ref, *, final_norm):
    x = x_ref[...]
    h = _rms(x, g_ref[...]).astype(BF16)
    acc = None
    for c0 in range(0, D_FF, MLP_CHUNK):
        u = jnp.maximum(_dot(h, wu_ref[:, c0:c0 + MLP_CHUNK]), 0.0)
        part = _dot((u * u).astype(BF16), wd_ref[c0:c0 + MLP_CHUNK, :])
        acc = part if acc is None else acc + part
    y = x + acc
    o_ref[...] = _rms(y, gf_ref[...]) if final_norm else y


def _mlp(x2d, g, wu, wd, gf, l, final_norm, tm):
    t, d = x2d.shape
    full = lambda a: _layer_spec(a, l)
    row = pl.BlockSpec((tm, d), lambda i: (i, 0))
    return pl.pallas_call(
        functools.partial(_mlp_kernel, final_norm=final_norm),
        grid=(t // tm,),
        in_specs=[row, full(g), full(wu), full(wd), pl.BlockSpec(gf.shape, lambda i: (0, 0))],
        out_specs=row,
        out_shape=jax.ShapeDtypeStruct((t, d), F32),
        compiler_params=_params(("parallel",)),
        name="mlp",
    )(x2d, g, wu, wd, gf)


def _pick(n, pref):
    t = min(pref, n)
    while n % t:
        t //= 2
    return t


def _rope_tables(positions):
    pos = positions.astype(F32)[..., None]
    b, s = positions.shape
    lane = np.arange(LANES)

    def cs(dim, freq_of_lane):
        inv_freq = ROPE_BASE ** (-jnp.arange(0, dim, 2, dtype=F32) / dim)
        ang = pos * inv_freq[freq_of_lane]
        return jnp.cos(ang), jnp.sin(ang)

    hm = MLA_ROPE // 2
    is_rope = (lane >= MLA_NOPE) & (lane < MLA_NOPE + MLA_ROPE)
    cm, sm = cs(MLA_ROPE, np.where(is_rope, (lane - MLA_NOPE) % hm, 0))
    sign_m = np.where(lane < MLA_NOPE + hm, -1.0, 1.0).astype(np.float32)
    mla_cos = jnp.where(lane < MLA_NOPE, 1.0, jnp.where(is_rope, cm, 0.0)).reshape(b * s, LANES)
    mla_sin = jnp.where(is_rope, sign_m * sm, 0.0).reshape(b * s, LANES)
    hr = RET_DK // 2
    ret_cos, sr = cs(RET_DK, lane % hr)
    ret_sin = np.where(lane % RET_DK < hr, -1.0, 1.0).astype(np.float32) * sr
    return mla_cos, mla_sin, ret_cos, ret_sin


def _prep_weights(w_in, mla_w_uq, mla_w_ukv, w_branch, w_out, w_up, w_down):
    depth, d, _ = w_in.shape
    offs = [int(o) for o in np.cumsum((0, MLA_Q_RANK, MLA_KV_RANK, MLA_ROPE) + (BRANCH_WIDTH,) * 7
                                      + (N_BRANCH * D_MODEL,))]
    col_scale = np.ones(offs[-1], np.float32)
    col_scale[offs[3]:offs[4]] = SB_DIM ** -0.5 * LOG2E
    col_scale[offs[7]:offs[8]] = RET_DK ** -0.5
    w = (w_in * col_scale).astype(BF16)
    seg = lambda n: w[:, :, offs[n]:offs[n + 1]]
    c_q, c_kv, k_pe, sb_q, sb_k, sb_v, r_q, r_k, r_v, r_g, gates = (seg(n) for n in range(11))
    zeros = lambda *shape: jnp.zeros(shape, BF16)
    pad = LANES - MLA_NOPE - MLA_ROPE
    w_proj = jnp.concatenate(
        [c_q, c_kv,
         zeros(depth, d, MLA_NOPE), k_pe, zeros(depth, d, pad),
         r_q, r_k, r_g,
         sb_q, sb_k, sb_v, r_v], axis=-1)

    uq = mla_w_uq.astype(BF16).reshape(depth, MLA_Q_RANK, MLA_HEADS, MLA_NOPE + MLA_ROPE)
    zq = lambda width: jnp.zeros((depth, MLA_Q_RANK, MLA_HEADS, width), BF16)
    nope, rope = uq[..., :MLA_NOPE], uq[..., MLA_NOPE:]
    wqa = jnp.concatenate([nope, rope, zq(pad)], axis=-1)
    ukv = mla_w_ukv.astype(BF16).reshape(depth, MLA_KV_RANK, MLA_HEADS, MLA_NOPE + MLA_V)
    wk = jnp.concatenate([ukv[..., :MLA_NOPE],
                          jnp.zeros((depth, MLA_KV_RANK, MLA_HEADS, LANES - MLA_NOPE), BF16)], axis=-1)
    wv = ukv[..., MLA_NOPE:]
    flat = lambda a: a.reshape(depth, a.shape[1], -1)
    bf = lambda a: a.astype(BF16)
    return dict(w_in=w_proj, w_gate=gates,
                wqa=flat(wqa), wk_t=jnp.swapaxes(flat(wk), 1, 2), wv=flat(wv),
                wb=bf(w_branch), wo=bf(w_out), wu=bf(w_up), wd=bf(w_down))


def kernel(x, positions, norm_mix_g, w_in, mla_q_norm_g, mla_w_uq, mla_kv_norm_g, mla_w_ukv, ret_norm_g, w_branch, w_out, norm_mlp_g, w_up, w_down, final_norm_g):
    b, s, d = x.shape
    depth = w_in.shape[0]
    t = b * s
    assert d == D_MODEL and s % CHUNK == 0
    tm_row = _pick(t, 512)
    t_mla = _pick(s, 1024)
    t_sb = max(_pick(s, 256), SB_SUB)
    w = _prep_weights(w_in, mla_w_uq, mla_w_ukv, w_branch, w_out, w_up, w_down)
    mla_cos, mla_sin, ret_cos, ret_sin = _rope_tables(positions)
    vec = lambda a: a.reshape(1, -1)
    gains = lambda a: a.reshape(depth, 1, -1)
    g_mix, g_q, g_kv, g_mlp = gains(norm_mix_g), gains(mla_q_norm_g), gains(mla_kv_norm_g), gains(norm_mlp_g)

    x2d = x.reshape(t, d)
    for l in range(depth):
        proj_ret, proj_bf, q_a, k_t, v_a = _in_proj(x2d, g_mix, w["w_in"], mla_cos, mla_sin, g_q, g_kv,
                                                    w["wqa"], w["wk_t"], w["wv"], l, b, tm_row)
        proj_bf = proj_bf.reshape(b, s, -1)

        y_a = _mla_attn(q_a.reshape(b, s, -1), k_t, v_a.reshape(b, s, -1), t_mla, 4)
        y_b = _sb_attn(proj_bf, t_sb, 8)
        y_c = _retention(proj_ret.reshape(b, s, -1), proj_bf, ret_cos, ret_sin, vec(ret_norm_g[l]),
                         _pick(s // CHUNK, 8))

        x2d = _merge(x2d, g_mix, y_a.reshape(t, -1), y_b.reshape(t, -1), y_c.reshape(t, -1),
                     w["w_gate"], w["wb"], w["wo"], l, tm_row)
        x2d = _mlp(x2d, g_mlp, w["wu"], w["wd"], vec(final_norm_g), l, l == depth - 1, tm_row)
    return x2d.reshape(b, s, d)
```

```python
import functools

import numpy as np
import jax
import jax.numpy as jnp
from jax import lax
from jax.experimental import pallas as pl
from jax.experimental.pallas import tpu as pltpu

F32 = jnp.float32
BF16 = jnp.bfloat16

D_MODEL = 1024
EPS = 1e-6
ROPE_BASE = 10000.0
CHUNK = 128

MLA_HEADS = 8
MLA_NOPE = 64
MLA_ROPE = 32
MLA_V = 64
MLA_Q_RANK = 384
MLA_KV_RANK = 256
SB_HEADS = 8
SB_DIM = 64
RET_HEADS = 8
RET_DK = 64
RET_DV = 64
BRANCH_WIDTH = 512
N_BRANCH = 3
D_FF = 4 * D_MODEL

LANES = 128
MXU_TILE = 256
LAT_WIDTH = MLA_Q_RANK + MLA_KV_RANK + LANES
VMEM_LIMIT = 56 * 1024 * 1024
LOG2E = 1.4426950408889634
MLA_Q_SCALE = (MLA_NOPE + MLA_ROPE) ** -0.5 * LOG2E
SIGN_BIT = np.uint32(0x80000000)
SB_MASKED_LOGIT = -1e30
SB_SUB = MXU_TILE
SB_SKIP_LOG2 = -135.0

NT_DIMS = (((1,), (1,)), ((), ()))
TN_DIMS = (((0,), (0,)), ((), ()))


def _dot(a, b):
    return jnp.dot(a, b, preferred_element_type=F32)


def _dot_nt(a, b):
    return lax.dot_general(a, b, NT_DIMS, preferred_element_type=F32)


def _dot_hilo(x, m_bf16):
    hi = x.astype(BF16)
    lo = (x - hi.astype(F32)).astype(BF16)
    return _dot(hi, m_bf16) + _dot(lo, m_bf16)


def _rep_lanes(x, n):
    return x if n == 1 else jnp.concatenate([x] * n, axis=1)


def _rms(x, g):
    return x * lax.rsqrt(jnp.mean(x * x, axis=-1, keepdims=True) + EPS) * g


def _layer_spec(stacked, l):
    nd = stacked.ndim - 1
    return pl.BlockSpec((None,) + stacked.shape[1:], lambda *_: (l,) + (0,) * nd, pipeline_mode=pl.Buffered(1))


def _params(sem):
    return pltpu.CompilerParams(dimension_semantics=sem, vmem_limit_bytes=VMEM_LIMIT)


IN_GROUPS = ((3 * BRANCH_WIDTH, F32),
             (4 * BRANCH_WIDTH, BF16))
IN_CHUNK = 512


def _swap_rope_halves(x):
    width = x.shape[1]
    half = MLA_ROPE // 2
    lane = lax.broadcasted_iota(jnp.int32, x.shape, 1) % LANES
    return jnp.where(lane < MLA_NOPE + half, pltpu.roll(x, width - half, 1), pltpu.roll(x, half, 1))


def _mla_prep(lat, cos_t, sin_t, gq, gkv, wqa, wk_t, wv, q_ref, k_ref, v_ref):
    c_q = lat[:, :MLA_Q_RANK]
    c_kv = lat[:, MLA_Q_RANK:MLA_Q_RANK + MLA_KV_RANK]
    k_pe = lat[:, MLA_Q_RANK + MLA_KV_RANK:]
    nq = _rms(c_q, gq).astype(BF16)
    nkv = _rms(c_kv, gkv).astype(BF16)
    q = _dot(nq, wqa)
    q = q * _rep_lanes(cos_t, MLA_HEADS) + _swap_rope_halves(q) * _rep_lanes(sin_t, MLA_HEADS)
    q_ref[...] = (q * MLA_Q_SCALE).astype(BF16)
    k_rot_t = jnp.transpose(k_pe * cos_t + _swap_rope_halves(k_pe) * sin_t)
    k_nope_t = _dot_nt(wk_t, nkv)
    k_rot_rows = k_rot_t[MLA_NOPE:]
    k_ref[0] = jnp.concatenate(
        [part for h in range(MLA_HEADS) for part in (k_nope_t[h * MLA_NOPE:(h + 1) * MLA_NOPE], k_rot_rows)],
        axis=0).astype(BF16)
    v_ref[...] = _dot(nkv, wv).astype(BF16)


def _in_proj_kernel(x_ref, g_ref, w_ref, cos_ref, sin_ref, gq_ref, gkv_ref, wqa_ref, wk_ref, wv_ref,
                    ret_ref, bf_ref, q_ref, k_ref, v_ref):
    h = _rms(x_ref[...], g_ref[...]).astype(BF16)
    lat = jnp.concatenate([_dot(h, w_ref[:, c0:min(c0 + IN_CHUNK, LAT_WIDTH)])
                           for c0 in range(0, LAT_WIDTH, IN_CHUNK)], axis=1)
    _mla_prep(lat, cos_ref[...], sin_ref[...], gq_ref[...], gkv_ref[...], wqa_ref[...],
              wk_ref[...], wv_ref[...], q_ref, k_ref, v_ref)
    col = LAT_WIDTH
    for o_ref, (width, _) in zip((ret_ref, bf_ref), IN_GROUPS):
        for c0 in range(0, width, IN_CHUNK):
            o_ref[:, c0:c0 + IN_CHUNK] = _dot(h, w_ref[:, col + c0:col + c0 + IN_CHUNK]).astype(o_ref.dtype)
        col += width


def _in_proj(x2d, g, w, cos_t, sin_t, gq, gkv, wqa, wk_t, wv, l, b, tm):
    t, d = x2d.shape
    hw = MLA_HEADS * LANES
    nsb = t // b // tm
    full = lambda a: _layer_spec(a, l)
    row = lambda width: pl.BlockSpec((tm, width), lambda i: (i, 0))
    return pl.pallas_call(
        _in_proj_kernel,
        grid=(t // tm,),
        in_specs=[row(d), full(g), full(w), row(LANES), row(LANES), full(gq), full(gkv),
                  full(wqa), full(wk_t), full(wv)],
        out_specs=[row(IN_GROUPS[0][0]), row(IN_GROUPS[1][0]), row(hw),
                   pl.BlockSpec((1, hw, tm), lambda i: (i // nsb, 0, i % nsb)), row(MLA_HEADS * MLA_V)],
        out_shape=[jax.ShapeDtypeStruct((t, IN_GROUPS[0][0]), IN_GROUPS[0][1]),
                   jax.ShapeDtypeStruct((t, IN_GROUPS[1][0]), IN_GROUPS[1][1]),
                   jax.ShapeDtypeStruct((t, hw), BF16), jax.ShapeDtypeStruct((b, hw, t // b), BF16),
                   jax.ShapeDtypeStruct((t, MLA_HEADS * MLA_V), BF16)],
        compiler_params=_params(("parallel",)),
        name="in_proj",
    )(x2d, g, w, cos_t, sin_t, gq, gkv, wqa, wk_t, wv)


def _mla_attn_kernel(q_ref, kt_ref, v_ref, o_ref, m_ref, l_ref, acc_ref, *, tq, tk, nheads):
    i = pl.program_id(2)
    npairs = nheads // 2
    heads = [slice(e * LANES, (e + 1) * LANES) for e in range(nheads)]

    def tiles(ks, nk, r0, nr, masked):
        rows = slice(r0, r0 + nr)
        scores = [_dot(q_ref[0, rows, heads[e]], kt_ref[0, heads[e], pl.ds(ks, nk)]) for e in range(nheads)]
        if masked:
            q_pos = i * tq + r0 + lax.broadcasted_iota(jnp.int32, (nr, nk), 0)
            k_pos = ks + lax.broadcasted_iota(jnp.int32, (nr, nk), 1)
            scores = [jnp.where(k_pos <= q_pos, s, -jnp.inf) for s in scores]
        probs, alphas = [], []
        for e, s in enumerate(scores):
            m_prev = m_ref[e, rows]
            m_next = jnp.maximum(m_prev, jnp.max(s, axis=1, keepdims=True))
            alpha = jnp.exp2(m_prev - m_next)
            p = jnp.exp2(s - _rep_lanes(m_next, nk // LANES))
            partial = functools.reduce(jnp.add, [p[:, c:c + LANES] for c in range(0, nk, LANES)])
            l_ref[e, rows] = alpha * l_ref[e, rows] + partial
            m_ref[e, rows] = m_next
            probs.append(p.astype(BF16))
            alphas.append(alpha)
        for p in range(npairs):
            pv = _dot(jnp.concatenate([probs[2 * p], probs[2 * p + 1]], axis=0), v_ref[0, pl.ds(ks, nk), heads[p]])
            for e in range(2):
                arows = slice(e * tq + r0, e * tq + r0 + nr)
                acc_ref[p, arows] = alphas[2 * p + e] * acc_ref[p, arows] + pv[e * nr:(e + 1) * nr]

    m_ref[...] = jnp.full(m_ref.shape, -jnp.inf, F32)
    l_ref[...] = jnp.zeros(l_ref.shape, F32)
    acc_ref[...] = jnp.zeros(acc_ref.shape, F32)

    def body(j, carry):
        tiles(pl.multiple_of(j * tq, tq), tq, 0, tq, False)
        return carry

    lax.fori_loop(0, i, body, 0)
    diag = pl.multiple_of(i * tq, tq)
    tiles(diag, tk, 0, tq, True)
    tiles(diag + tk, tk, tk, tq - tk, True)
    lane = lax.broadcasted_iota(jnp.int32, (tq, LANES), 1)
    denom = [jnp.sum(l_ref[e], axis=1, keepdims=True) for e in range(nheads)]
    for p in range(npairs):
        o_ref[0, :, heads[p]] = jnp.where(
            lane < MLA_V, acc_ref[p, :tq] / denom[2 * p], acc_ref[p, tq:] / denom[2 * p + 1]).astype(o_ref.dtype)


def _mla_attn(q, k_t, v, tq, nheads):
    b, s, _ = q.shape
    tk = tq // 2
    return pl.pallas_call(
        functools.partial(_mla_attn_kernel, tq=tq, tk=tk, nheads=nheads),
        grid=(b, MLA_HEADS // nheads, s // tq),
        in_specs=[pl.BlockSpec((1, tq, nheads * LANES), lambda bi, p, i: (bi, i, p)),
                  pl.BlockSpec((1, nheads * LANES, s), lambda bi, p, i: (bi, p, 0)),
                  pl.BlockSpec((1, s, nheads * MLA_V), lambda bi, p, i: (bi, 0, p))],
        out_specs=pl.BlockSpec((1, tq, nheads * MLA_V), lambda bi, p, i: (bi, i, p)),
        out_shape=jax.ShapeDtypeStruct((b, s, MLA_HEADS * MLA_V), BF16),
        scratch_shapes=[pltpu.VMEM((nheads, tq, LANES), F32), pltpu.VMEM((nheads, tq, LANES), F32),
                        pltpu.VMEM((nheads // 2, 2 * tq, LANES), F32)],
        compiler_params=_params(("parallel", "parallel", "arbitrary")),
        name="mla_attn",
    )(q, k_t, v)


def _sb_attn_kernel(q_ref, k_ref, v_ref, u_ref, o_ref, qm_ref, c_ref, acc_ref, *, tq, nheads):
    i = pl.program_id(2)
    npairs = nheads // 2
    lane = lax.broadcasted_iota(jnp.int32, (tq, LANES), 1)
    zero = jnp.zeros((tq, LANES), BF16)
    for p in range(npairs):
        q_pair = q_ref[0, :, p * LANES:(p + 1) * LANES]
        qm_ref[p, :tq] = jnp.where(lane < SB_DIM, q_pair, zero)
        qm_ref[p, tq:] = jnp.where(lane < SB_DIM, zero, q_pair)

    pairs = [slice(p * LANES, (p + 1) * LANES) for p in range(npairs)]
    has_left = i > 0
    row = lax.broadcasted_iota(jnp.int32, (2 * tq, tq), 0)
    col = lax.broadcasted_iota(jnp.int32, (2 * tq, tq), 1)
    before = col < jnp.where(row < tq, row, row - tq)

    def run(jobs):
        starts = [pl.multiple_of(j * tq, tq) for _, j, _ in jobs]
        zs = [_dot_nt(qm_ref[p], k_ref[0, pl.ds(ks, tq), pairs[p]]) for (p, _, _), ks in zip(jobs, starts)]
        stats = []
        for z, (_, _, kind) in zip(zs, jobs):
            if kind == "diag":
                z = jnp.where(before, z, SB_MASKED_LOGIT)
            neg_abs = lax.bitcast_convert_type(lax.bitcast_convert_type(z, jnp.uint32) | SIGN_BIT, F32)
            log_beta = jnp.minimum(z, 0.0) - jnp.log(1.0 + jnp.exp2(neg_abs)) * LOG2E
            log_not = log_beta - z
            hi = log_not.astype(BF16)
            lo = (log_not - hi.astype(F32)).astype(BF16)
            stats.append((log_beta, log_not, jnp.concatenate([hi, lo], axis=1)))
        withins = [_dot(hilo, u_ref[...]) for _, _, hilo in stats]
        weights = []
        for (p, _, kind), (log_beta, log_not, _), within in zip(jobs, stats, withins):
            c = c_ref[p]
            if kind == "left":
                c = jnp.where(has_left, c, -jnp.inf)
            a = jnp.exp2(log_beta + within + _rep_lanes(c, tq // LANES))
            weights.append(a.astype(BF16))
            c_ref[p] = c + jnp.sum(log_not, axis=1, keepdims=True)
        for (p, _, _), ks, a in zip(jobs, starts, weights):
            acc_ref[p] += _dot(a, v_ref[0, pl.ds(ks, tq), pairs[p]])

    def live():
        return (jnp.max(c_ref[...]) > SB_SKIP_LOG2).astype(jnp.int32)

    c_ref[...] = jnp.zeros(c_ref.shape, F32)
    acc_ref[...] = jnp.zeros(acc_ref.shape, F32)
    left = jnp.maximum(i - 1, 0)
    run([(p, i, "diag") for p in range(npairs)] + [(p, left, "left") for p in range(npairs)])

    def cond(carry):
        jj, go = carry
        return jnp.logical_and(jj < i, go > 0)

    def body(carry):
        jj, _ = carry
        run([(p, i - 1 - jj, "plain") for p in range(npairs)])
        return jj + 1, live()

    lax.while_loop(cond, body, (jnp.int32(1), live()))
    for p in range(npairs):
        o_ref[0, :, p * LANES:(p + 1) * LANES] = jnp.where(
            lane < SB_DIM, acc_ref[p, :tq], acc_ref[p, tq:]).astype(o_ref.dtype)


def _sb_attn(proj_bf, tq, nheads):
    b, s, _ = proj_bf.shape
    assert tq == SB_SUB
    ngroup = SB_HEADS // nheads
    width = nheads * SB_DIM
    idx = np.arange(SB_SUB)
    tri = (idx[:, None] > idx[None, :]).astype(np.float32)
    u = jnp.asarray(np.concatenate([tri, tri], axis=0), dtype=BF16)
    return pl.pallas_call(
        functools.partial(_sb_attn_kernel, tq=tq, nheads=nheads),
        grid=(b, ngroup, s // tq),
        in_specs=[pl.BlockSpec((1, tq, width), lambda bi, p, i: (bi, i, p)),
                  pl.BlockSpec((1, s, width), lambda bi, p, i: (bi, 0, ngroup + p)),
                  pl.BlockSpec((1, s, width), lambda bi, p, i: (bi, 0, 2 * ngroup + p)),
                  pl.BlockSpec(u.shape, lambda bi, p, i: (0, 0))],
        out_specs=pl.BlockSpec((1, tq, width), lambda bi, p, i: (bi, i, p)),
        out_shape=jax.ShapeDtypeStruct((b, s, SB_HEADS * SB_DIM), BF16),
        scratch_shapes=[pltpu.VMEM((nheads // 2, 2 * tq, LANES), BF16),
                        pltpu.VMEM((nheads // 2, 2 * tq, LANES), F32),
                        pltpu.VMEM((nheads // 2, 2 * tq, LANES), F32)],
        compiler_params=_params(("parallel", "parallel", "arbitrary")),
        name="sb_attn",
    )(proj_bf, proj_bf, proj_bf, u)


def _retention_kernel(q_ref, k_ref, g_ref, v_ref, cos_ref, sin_ref, dec_ref, zeta_ref, xi_ref,
                      cd_ref, bd_ref, avg_ref, gn_ref, o_ref, state_ref, *, nchunk):
    @pl.when(pl.program_id(1) == 0)
    def _():
        state_ref[...] = jnp.zeros(state_ref.shape, F32)

    rows = nchunk * CHUNK
    width = RET_HEADS * RET_DK
    npair = RET_HEADS // 2
    half = RET_DK // 2
    lane_w = lax.broadcasted_iota(jnp.int32, (rows, width), 1)
    first_half = (lane_w % RET_DK) < half
    head0 = (lane_w % LANES) < RET_DK
    cos_t = _rep_lanes(cos_ref[0], width // LANES)
    sin_t = _rep_lanes(sin_ref[0], width // LANES)

    def rope(x):
        swapped = jnp.where(first_half, pltpu.roll(x, width - half, 1), pltpu.roll(x, half, 1))
        return x * cos_t + swapped * sin_t

    q = rope(q_ref[0])
    k = rope(k_ref[0])
    kz = (k * zeta_ref[...]).astype(BF16)
    qb = q.astype(BF16)
    kb = k.astype(BF16)
    zero = jnp.zeros_like(qb)
    q0 = jnp.where(head0, qb, zero)
    q1 = jnp.where(head0, zero, qb)
    v = v_ref[0]
    units = [(c, p) for c in range(nchunk) for p in range(npair)]
    rs = lambda c: slice(c * CHUNK, (c + 1) * CHUNK)
    ls = lambda p: slice(p * LANES, (p + 1) * LANES)
    scores = [_dot_nt(jnp.concatenate([q0[rs(c), ls(p)], q1[rs(c), ls(p)]], axis=0), kb[rs(c), ls(p)])
              for c, p in units]
    scores = [(s * dec_ref[p]).astype(BF16) for s, (c, p) in zip(scores, units)]
    intra = [_dot(s, v[rs(c), ls(p)]) for s, (c, p) in zip(scores, units)]
    kvs = [lax.dot_general(kz[rs(c), ls(p)], v[rs(c), ls(p)], TN_DIMS, preferred_element_type=F32)
           for c, p in units]
    lane = lax.broadcasted_iota(jnp.int32, (CHUNK, LANES), 1)
    chunks = []
    for c in range(nchunk):
        ys = []
        for p in range(npair):
            u = c * npair + p
            state = state_ref[p]
            y_cross = _dot(qb[rs(c), ls(p)], state.astype(BF16)) * xi_ref[:, ls(p)]
            state_ref[p] = state * cd_ref[p] + kvs[u] * bd_ref[...]
            y_intra = jnp.where(lane < RET_DV, intra[u][:CHUNK], intra[u][CHUNK:])
            ys.append(y_intra + y_cross)
        chunks.append(jnp.concatenate(ys, axis=1))
    y = jnp.concatenate(chunks, axis=0)
    avg = avg_ref[...]

    def head_mean(a):
        return jnp.concatenate([_dot_hilo(a[:, c0:c0 + MXU_TILE], avg) for c0 in range(0, width, MXU_TILE)], axis=1)

    d = y - head_mean(y)
    var = head_mean(d * d)
    yn = d * lax.rsqrt(var + EPS) * gn_ref[...]
    g = g_ref[0]
    o_ref[0] = (g * jax.nn.sigmoid(g) * yn).astype(o_ref.dtype)


def _retention_consts(nchunk):
    h = np.arange(RET_HEADS, dtype=np.float64)
    log_gamma = np.log1p(-np.exp2(-5.0 - h))
    idx = np.arange(CHUNK, dtype=np.float64)
    diff = idx[:, None] - idx[None, :]
    dec = np.where(diff[None] >= 0, np.exp(np.maximum(diff, 0.0)[None] * log_gamma[:, None, None]), 0.0)
    npair = RET_HEADS // 2
    dec = dec.reshape(npair, 2 * CHUNK, CHUNK)
    zeta = np.exp((CHUNK - 1 - idx)[:, None] * log_gamma[None, :])
    xi = np.exp((idx + 1.0)[:, None] * log_gamma[None, :])
    zeta_l = np.tile(np.repeat(zeta, RET_DK, axis=1), (nchunk, 1))
    xi_l = np.repeat(xi, RET_DV, axis=1)
    chunk_decay = np.exp(CHUNK * log_gamma)
    head_of = np.arange(LANES) // RET_DK
    bd = (head_of[:, None] == head_of[None, :]).astype(np.float64)
    cd = np.stack([bd * chunk_decay[2 * p + head_of][:, None] for p in range(npair)])
    gh = np.arange(MXU_TILE) // RET_DV
    avg = (gh[:, None] == gh[None, :]).astype(np.float64) / RET_DV
    f = lambda a: jnp.asarray(a, dtype=F32)
    return f(dec), f(zeta_l), f(xi_l), f(cd), f(bd), jnp.asarray(avg, dtype=BF16)


def _retention(proj_ret, proj_bf, cos_t, sin_t, gn, nchunk):
    b, s, _ = proj_ret.shape
    width = RET_HEADS * RET_DK
    rows = nchunk * CHUNK
    dec, zeta_l, xi_l, cd, bd, avg = _retention_consts(nchunk)
    full = lambda a: pl.BlockSpec(a.shape, lambda bi, n: (0,) * a.ndim)
    blk = lambda c: pl.BlockSpec((1, rows, width), lambda bi, n: (bi, n, c))
    tab = pl.BlockSpec((1, rows, LANES), lambda bi, n: (bi, n, 0))
    return pl.pallas_call(
        functools.partial(_retention_kernel, nchunk=nchunk),
        grid=(b, s // rows),
        in_specs=[blk(0), blk(1), blk(2), blk(3), tab, tab,
                  full(dec), full(zeta_l), full(xi_l), full(cd), full(bd), full(avg), full(gn)],
        out_specs=blk(0),
        out_shape=jax.ShapeDtypeStruct((b, s, RET_HEADS * RET_DV), BF16),
        scratch_shapes=[pltpu.VMEM((RET_HEADS // 2, LANES, LANES), F32)],
        compiler_params=_params(("parallel", "arbitrary")),
        name="retention",
    )(proj_ret, proj_ret, proj_ret, proj_bf, cos_t, sin_t, dec, zeta_l, xi_l, cd, bd, avg, gn)


def _merge_kernel(x_ref, g_ref, ya_ref, yb_ref, yc_ref, wg_ref, wb_ref, wo_ref, o_ref):
    x = x_ref[...]
    h = _rms(x, g_ref[...]).astype(BF16)
    merged = None
    for n, y_ref in enumerate((ya_ref, yb_ref, yc_ref)):
        gate = jax.nn.sigmoid(_dot(h, wg_ref[:, n * D_MODEL:(n + 1) * D_MODEL]))
        term = gate * _dot(y_ref[...], wb_ref[n])
        merged = term if merged is None else merged + term
    o_ref[...] = x + _dot(merged.astype(BF16), wo_ref[...])


def _merge(x2d, g, ya, yb, yc, wg, wb, wo, l, tm):
    t = x2d.shape[0]
    full = lambda a: _layer_spec(a, l)
    row = lambda w: pl.BlockSpec((tm, w), lambda i: (i, 0))
    return pl.pallas_call(
        _merge_kernel,
        grid=(t // tm,),
        in_specs=[row(D_MODEL), full(g), row(BRANCH_WIDTH), row(BRANCH_WIDTH), row(BRANCH_WIDTH),
                  full(wg), full(wb), full(wo)],
        out_specs=row(D_MODEL),
        out_shape=jax.ShapeDtypeStruct((t, D_MODEL), F32),
        compiler_params=_params(("parallel",)),
        name="merge",
    )(x2d, g, ya, yb, yc, wg, wb, wo)


MLP_CHUNK = 1024


def _mlp_kernel(x_ref, g_ref, wu_ref, wd_ref, gf_ref, o_ref, *, final_norm):
    x = x_ref[...]
    h = _rms(x, g_ref[...]).astype(BF16)
    acc = None
    for c0 in range(0, D_FF, MLP_CHUNK):
        u = jnp.maximum(_dot(h, wu_ref[:, c0:c0 + MLP_CHUNK]), 0.0)
        part = _dot((u * u).astype(BF16), wd_ref[c0:c0 + MLP_CHUNK, :])
        acc = part if acc is None else acc + part
    y = x + acc
    o_ref[...] = _rms(y, gf_ref[...]) if final_norm else y


def _mlp(x2d, g, wu, wd, gf, l, final_norm, tm):
    t, d = x2d.shape
    full = lambda a: _layer_spec(a, l)
    row = pl.BlockSpec((tm, d), lambda i: (i, 0))
    return pl.pallas_call(
        functools.partial(_mlp_kernel, final_norm=final_norm),
        grid=(t // tm,),
        in_specs=[row, full(g), full(wu), full(wd), pl.BlockSpec(gf.shape, lambda i: (0, 0))],
        out_specs=row,
        out_shape=jax.ShapeDtypeStruct((t, d), F32),
        compiler_params=_params(("parallel",)),
        name="mlp",
    )(x2d, g, wu, wd, gf)


def _pick(n, pref):
    t = min(pref, n)
    while n % t:
        t //= 2
    return t


def _rope_tables(positions):
    pos = positions.astype(F32)[..., None]
    b, s = positions.shape
    lane = np.arange(LANES)

    def cs(dim, freq_of_lane):
        inv_freq = ROPE_BASE ** (-jnp.arange(0, dim, 2, dtype=F32) / dim)
        ang = pos * inv_freq[freq_of_lane]
        return jnp.cos(ang), jnp.sin(ang)

    hm = MLA_ROPE // 2
    is_rope = (lane >= MLA_NOPE) & (lane < MLA_NOPE + MLA_ROPE)
    cm, sm = cs(MLA_ROPE, np.where(is_rope, (lane - MLA_NOPE) % hm, 0))
    sign_m = np.where(lane < MLA_NOPE + hm, -1.0, 1.0).astype(np.float32)
    mla_cos = jnp.where(lane < MLA_NOPE, 1.0, jnp.where(is_rope, cm, 0.0)).reshape(b * s, LANES)
    mla_sin = jnp.where(is_rope, sign_m * sm, 0.0).reshape(b * s, LANES)
    hr = RET_DK // 2
    ret_cos, sr = cs(RET_DK, lane % hr)
    ret_sin = np.where(lane % RET_DK < hr, -1.0, 1.0).astype(np.float32) * sr
    return mla_cos, mla_sin, ret_cos, ret_sin


def _prep_weights(w_in, mla_w_uq, mla_w_ukv, w_branch, w_out, w_up, w_down):
    depth, d, _ = w_in.shape
    offs = [int(o) for o in np.cumsum((0, MLA_Q_RANK, MLA_KV_RANK, MLA_ROPE) + (BRANCH_WIDTH,) * 7
                                      + (N_BRANCH * D_MODEL,))]
    col_scale = np.ones(offs[-1], np.float32)
    col_scale[offs[3]:offs[4]] = SB_DIM ** -0.5 * LOG2E
    col_scale[offs[7]:offs[8]] = RET_DK ** -0.5
    w = (w_in * col_scale).astype(BF16)
    seg = lambda n: w[:, :, offs[n]:offs[n + 1]]
    c_q, c_kv, k_pe, sb_q, sb_k, sb_v, r_q, r_k, r_v, r_g, gates = (seg(n) for n in range(11))
    zeros = lambda *shape: jnp.zeros(shape, BF16)
    pad = LANES - MLA_NOPE - MLA_ROPE
    w_proj = jnp.concatenate(
        [c_q, c_kv,
         zeros(depth, d, MLA_NOPE), k_pe, zeros(depth, d, pad),
         r_q, r_k, r_g,
         sb_q, sb_k, sb_v, r_v], axis=-1)

    uq = mla_w_uq.astype(BF16).reshape(depth, MLA_Q_RANK, MLA_HEADS, MLA_NOPE + MLA_ROPE)
    zq = lambda width: jnp.zeros((depth, MLA_Q_RANK, MLA_HEADS, width), BF16)
    nope, rope = uq[..., :MLA_NOPE], uq[..., MLA_NOPE:]
    wqa = jnp.concatenate([nope, rope, zq(pad)], axis=-1)
    ukv = mla_w_ukv.astype(BF16).reshape(depth, MLA_KV_RANK, MLA_HEADS, MLA_NOPE + MLA_V)
    wk = ukv[..., :MLA_NOPE]
    wv = ukv[..., MLA_NOPE:]
    flat = lambda a: a.reshape(depth, a.shape[1], -1)
    bf = lambda a: a.astype(BF16)
    return dict(w_in=w_proj, w_gate=gates,
                wqa=flat(wqa), wk_t=jnp.swapaxes(flat(wk), 1, 2), wv=flat(wv),
                wb=bf(w_branch), wo=bf(w_out), wu=bf(w_up), wd=bf(w_down))


def kernel(x, positions, norm_mix_g, w_in, mla_q_norm_g, mla_w_uq, mla_kv_norm_g, mla_w_ukv, ret_norm_g, w_branch, w_out, norm_mlp_g, w_up, w_down, final_norm_g):
    b, s, d = x.shape
    depth = w_in.shape[0]
    t = b * s
    assert d == D_MODEL and s % CHUNK == 0
    tm_row = _pick(t, 512)
    t_mla = _pick(s, 1024)
    t_sb = max(_pick(s, 256), SB_SUB)
    w = _prep_weights(w_in, mla_w_uq, mla_w_ukv, w_branch, w_out, w_up, w_down)
    mla_cos, mla_sin, ret_cos, ret_sin = _rope_tables(positions)
    vec = lambda a: a.reshape(1, -1)
    gains = lambda a: a.reshape(depth, 1, -1)
    g_mix, g_q, g_kv, g_mlp = gains(norm_mix_g), gains(mla_q_norm_g), gains(mla_kv_norm_g), gains(norm_mlp_g)

    x2d = x.reshape(t, d)
    for l in range(depth):
        proj_ret, proj_bf, q_a, k_t, v_a = _in_proj(x2d, g_mix, w["w_in"], mla_cos, mla_sin, g_q, g_kv,
                                                    w["wqa"], w["wk_t"], w["wv"], l, b, tm_row)
        proj_bf = proj_bf.reshape(b, s, -1)

        y_a = _mla_attn(q_a.reshape(b, s, -1), k_t, v_a.reshape(b, s, -1), t_mla, 4)
        y_b = _sb_attn(proj_bf, t_sb, 8)
        y_c = _retention(proj_ret.reshape(b, s, -1), proj_bf, ret_cos, ret_sin, vec(ret_norm_g[l]),
                         _pick(s // CHUNK, 8))

        x2d = _merge(x2d, g_mix, y_a.reshape(t, -1), y_b.reshape(t, -1), y_c.reshape(t, -1),
                     w["w_gate"], w["wb"], w["wo"], l, tm_row)
        x2d = _mlp(x2d, g_mlp, w["wu"], w["wd"], vec(final_norm_g), l, l == depth - 1, tm_row)
    return x2d.reshape(b, s, d)
```

```python
import functools

import numpy as np
import jax
import jax.numpy as jnp
from jax import lax
from jax.experimental import pallas as pl
from jax.experimental.pallas import tpu as pltpu

F32 = jnp.float32
BF16 = jnp.bfloat16

D_MODEL = 1024
EPS = 1e-6
ROPE_BASE = 10000.0
CHUNK = 128

MLA_HEADS = 8
MLA_NOPE = 64
MLA_ROPE = 32
MLA_V = 64
MLA_Q_RANK = 384
MLA_KV_RANK = 256
SB_HEADS = 8
SB_DIM = 64
RET_HEADS = 8
RET_DK = 64
RET_DV = 64
BRANCH_WIDTH = 512
N_BRANCH = 3
D_FF = 4 * D_MODEL

LANES = 128
MXU_TILE = 256
LAT_WIDTH = MLA_Q_RANK + MLA_KV_RANK + LANES
VMEM_LIMIT = 56 * 1024 * 1024
LOG2E = 1.4426950408889634
MLA_Q_SCALE = (MLA_NOPE + MLA_ROPE) ** -0.5 * LOG2E
SIGN_BIT = np.uint32(0x80000000)
SB_MASKED_LOGIT = -1e30
SB_SUB = MXU_TILE
SB_SKIP_LOG2 = -135.0

NT_DIMS = (((1,), (1,)), ((), ()))
TN_DIMS = (((0,), (0,)), ((), ()))


def _dot(a, b):
    return jnp.dot(a, b, preferred_element_type=F32)


def _dot_nt(a, b):
    return lax.dot_general(a, b, NT_DIMS, preferred_element_type=F32)


def _dot_hilo(x, m_bf16):
    hi = x.astype(BF16)
    lo = (x - hi.astype(F32)).astype(BF16)
    return _dot(hi, m_bf16) + _dot(lo, m_bf16)


def _rep_lanes(x, n):
    return x if n == 1 else jnp.concatenate([x] * n, axis=1)


def _rms(x, g):
    return x * lax.rsqrt(jnp.mean(x * x, axis=-1, keepdims=True) + EPS) * g


def _layer_spec(stacked, l):
    nd = stacked.ndim - 1
    return pl.BlockSpec((None,) + stacked.shape[1:], lambda *_: (l,) + (0,) * nd, pipeline_mode=pl.Buffered(1))


def _params(sem):
    return pltpu.CompilerParams(dimension_semantics=sem, vmem_limit_bytes=VMEM_LIMIT)


IN_GROUPS = ((3 * BRANCH_WIDTH, F32),
             (4 * BRANCH_WIDTH, BF16))
IN_CHUNK = 512


def _swap_rope_halves(x):
    width = x.shape[1]
    half = MLA_ROPE // 2
    lane = lax.broadcasted_iota(jnp.int32, x.shape, 1) % LANES
    return jnp.where(lane < MLA_NOPE + half, pltpu.roll(x, width - half, 1), pltpu.roll(x, half, 1))


def _mla_prep(lat, cos_t, sin_t, gq, gkv, wqa, wk_t, wv, q_ref, k_ref, v_ref):
    c_q = lat[:, :MLA_Q_RANK]
    c_kv = lat[:, MLA_Q_RANK:MLA_Q_RANK + MLA_KV_RANK]
    k_pe = lat[:, MLA_Q_RANK + MLA_KV_RANK:]
    nq = _rms(c_q, gq).astype(BF16)
    nkv = _rms(c_kv, gkv).astype(BF16)
    q = _dot(nq, wqa)
    q = q * _rep_lanes(cos_t, MLA_HEADS) + _swap_rope_halves(q) * _rep_lanes(sin_t, MLA_HEADS)
    q_ref[...] = (q * MLA_Q_SCALE).astype(BF16)
    k_rot_t = jnp.transpose(k_pe * cos_t + _swap_rope_halves(k_pe) * sin_t)
    k_nope_t = _dot_nt(wk_t, nkv)
    k_rot_rows = k_rot_t[MLA_NOPE:]
    k_ref[0] = jnp.concatenate(
        [part for h in range(MLA_HEADS) for part in (k_nope_t[h * MLA_NOPE:(h + 1) * MLA_NOPE], k_rot_rows)],
        axis=0).astype(BF16)
    v_ref[...] = _dot(nkv, wv).astype(BF16)


def _in_proj_kernel(x_ref, g_ref, w_ref, cos_ref, sin_ref, gq_ref, gkv_ref, wqa_ref, wk_ref, wv_ref,
                    ret_ref, bf_ref, q_ref, k_ref, v_ref):
    h = _rms(x_ref[...], g_ref[...]).astype(BF16)
    lat = jnp.concatenate([_dot(h, w_ref[:, c0:min(c0 + IN_CHUNK, LAT_WIDTH)])
                           for c0 in range(0, LAT_WIDTH, IN_CHUNK)], axis=1)
    _mla_prep(lat, cos_ref[...], sin_ref[...], gq_ref[...], gkv_ref[...], wqa_ref[...],
              wk_ref[...], wv_ref[...], q_ref, k_ref, v_ref)
    col = LAT_WIDTH
    for o_ref, (width, _) in zip((ret_ref, bf_ref), IN_GROUPS):
        for c0 in range(0, width, IN_CHUNK):
            o_ref[:, c0:c0 + IN_CHUNK] = _dot(h, w_ref[:, col + c0:col + c0 + IN_CHUNK]).astype(o_ref.dtype)
        col += width


def _in_proj(x2d, g, w, cos_t, sin_t, gq, gkv, wqa, wk_t, wv, l, b, tm):
    t, d = x2d.shape
    hw = MLA_HEADS * LANES
    nsb = t // b // tm
    full = lambda a: _layer_spec(a, l)
    row = lambda width: pl.BlockSpec((tm, width), lambda i: (i, 0))
    return pl.pallas_call(
        _in_proj_kernel,
        grid=(t // tm,),
        in_specs=[row(d), full(g), full(w), row(LANES), row(LANES), full(gq), full(gkv),
                  full(wqa), full(wk_t), full(wv)],
        out_specs=[row(IN_GROUPS[0][0]), row(IN_GROUPS[1][0]), row(hw),
                   pl.BlockSpec((1, hw, tm), lambda i: (i // nsb, 0, i % nsb)), row(MLA_HEADS * MLA_V)],
        out_shape=[jax.ShapeDtypeStruct((t, IN_GROUPS[0][0]), IN_GROUPS[0][1]),
                   jax.ShapeDtypeStruct((t, IN_GROUPS[1][0]), IN_GROUPS[1][1]),
                   jax.ShapeDtypeStruct((t, hw), BF16), jax.ShapeDtypeStruct((b, hw, t // b), BF16),
                   jax.ShapeDtypeStruct((t, MLA_HEADS * MLA_V), BF16)],
        compiler_params=_params(("parallel",)),
        name="in_proj",
    )(x2d, g, w, cos_t, sin_t, gq, gkv, wqa, wk_t, wv)


def _mla_attn_kernel(q_ref, kt_ref, v_ref, o_ref, m_ref, l_ref, acc_ref, *, tq, tk, nheads):
    i = pl.program_id(2)
    npairs = nheads // 2
    heads = [slice(e * LANES, (e + 1) * LANES) for e in range(nheads)]

    def tiles(ks, nk, r0, nr, masked):
        rows = slice(r0, r0 + nr)
        scores = [_dot(q_ref[0, rows, heads[e]], kt_ref[0, heads[e], pl.ds(ks, nk)]) for e in range(nheads)]
        if masked:
            q_pos = i * tq + r0 + lax.broadcasted_iota(jnp.int32, (nr, nk), 0)
            k_pos = ks + lax.broadcasted_iota(jnp.int32, (nr, nk), 1)
            scores = [jnp.where(k_pos <= q_pos, s, -jnp.inf) for s in scores]
        probs, alphas = [], []
        for e, s in enumerate(scores):
            m_prev = m_ref[e, rows]
            m_next = jnp.maximum(m_prev, jnp.max(s, axis=1, keepdims=True))
            alpha = jnp.exp2(m_prev - m_next)
            p = jnp.exp2(s - _rep_lanes(m_next, nk // LANES))
            partial = functools.reduce(jnp.add, [p[:, c:c + LANES] for c in range(0, nk, LANES)])
            l_ref[e, rows] = alpha * l_ref[e, rows] + partial
            m_ref[e, rows] = m_next
            probs.append(p.astype(BF16))
            alphas.append(alpha)
        for p in range(npairs):
            pv = _dot(jnp.concatenate([probs[2 * p], probs[2 * p + 1]], axis=0), v_ref[0, pl.ds(ks, nk), heads[p]])
            for e in range(2):
                arows = slice(e * tq + r0, e * tq + r0 + nr)
                acc_ref[p, arows] = alphas[2 * p + e] * acc_ref[p, arows] + pv[e * nr:(e + 1) * nr]

    m_ref[...] = jnp.full(m_ref.shape, -jnp.inf, F32)
    l_ref[...] = jnp.zeros(l_ref.shape, F32)
    acc_ref[...] = jnp.zeros(acc_ref.shape, F32)

    def body(j, carry):
        tiles(pl.multiple_of(j * tq, tq), tq, 0, tq, False)
        return carry

    lax.fori_loop(0, i, body, 0)
    diag = pl.multiple_of(i * tq, tq)
    tiles(diag, tk, 0, tq, True)
    tiles(diag + tk, tk, tk, tq - tk, True)
    lane = lax.broadcasted_iota(jnp.int32, (tq, LANES), 1)
    denom = [jnp.sum(l_ref[e], axis=1, keepdims=True) for e in range(nheads)]
    for p in range(npairs):
        o_ref[0, :, heads[p]] = jnp.where(
            lane < MLA_V, acc_ref[p, :tq] / denom[2 * p], acc_ref[p, tq:] / denom[2 * p + 1]).astype(o_ref.dtype)


def _mla_attn(q, k_t, v, tq, nheads):
    b, s, _ = q.shape
    tk = tq // 2
    return pl.pallas_call(
        functools.partial(_mla_attn_kernel, tq=tq, tk=tk, nheads=nheads),
        grid=(b, MLA_HEADS // nheads, s // tq),
        in_specs=[pl.BlockSpec((1, tq, nheads * LANES), lambda bi, p, i: (bi, i, p)),
                  pl.BlockSpec((1, nheads * LANES, s), lambda bi, p, i: (bi, p, 0)),
                  pl.BlockSpec((1, s, nheads * MLA_V), lambda bi, p, i: (bi, 0, p))],
        out_specs=pl.BlockSpec((1, tq, nheads * MLA_V), lambda bi, p, i: (bi, i, p)),
        out_shape=jax.ShapeDtypeStruct((b, s, MLA_HEADS * MLA_V), BF16),
        scratch_shapes=[pltpu.VMEM((nheads, tq, LANES), F32), pltpu.VMEM((nheads, tq, LANES), F32),
                        pltpu.VMEM((nheads // 2, 2 * tq, LANES), F32)],
        compiler_params=_params(("parallel", "parallel", "arbitrary")),
        name="mla_attn",
    )(q, k_t, v)


def _sb_attn_kernel(q_ref, k_ref, v_ref, u_ref, o_ref, qm_ref, c_ref, acc_ref, *, tq, nheads):
    i = pl.program_id(2)
    npairs = nheads // 2
    lane = lax.broadcasted_iota(jnp.int32, (tq, LANES), 1)
    zero = jnp.zeros((tq, LANES), BF16)
    for p in range(npairs):
        q_pair = q_ref[0, :, p * LANES:(p + 1) * LANES]
        qm_ref[p, :tq] = jnp.where(lane < SB_DIM, q_pair, zero)
        qm_ref[p, tq:] = jnp.where(lane < SB_DIM, zero, q_pair)

    pairs = [slice(p * LANES, (p + 1) * LANES) for p in range(npairs)]
    has_left = i > 0
    hq = tq // 2
    row_sets = {"all": ((0, 2 * tq),), "upper": ((0, hq), (tq, hq)), "lower": ((hq, hq), (tq + hq, hq))}

    def load(ref, p, sel):
        parts = [ref[p, s:s + n] for s, n in row_sets[sel]]
        return parts[0] if len(parts) == 1 else jnp.concatenate(parts, axis=0)

    def store(ref, p, sel, val):
        off = 0
        for s, n in row_sets[sel]:
            ref[p, s:s + n] = val[off:off + n]
            off += n

    def diag_mask(nk, first_row):
        row = lax.broadcasted_iota(jnp.int32, (2 * hq, nk), 0) % hq + first_row
        return lax.broadcasted_iota(jnp.int32, (2 * hq, nk), 1) < row

    masks = {"diag_upper": diag_mask(hq, 0), "diag_lower": diag_mask(tq, hq)}
    tri = {tq: u_ref[...], hq: jnp.concatenate([u_ref[:hq, :hq], u_ref[tq:tq + hq, :hq]], axis=0)}

    def run(jobs):
        zs = [_dot_nt(load(qm_ref, p, sel), k_ref[0, pl.ds(ks, nk), pairs[p]]) for p, ks, nk, sel, _ in jobs]
        stats = []
        for z, (_, _, _, _, kind) in zip(zs, jobs):
            if kind in masks:
                z = jnp.where(masks[kind], z, SB_MASKED_LOGIT)
            neg_abs = lax.bitcast_convert_type(lax.bitcast_convert_type(z, jnp.uint32) | SIGN_BIT, F32)
            log_beta = jnp.minimum(z, 0.0) - jnp.log(1.0 + jnp.exp2(neg_abs)) * LOG2E
            log_not = log_beta - z
            hi = log_not.astype(BF16)
            lo = (log_not - hi.astype(F32)).astype(BF16)
            stats.append((log_beta, log_not, jnp.concatenate([hi, lo], axis=1)))
        withins = [_dot(hilo, tri[nk]) for (_, _, nk, _, _), (_, _, hilo) in zip(jobs, stats)]
        weights = []
        for (p, _, nk, sel, kind), (log_beta, log_not, _), within in zip(jobs, stats, withins):
            c = load(c_ref, p, sel)
            if kind == "left":
                c = jnp.where(has_left, c, -jnp.inf)
            a = jnp.exp2(log_beta + within + _rep_lanes(c, nk // LANES))
            weights.append(a.astype(BF16))
            store(c_ref, p, sel, c + jnp.sum(log_not, axis=1, keepdims=True))
        for (p, ks, nk, sel, _), a in zip(jobs, weights):
            store(acc_ref, p, sel, load(acc_ref, p, sel) + _dot(a, v_ref[0, pl.ds(ks, nk), pairs[p]]))

    def live():
        return (jnp.max(c_ref[...]) > SB_SKIP_LOG2).astype(jnp.int32)

    c_ref[...] = jnp.zeros(c_ref.shape, F32)
    acc_ref[...] = jnp.zeros(acc_ref.shape, F32)
    here = pl.multiple_of(i * tq, tq)
    left = pl.multiple_of(jnp.maximum(i - 1, 0) * tq, tq)
    run([(p, here, hq, "upper", "diag_upper") for p in range(npairs)]
        + [(p, here, tq, "lower", "diag_lower") for p in range(npairs)]
        + [(p, left, tq, "all", "left") for p in range(npairs)])

    def cond(carry):
        jj, go = carry
        return jnp.logical_and(jj < i, go > 0)

    def body(carry):
        jj, _ = carry
        ks = pl.multiple_of((i - 1 - jj) * tq, tq)
        run([(p, ks, tq, "all", "plain") for p in range(npairs)])
        return jj + 1, live()

    lax.while_loop(cond, body, (jnp.int32(1), live()))
    for p in range(npairs):
        o_ref[0, :, p * LANES:(p + 1) * LANES] = jnp.where(
            lane < SB_DIM, acc_ref[p, :tq], acc_ref[p, tq:]).astype(o_ref.dtype)


def _sb_attn(proj_bf, tq, nheads):
    b, s, _ = proj_bf.shape
    assert tq == SB_SUB
    ngroup = SB_HEADS // nheads
    width = nheads * SB_DIM
    idx = np.arange(SB_SUB)
    tri = (idx[:, None] > idx[None, :]).astype(np.float32)
    u = jnp.asarray(np.concatenate([tri, tri], axis=0), dtype=BF16)
    return pl.pallas_call(
        functools.partial(_sb_attn_kernel, tq=tq, nheads=nheads),
        grid=(b, ngroup, s // tq),
        in_specs=[pl.BlockSpec((1, tq, width), lambda bi, p, i: (bi, i, p)),
                  pl.BlockSpec((1, s, width), lambda bi, p, i: (bi, 0, ngroup + p)),
                  pl.BlockSpec((1, s, width), lambda bi, p, i: (bi, 0, 2 * ngroup + p)),
                  pl.BlockSpec(u.shape, lambda bi, p, i: (0, 0))],
        out_specs=pl.BlockSpec((1, tq, width), lambda bi, p, i: (bi, i, p)),
        out_shape=jax.ShapeDtypeStruct((b, s, SB_HEADS * SB_DIM), BF16),
        scratch_shapes=[pltpu.VMEM((nheads // 2, 2 * tq, LANES), BF16),
                        pltpu.VMEM((nheads // 2, 2 * tq, LANES), F32),
                        pltpu.VMEM((nheads // 2, 2 * tq, LANES), F32)],
        compiler_params=_params(("parallel", "parallel", "arbitrary")),
        name="sb_attn",
    )(proj_bf, proj_bf, proj_bf, u)


def _retention_kernel(q_ref, k_ref, g_ref, v_ref, cos_ref, sin_ref, dec_ref, zeta_ref, xi_ref,
                      cd_ref, bd_ref, avg_ref, gn_ref, o_ref, state_ref, *, nchunk):
    @pl.when(pl.program_id(1) == 0)
    def _():
        state_ref[...] = jnp.zeros(state_ref.shape, F32)

    rows = nchunk * CHUNK
    width = RET_HEADS * RET_DK
    npair = RET_HEADS // 2
    half = RET_DK // 2
    lane_w = lax.broadcasted_iota(jnp.int32, (rows, width), 1)
    first_half = (lane_w % RET_DK) < half
    head0 = (lane_w % LANES) < RET_DK
    cos_t = _rep_lanes(cos_ref[0], width // LANES)
    sin_t = _rep_lanes(sin_ref[0], width // LANES)

    def rope(x):
        swapped = jnp.where(first_half, pltpu.roll(x, width - half, 1), pltpu.roll(x, half, 1))
        return x * cos_t + swapped * sin_t

    q = rope(q_ref[0])
    k = rope(k_ref[0])
    kz = (k * zeta_ref[...]).astype(BF16)
    qb = q.astype(BF16)
    kb = k.astype(BF16)
    zero = jnp.zeros_like(qb)
    q0 = jnp.where(head0, qb, zero)
    q1 = jnp.where(head0, zero, qb)
    v = v_ref[0]
    units = [(c, p) for c in range(nchunk) for p in range(npair)]
    rs = lambda c: slice(c * CHUNK, (c + 1) * CHUNK)
    ls = lambda p: slice(p * LANES, (p + 1) * LANES)
    scores = [_dot_nt(jnp.concatenate([q0[rs(c), ls(p)], q1[rs(c), ls(p)]], axis=0), kb[rs(c), ls(p)])
              for c, p in units]
    scores = [(s * dec_ref[p]).astype(BF16) for s, (c, p) in zip(scores, units)]
    intra = [_dot(s, v[rs(c), ls(p)]) for s, (c, p) in zip(scores, units)]
    kvs = [lax.dot_general(kz[rs(c), ls(p)], v[rs(c), ls(p)], TN_DIMS, preferred_element_type=F32)
           for c, p in units]
    lane = lax.broadcasted_iota(jnp.int32, (CHUNK, LANES), 1)
    chunks = []
    for c in range(nchunk):
        ys = []
        for p in range(npair):
            u = c * npair + p
            state = state_ref[p]
            y_cross = _dot(qb[rs(c), ls(p)], state.astype(BF16)) * xi_ref[:, ls(p)]
            state_ref[p] = state * cd_ref[p] + kvs[u] * bd_ref[...]
            y_intra = jnp.where(lane < RET_DV, intra[u][:CHUNK], intra[u][CHUNK:])
            ys.append(y_intra + y_cross)
        chunks.append(jnp.concatenate(ys, axis=1))
    y = jnp.concatenate(chunks, axis=0)
    avg = avg_ref[...]

    def head_mean(a):
        return jnp.concatenate([_dot_hilo(a[:, c0:c0 + MXU_TILE], avg) for c0 in range(0, width, MXU_TILE)], axis=1)

    d = y - head_mean(y)
    var = head_mean(d * d)
    yn = d * lax.rsqrt(var + EPS) * gn_ref[...]
    g = g_ref[0]
    o_ref[0] = (g * jax.nn.sigmoid(g) * yn).astype(o_ref.dtype)


def _retention_consts(nchunk):
    h = np.arange(RET_HEADS, dtype=np.float64)
    log_gamma = np.log1p(-np.exp2(-5.0 - h))
    idx = np.arange(CHUNK, dtype=np.float64)
    diff = idx[:, None] - idx[None, :]
    dec = np.where(diff[None] >= 0, np.exp(np.maximum(diff, 0.0)[None] * log_gamma[:, None, None]), 0.0)
    npair = RET_HEADS // 2
    dec = dec.reshape(npair, 2 * CHUNK, CHUNK)
    zeta = np.exp((CHUNK - 1 - idx)[:, None] * log_gamma[None, :])
    xi = np.exp((idx + 1.0)[:, None] * log_gamma[None, :])
    zeta_l = np.tile(np.repeat(zeta, RET_DK, axis=1), (nchunk, 1))
    xi_l = np.repeat(xi, RET_DV, axis=1)
    chunk_decay = np.exp(CHUNK * log_gamma)
    head_of = np.arange(LANES) // RET_DK
    bd = (head_of[:, None] == head_of[None, :]).astype(np.float64)
    cd = np.stack([bd * chunk_decay[2 * p + head_of][:, None] for p in range(npair)])
    gh = np.arange(MXU_TILE) // RET_DV
    avg = (gh[:, None] == gh[None, :]).astype(np.float64) / RET_DV
    f = lambda a: jnp.asarray(a, dtype=F32)
    return f(dec), f(zeta_l), f(xi_l), f(cd), f(bd), jnp.asarray(avg, dtype=BF16)


def _retention(proj_ret, proj_bf, cos_t, sin_t, gn, nchunk):
    b, s, _ = proj_ret.shape
    width = RET_HEADS * RET_DK
    rows = nchunk * CHUNK
    dec, zeta_l, xi_l, cd, bd, avg = _retention_consts(nchunk)
    full = lambda a: pl.BlockSpec(a.shape, lambda bi, n: (0,) * a.ndim)
    blk = lambda c: pl.BlockSpec((1, rows, width), lambda bi, n: (bi, n, c))
    tab = pl.BlockSpec((1, rows, LANES), lambda bi, n: (bi, n, 0))
    return pl.pallas_call(
        functools.partial(_retention_kernel, nchunk=nchunk),
        grid=(b, s // rows),
        in_specs=[blk(0), blk(1), blk(2), blk(3), tab, tab,
                  full(dec), full(zeta_l), full(xi_l), full(cd), full(bd), full(avg), full(gn)],
        out_specs=blk(0),
        out_shape=jax.ShapeDtypeStruct((b, s, RET_HEADS * RET_DV), BF16),
        scratch_shapes=[pltpu.VMEM((RET_HEADS // 2, LANES, LANES), F32)],
        compiler_params=_params(("parallel", "arbitrary")),
        name="retention",
    )(proj_ret, proj_ret, proj_ret, proj_bf, cos_t, sin_t, dec, zeta_l, xi_l, cd, bd, avg, gn)


def _merge_kernel(x_ref, g_ref, ya_ref, yb_ref, yc_ref, wg_ref, wb_ref, wo_ref, o_ref):
    x = x_ref[...]
    h = _rms(x, g_ref[...]).astype(BF16)
    merged = None
    for n, y_ref in enumerate((ya_ref, yb_ref, yc_ref)):
        gate = jax.nn.sigmoid(_dot(h, wg_ref[:, n * D_MODEL:(n + 1) * D_MODEL]))
        term = gate * _dot(y_ref[...], wb_ref[n])
        merged = term if merged is None else merged + term
    o_ref[...] = x + _dot(merged.astype(BF16), wo_ref[...])


def _merge(x2d, g, ya, yb, yc, wg, wb, wo, l, tm):
    t = x2d.shape[0]
    full = lambda a: _layer_spec(a, l)
    row = lambda w: pl.BlockSpec((tm, w), lambda i: (i, 0))
    return pl.pallas_call(
        _merge_kernel,
        grid=(t // tm,),
        in_specs=[row(D_MODEL), full(g), row(BRANCH_WIDTH), row(BRANCH_WIDTH), row(BRANCH_WIDTH),
                  full(wg), full(wb), full(wo)],
        out_specs=row(D_MODEL),
        out_shape=jax.ShapeDtypeStruct((t, D_MODEL), F32),
        compiler_params=_params(("parallel",)),
        name="merge",
    )(x2d, g, ya, yb, yc, wg, wb, wo)


MLP_CHUNK = 1024


def _mlp_kernel(x_ref, g_ref, wu_ref, wd_ref, gf_ref, o_ref, *, final_norm):
    x = x_ref[...]
    h = _rms(x, g_ref[...]).astype(BF16)
    acc = None
    for c0 in range(0, D_FF, MLP_CHUNK):
        u = jnp.maximum(_dot(h, wu_ref[:, c0:c0 + MLP_CHUNK]), 0.0)
        part = _dot((u * u).astype(BF16), wd_ref[c0:c0 + MLP_CHUNK, :])
        acc = part if acc is None else acc + part
    y = x + acc
    o_ref[...] = _rms(y, gf_ref[...]) if final_norm else y


def _mlp(x2d, g, wu, wd, gf, l, final_norm, tm):
    t, d = x2d.shape
    full = lambda a: _layer_spec(a, l)
    row = pl.BlockSpec((tm, d), lambda i: (i, 0))
    return pl.pallas_call(
        functools.partial(_mlp_kernel, final_norm=final_norm),
        grid=(t // tm,),
        in_specs=[row, full(g), full(wu), full(wd), pl.BlockSpec(gf.shape, lambda i: (0, 0))],
        out_specs=row,
        out_shape=jax.ShapeDtypeStruct((t, d), F32),
        compiler_params=_params(("parallel",)),
        name="mlp",
    )(x2d, g, wu, wd, gf)


def _pick(n, pref):
    t = min(pref, n)
    while n % t:
        t //= 2
    return t


def _rope_tables(positions):
    pos = positions.astype(F32)[..., None]
    b, s = positions.shape
    lane = np.arange(LANES)

    def cs(dim, freq_of_lane):
        inv_freq = ROPE_BASE ** (-jnp.arange(0, dim, 2, dtype=F32) / dim)
        ang = pos * inv_freq[freq_of_lane]
        return jnp.cos(ang), jnp.sin(ang)

    hm = MLA_ROPE // 2
    is_rope = (lane >= MLA_NOPE) & (lane < MLA_NOPE + MLA_ROPE)
    cm, sm = cs(MLA_ROPE, np.where(is_rope, (lane - MLA_NOPE) % hm, 0))
    sign_m = np.where(lane < MLA_NOPE + hm, -1.0, 1.0).astype(np.float32)
    mla_cos = jnp.where(lane < MLA_NOPE, 1.0, jnp.where(is_rope, cm, 0.0)).reshape(b * s, LANES)
    mla_sin = jnp.where(is_rope, sign_m * sm, 0.0).reshape(b * s, LANES)
    hr = RET_DK // 2
    ret_cos, sr = cs(RET_DK, lane % hr)
    ret_sin = np.where(lane % RET_DK < hr, -1.0, 1.0).astype(np.float32) * sr
    return mla_cos, mla_sin, ret_cos, ret_sin


def _prep_weights(w_in, mla_w_uq, mla_w_ukv, w_branch, w_out, w_up, w_down):
    depth, d, _ = w_in.shape
    offs = [int(o) for o in np.cumsum((0, MLA_Q_RANK, MLA_KV_RANK, MLA_ROPE) + (BRANCH_WIDTH,) * 7
                                      + (N_BRANCH * D_MODEL,))]
    col_scale = np.ones(offs[-1], np.float32)
    col_scale[offs[3]:offs[4]] = SB_DIM ** -0.5 * LOG2E
    col_scale[offs[7]:offs[8]] = RET_DK ** -0.5
    w = (w_in * col_scale).astype(BF16)
    seg = lambda n: w[:, :, offs[n]:offs[n + 1]]
    c_q, c_kv, k_pe, sb_q, sb_k, sb_v, r_q, r_k, r_v, r_g, gates = (seg(n) for n in range(11))
    zeros = lambda *shape: jnp.zeros(shape, BF16)
    pad = LANES - MLA_NOPE - MLA_ROPE
    w_proj = jnp.concatenate(
        [c_q, c_kv,
         zeros(depth, d, MLA_NOPE), k_pe, zeros(depth, d, pad),
         r_q, r_k, r_g,
         sb_q, sb_k, sb_v, r_v], axis=-1)

    uq = mla_w_uq.astype(BF16).reshape(depth, MLA_Q_RANK, MLA_HEADS, MLA_NOPE + MLA_ROPE)
    zq = lambda width: jnp.zeros((depth, MLA_Q_RANK, MLA_HEADS, width), BF16)
    nope, rope = uq[..., :MLA_NOPE], uq[..., MLA_NOPE:]
    wqa = jnp.concatenate([nope, rope, zq(pad)], axis=-1)
    ukv = mla_w_ukv.astype(BF16).reshape(depth, MLA_KV_RANK, MLA_HEADS, MLA_NOPE + MLA_V)
    wk = ukv[..., :MLA_NOPE]
    wv = ukv[..., MLA_NOPE:]
    flat = lambda a: a.reshape(depth, a.shape[1], -1)
    bf = lambda a: a.astype(BF16)
    return dict(w_in=w_proj, w_gate=gates,
                wqa=flat(wqa), wk_t=jnp.swapaxes(flat(wk), 1, 2), wv=flat(wv),
                wb=bf(w_branch), wo=bf(w_out), wu=bf(w_up), wd=bf(w_down))


def kernel(x, positions, norm_mix_g, w_in, mla_q_norm_g, mla_w_uq, mla_kv_norm_g, mla_w_ukv, ret_norm_g, w_branch, w_out, norm_mlp_g, w_up, w_down, final_norm_g):
    b, s, d = x.shape
    depth = w_in.shape[0]
    t = b * s
    assert d == D_MODEL and s % CHUNK == 0
    tm_row = _pick(t, 512)
    t_mla = _pick(s, 1024)
    t_sb = max(_pick(s, 256), SB_SUB)
    w = _prep_weights(w_in, mla_w_uq, mla_w_ukv, w_branch, w_out, w_up, w_down)
    mla_cos, mla_sin, ret_cos, ret_sin = _rope_tables(positions)
    vec = lambda a: a.reshape(1, -1)
    gains = lambda a: a.reshape(depth, 1, -1)
    g_mix, g_q, g_kv, g_mlp = gains(norm_mix_g), gains(mla_q_norm_g), gains(mla_kv_norm_g), gains(norm_mlp_g)

    x2d = x.reshape(t, d)
    for l in range(depth):
        proj_ret, proj_bf, q_a, k_t, v_a = _in_proj(x2d, g_mix, w["w_in"], mla_cos, mla_sin, g_q, g_kv,
                                                    w["wqa"], w["wk_t"], w["wv"], l, b, tm_row)
        proj_bf = proj_bf.reshape(b, s, -1)

        y_a = _mla_attn(q_a.reshape(b, s, -1), k_t, v_a.reshape(b, s, -1), t_mla, 4)
        y_b = _sb_attn(proj_bf, t_sb, 8)
        y_c = _retention(proj_ret.reshape(b, s, -1), proj_bf, ret_cos, ret_sin, vec(ret_norm_g[l]),
                         _pick(s // CHUNK, 8))

        x2d = _merge(x2d, g_mix, y_a.reshape(t, -1), y_b.reshape(t, -1), y_c.reshape(t, -1),
                     w["w_gate"], w["wb"], w["wo"], l, tm_row)
        x2d = _mlp(x2d, g_mlp, w["wu"], w["wd"], vec(final_norm_g), l, l == depth - 1, tm_row)
    return x2d.reshape(b, s, d)
```

```python
import functools

import numpy as np
import jax
import jax.numpy as jnp
from jax import lax
from jax.experimental import pallas as pl
from jax.experimental.pallas import tpu as pltpu

F32 = jnp.float32
BF16 = jnp.bfloat16

D_MODEL = 1024
EPS = 1e-6
ROPE_BASE = 10000.0
CHUNK = 128

MLA_HEADS = 8
MLA_NOPE = 64
MLA_ROPE = 32
MLA_V = 64
MLA_Q_RANK = 384
MLA_KV_RANK = 256
SB_HEADS = 8
SB_DIM = 64
RET_HEADS = 8
RET_DK = 64
RET_DV = 64
BRANCH_WIDTH = 512
N_BRANCH = 3
D_FF = 4 * D_MODEL

LANES = 128
MXU_TILE = 256
LAT_WIDTH = MLA_Q_RANK + MLA_KV_RANK + LANES
VMEM_LIMIT = 56 * 1024 * 1024
LOG2E = 1.4426950408889634
MLA_Q_SCALE = (MLA_NOPE + MLA_ROPE) ** -0.5 * LOG2E
SIGN_BIT = np.uint32(0x80000000)
SB_MASKED_LOGIT = -1e30
SB_SUB = MXU_TILE
SB_SKIP_LOG2 = -135.0

NT_DIMS = (((1,), (1,)), ((), ()))
TN_DIMS = (((0,), (0,)), ((), ()))


def _dot(a, b):
    return jnp.dot(a, b, preferred_element_type=F32)


def _dot_nt(a, b):
    return lax.dot_general(a, b, NT_DIMS, preferred_element_type=F32)


def _dot_hilo(x, m_bf16):
    hi = x.astype(BF16)
    lo = (x - hi.astype(F32)).astype(BF16)
    return _dot(hi, m_bf16) + _dot(lo, m_bf16)


def _rep_lanes(x, n):
    return x if n == 1 else jnp.concatenate([x] * n, axis=1)


def _rms(x, g):
    return x * lax.rsqrt(jnp.mean(x * x, axis=-1, keepdims=True) + EPS) * g


def _layer_spec(stacked, l):
    nd = stacked.ndim - 1
    return pl.BlockSpec((None,) + stacked.shape[1:], lambda *_: (l,) + (0,) * nd, pipeline_mode=pl.Buffered(1))


def _params(sem):
    return pltpu.CompilerParams(dimension_semantics=sem, vmem_limit_bytes=VMEM_LIMIT)


IN_GROUPS = ((3 * BRANCH_WIDTH, F32),
             (4 * BRANCH_WIDTH, BF16))
IN_CHUNK = 512


def _swap_rope_halves(x):
    width = x.shape[1]
    half = MLA_ROPE // 2
    lane = lax.broadcasted_iota(jnp.int32, x.shape, 1) % LANES
    return jnp.where(lane < MLA_NOPE + half, pltpu.roll(x, width - half, 1), pltpu.roll(x, half, 1))


def _mla_prep(lat, cos_t, sin_t, gq, gkv, wqa, wk_t, wv, q_ref, k_ref, v_ref):
    c_q = lat[:, :MLA_Q_RANK]
    c_kv = lat[:, MLA_Q_RANK:MLA_Q_RANK + MLA_KV_RANK]
    k_pe = lat[:, MLA_Q_RANK + MLA_KV_RANK:]
    nq = _rms(c_q, gq).astype(BF16)
    nkv = _rms(c_kv, gkv).astype(BF16)
    q = _dot(nq, wqa)
    q = q * _rep_lanes(cos_t, MLA_HEADS) + _swap_rope_halves(q) * _rep_lanes(sin_t, MLA_HEADS)
    q_ref[...] = (q * MLA_Q_SCALE).astype(BF16)
    k_rot_t = jnp.transpose(k_pe * cos_t + _swap_rope_halves(k_pe) * sin_t)
    k_nope_t = _dot_nt(wk_t, nkv)
    k_rot_rows = k_rot_t[MLA_NOPE:]
    k_ref[0] = jnp.concatenate(
        [part for h in range(MLA_HEADS) for part in (k_nope_t[h * MLA_NOPE:(h + 1) * MLA_NOPE], k_rot_rows)],
        axis=0).astype(BF16)
    v_ref[...] = _dot(nkv, wv).astype(BF16)


def _in_proj_kernel(x_ref, g_ref, w_ref, cos_ref, sin_ref, gq_ref, gkv_ref, wqa_ref, wk_ref, wv_ref,
                    ret_ref, bf_ref, q_ref, k_ref, v_ref):
    h = _rms(x_ref[...], g_ref[...]).astype(BF16)
    lat = jnp.concatenate([_dot(h, w_ref[:, c0:min(c0 + IN_CHUNK, LAT_WIDTH)])
                           for c0 in range(0, LAT_WIDTH, IN_CHUNK)], axis=1)
    _mla_prep(lat, cos_ref[...], sin_ref[...], gq_ref[...], gkv_ref[...], wqa_ref[...],
              wk_ref[...], wv_ref[...], q_ref, k_ref, v_ref)
    col = LAT_WIDTH
    for o_ref, (width, _) in zip((ret_ref, bf_ref), IN_GROUPS):
        for c0 in range(0, width, IN_CHUNK):
            o_ref[:, c0:c0 + IN_CHUNK] = _dot(h, w_ref[:, col + c0:col + c0 + IN_CHUNK]).astype(o_ref.dtype)
        col += width


def _in_proj(x2d, g, w, cos_t, sin_t, gq, gkv, wqa, wk_t, wv, l, b, tm):
    t, d = x2d.shape
    hw = MLA_HEADS * LANES
    nsb = t // b // tm
    full = lambda a: _layer_spec(a, l)
    row = lambda width: pl.BlockSpec((tm, width), lambda i: (i, 0))
    return pl.pallas_call(
        _in_proj_kernel,
        grid=(t // tm,),
        in_specs=[row(d), full(g), full(w), row(LANES), row(LANES), full(gq), full(gkv),
                  full(wqa), full(wk_t), full(wv)],
        out_specs=[row(IN_GROUPS[0][0]), row(IN_GROUPS[1][0]), row(hw),
                   pl.BlockSpec((1, hw, tm), lambda i: (i // nsb, 0, i % nsb)), row(MLA_HEADS * MLA_V)],
        out_shape=[jax.ShapeDtypeStruct((t, IN_GROUPS[0][0]), IN_GROUPS[0][1]),
                   jax.ShapeDtypeStruct((t, IN_GROUPS[1][0]), IN_GROUPS[1][1]),
                   jax.ShapeDtypeStruct((t, hw), BF16), jax.ShapeDtypeStruct((b, hw, t // b), BF16),
                   jax.ShapeDtypeStruct((t, MLA_HEADS * MLA_V), BF16)],
        compiler_params=_params(("parallel",)),
        name="in_proj",
    )(x2d, g, w, cos_t, sin_t, gq, gkv, wqa, wk_t, wv)


def _mla_attn_kernel(q_ref, kt_ref, v_ref, o_ref, m_ref, l_ref, acc_ref, *, tq, tk, nheads):
    i = pl.program_id(2)
    npairs = nheads // 2
    heads = [slice(e * LANES, (e + 1) * LANES) for e in range(nheads)]

    def tiles(ks, nk, r0, nr, masked):
        rows = slice(r0, r0 + nr)
        scores = [_dot(q_ref[0, rows, heads[e]], kt_ref[0, heads[e], pl.ds(ks, nk)]) for e in range(nheads)]
        if masked:
            q_pos = i * tq + r0 + lax.broadcasted_iota(jnp.int32, (nr, nk), 0)
            k_pos = ks + lax.broadcasted_iota(jnp.int32, (nr, nk), 1)
            scores = [jnp.where(k_pos <= q_pos, s, -jnp.inf) for s in scores]
        probs, alphas = [], []
        for e, s in enumerate(scores):
            m_prev = m_ref[e, rows]
            m_next = jnp.maximum(m_prev, jnp.max(s, axis=1, keepdims=True))
            alpha = jnp.exp2(m_prev - m_next)
            p = jnp.exp2(s - _rep_lanes(m_next, nk // LANES))
            partial = functools.reduce(jnp.add, [p[:, c:c + LANES] for c in range(0, nk, LANES)])
            l_ref[e, rows] = alpha * l_ref[e, rows] + partial
            m_ref[e, rows] = m_next
            probs.append(p.astype(BF16))
            alphas.append(alpha)
        for p in range(npairs):
            pv = _dot(jnp.concatenate([probs[2 * p], probs[2 * p + 1]], axis=0), v_ref[0, pl.ds(ks, nk), heads[p]])
            for e in range(2):
                arows = slice(e * tq + r0, e * tq + r0 + nr)
                acc_ref[p, arows] = alphas[2 * p + e] * acc_ref[p, arows] + pv[e * nr:(e + 1) * nr]

    m_ref[...] = jnp.full(m_ref.shape, -jnp.inf, F32)
    l_ref[...] = jnp.zeros(l_ref.shape, F32)
    acc_ref[...] = jnp.zeros(acc_ref.shape, F32)

    def body(j, carry):
        tiles(pl.multiple_of(j * tq, tq), tq, 0, tq, False)
        return carry

    lax.fori_loop(0, i, body, 0)
    diag = pl.multiple_of(i * tq, tq)
    half = tk // 2
    tiles(diag, half, 0, half, True)
    tiles(diag, tk, half, tq - half, True)
    tiles(diag + tk, half, tk, half, True)
    tiles(diag + tk, tk, tk + half, tq - tk - half, True)
    lane = lax.broadcasted_iota(jnp.int32, (tq, LANES), 1)
    denom = [jnp.sum(l_ref[e], axis=1, keepdims=True) for e in range(nheads)]
    for p in range(npairs):
        o_ref[0, :, heads[p]] = jnp.where(
            lane < MLA_V, acc_ref[p, :tq] / denom[2 * p], acc_ref[p, tq:] / denom[2 * p + 1]).astype(o_ref.dtype)


def _mla_attn(q, k_t, v, tq, nheads):
    b, s, _ = q.shape
    tk = tq // 2
    return pl.pallas_call(
        functools.partial(_mla_attn_kernel, tq=tq, tk=tk, nheads=nheads),
        grid=(b, MLA_HEADS // nheads, s // tq),
        in_specs=[pl.BlockSpec((1, tq, nheads * LANES), lambda bi, p, i: (bi, i, p)),
                  pl.BlockSpec((1, nheads * LANES, s), lambda bi, p, i: (bi, p, 0)),
                  pl.BlockSpec((1, s, nheads * MLA_V), lambda bi, p, i: (bi, 0, p))],
        out_specs=pl.BlockSpec((1, tq, nheads * MLA_V), lambda bi, p, i: (bi, i, p)),
        out_shape=jax.ShapeDtypeStruct((b, s, MLA_HEADS * MLA_V), BF16),
        scratch_shapes=[pltpu.VMEM((nheads, tq, LANES), F32), pltpu.VMEM((nheads, tq, LANES), F32),
                        pltpu.VMEM((nheads // 2, 2 * tq, LANES), F32)],
        compiler_params=_params(("parallel", "parallel", "arbitrary")),
        name="mla_attn",
    )(q, k_t, v)


def _sb_attn_kernel(q_ref, k_ref, v_ref, u_ref, o_ref, qm_ref, c_ref, acc_ref, *, tq, nheads):
    i = pl.program_id(2)
    npairs = nheads // 2
    lane = lax.broadcasted_iota(jnp.int32, (tq, LANES), 1)
    zero = jnp.zeros((tq, LANES), BF16)
    for p in range(npairs):
        q_pair = q_ref[0, :, p * LANES:(p + 1) * LANES]
        qm_ref[p, :tq] = jnp.where(lane < SB_DIM, q_pair, zero)
        qm_ref[p, tq:] = jnp.where(lane < SB_DIM, zero, q_pair)

    pairs = [slice(p * LANES, (p + 1) * LANES) for p in range(npairs)]
    has_left = i > 0
    hq = tq // 2
    row_sets = {"all": ((0, 2 * tq),), "upper": ((0, hq), (tq, hq)), "lower": ((hq, hq), (tq + hq, hq))}

    def load(ref, p, sel):
        parts = [ref[p, s:s + n] for s, n in row_sets[sel]]
        return parts[0] if len(parts) == 1 else jnp.concatenate(parts, axis=0)

    def store(ref, p, sel, val):
        off = 0
        for s, n in row_sets[sel]:
            ref[p, s:s + n] = val[off:off + n]
            off += n

    def diag_mask(nk, first_row):
        row = lax.broadcasted_iota(jnp.int32, (2 * hq, nk), 0) % hq + first_row
        return lax.broadcasted_iota(jnp.int32, (2 * hq, nk), 1) < row

    masks = {"diag_upper": diag_mask(hq, 0), "diag_lower": diag_mask(tq, hq)}
    tri = {tq: u_ref[...], hq: jnp.concatenate([u_ref[:hq, :hq], u_ref[tq:tq + hq, :hq]], axis=0)}

    def run(jobs):
        zs = [_dot_nt(load(qm_ref, p, sel), k_ref[0, pl.ds(ks, nk), pairs[p]]) for p, ks, nk, sel, _ in jobs]
        stats = []
        for z, (_, _, _, _, kind) in zip(zs, jobs):
            if kind in masks:
                z = jnp.where(masks[kind], z, SB_MASKED_LOGIT)
            neg_abs = lax.bitcast_convert_type(lax.bitcast_convert_type(z, jnp.uint32) | SIGN_BIT, F32)
            log_beta = jnp.minimum(z, 0.0) - jnp.log(1.0 + jnp.exp2(neg_abs)) * LOG2E
            log_not = log_beta - z
            hi = log_not.astype(BF16)
            lo = (log_not - hi.astype(F32)).astype(BF16)
            stats.append((log_beta, log_not, jnp.concatenate([hi, lo], axis=1)))
        withins = [_dot(hilo, tri[nk]) for (_, _, nk, _, _), (_, _, hilo) in zip(jobs, stats)]
        weights = []
        for (p, _, nk, sel, kind), (log_beta, log_not, _), within in zip(jobs, stats, withins):
            c = load(c_ref, p, sel)
            if kind == "left":
                c = jnp.where(has_left, c, -jnp.inf)
            a = jnp.exp2(log_beta + within + _rep_lanes(c, nk // LANES))
            weights.append(a.astype(BF16))
            store(c_ref, p, sel, c + jnp.sum(log_not, axis=1, keepdims=True))
        for (p, ks, nk, sel, _), a in zip(jobs, weights):
            store(acc_ref, p, sel, load(acc_ref, p, sel) + _dot(a, v_ref[0, pl.ds(ks, nk), pairs[p]]))

    def live():
        return (jnp.max(c_ref[...]) > SB_SKIP_LOG2).astype(jnp.int32)

    c_ref[...] = jnp.zeros(c_ref.shape, F32)
    acc_ref[...] = jnp.zeros(acc_ref.shape, F32)
    here = pl.multiple_of(i * tq, tq)
    left = pl.multiple_of(jnp.maximum(i - 1, 0) * tq, tq)
    run([(p, here, hq, "upper", "diag_upper") for p in range(npairs)]
        + [(p, here, tq, "lower", "diag_lower") for p in range(npairs)]
        + [(p, left, tq, "all", "left") for p in range(npairs)])

    def cond(carry):
        jj, go = carry
        return jnp.logical_and(jj < i, go > 0)

    def body(carry):
        jj, _ = carry
        ks = pl.multiple_of((i - 1 - jj) * tq, tq)
        run([(p, ks, tq, "all", "plain") for p in range(npairs)])
        return jj + 1, live()

    lax.while_loop(cond, body, (jnp.int32(1), live()))
    for p in range(npairs):
        o_ref[0, :, p * LANES:(p + 1) * LANES] = jnp.where(
            lane < SB_DIM, acc_ref[p, :tq], acc_ref[p, tq:]).astype(o_ref.dtype)


def _sb_attn(proj_bf, tq, nheads):
    b, s, _ = proj_bf.shape
    assert tq == SB_SUB
    ngroup = SB_HEADS // nheads
    width = nheads * SB_DIM
    idx = np.arange(SB_SUB)
    tri = (idx[:, None] > idx[None, :]).astype(np.float32)
    u = jnp.asarray(np.concatenate([tri, tri], axis=0), dtype=BF16)
    return pl.pallas_call(
        functools.partial(_sb_attn_kernel, tq=tq, nheads=nheads),
        grid=(b, ngroup, s // tq),
        in_specs=[pl.BlockSpec((1, tq, width), lambda bi, p, i: (bi, i, p)),
                  pl.BlockSpec((1, s, width), lambda bi, p, i: (bi, 0, ngroup + p)),
                  pl.BlockSpec((1, s, width), lambda bi, p, i: (bi, 0, 2 * ngroup + p)),
                  pl.BlockSpec(u.shape, lambda bi, p, i: (0, 0))],
        out_specs=pl.BlockSpec((1, tq, width), lambda bi, p, i: (bi, i, p)),
        out_shape=jax.ShapeDtypeStruct((b, s, SB_HEADS * SB_DIM), BF16),
        scratch_shapes=[pltpu.VMEM((nheads // 2, 2 * tq, LANES), BF16),
                        pltpu.VMEM((nheads // 2, 2 * tq, LANES), F32),
                        pltpu.VMEM((nheads // 2, 2 * tq, LANES), F32)],
        compiler_params=_params(("parallel", "parallel", "arbitrary")),
        name="sb_attn",
    )(proj_bf, proj_bf, proj_bf, u)


def _retention_kernel(q_ref, k_ref, g_ref, v_ref, cos_ref, sin_ref, dec_ref, zeta_ref, xi_ref,
                      cd_ref, bd_ref, avg_ref, gn_ref, o_ref, state_ref, *, nchunk):
    @pl.when(pl.program_id(1) == 0)
    def _():
        state_ref[...] = jnp.zeros(state_ref.shape, F32)

    rows = nchunk * CHUNK
    width = RET_HEADS * RET_DK
    npair = RET_HEADS // 2
    half = RET_DK // 2
    lane_w = lax.broadcasted_iota(jnp.int32, (rows, width), 1)
    first_half = (lane_w % RET_DK) < half
    head0 = (lane_w % LANES) < RET_DK
    cos_t = _rep_lanes(cos_ref[0], width // LANES)
    sin_t = _rep_lanes(sin_ref[0], width // LANES)

    def rope(x):
        swapped = jnp.where(first_half, pltpu.roll(x, width - half, 1), pltpu.roll(x, half, 1))
        return x * cos_t + swapped * sin_t

    q = rope(q_ref[0])
    k = rope(k_ref[0])
    kz = (k * zeta_ref[...]).astype(BF16)
    qb = q.astype(BF16)
    kb = k.astype(BF16)
    zero = jnp.zeros_like(qb)
    q0 = jnp.where(head0, qb, zero)
    q1 = jnp.where(head0, zero, qb)
    v = v_ref[0]
    units = [(c, p) for c in range(nchunk) for p in range(npair)]
    rs = lambda c: slice(c * CHUNK, (c + 1) * CHUNK)
    ls = lambda p: slice(p * LANES, (p + 1) * LANES)
    scores = [_dot_nt(jnp.concatenate([q0[rs(c), ls(p)], q1[rs(c), ls(p)]], axis=0), kb[rs(c), ls(p)])
              for c, p in units]
    scores = [(s * dec_ref[p]).astype(BF16) for s, (c, p) in zip(scores, units)]
    intra = [_dot(s, v[rs(c), ls(p)]) for s, (c, p) in zip(scores, units)]
    kvs = [lax.dot_general(kz[rs(c), ls(p)], v[rs(c), ls(p)], TN_DIMS, preferred_element_type=F32)
           for c, p in units]
    lane = lax.broadcasted_iota(jnp.int32, (CHUNK, LANES), 1)
    chunks = []
    for c in range(nchunk):
        ys = []
        for p in range(npair):
            u = c * npair + p
            state = state_ref[p]
            y_cross = _dot(qb[rs(c), ls(p)], state.astype(BF16)) * xi_ref[:, ls(p)]
            state_ref[p] = state * cd_ref[p] + kvs[u] * bd_ref[...]
            y_intra = jnp.where(lane < RET_DV, intra[u][:CHUNK], intra[u][CHUNK:])
            ys.append(y_intra + y_cross)
        chunks.append(jnp.concatenate(ys, axis=1))
    y = jnp.concatenate(chunks, axis=0)
    avg = avg_ref[...]

    def head_mean(a):
        return jnp.concatenate([_dot_hilo(a[:, c0:c0 + MXU_TILE], avg) for c0 in range(0, width, MXU_TILE)], axis=1)

    d = y - head_mean(y)
    var = head_mean(d * d)
    yn = d * lax.rsqrt(var + EPS) * gn_ref[...]
    g = g_ref[0]
    o_ref[0] = (g * jax.nn.sigmoid(g) * yn).astype(o_ref.dtype)


def _retention_consts(nchunk):
    h = np.arange(RET_HEADS, dtype=np.float64)
    log_gamma = np.log1p(-np.exp2(-5.0 - h))
    idx = np.arange(CHUNK, dtype=np.float64)
    diff = idx[:, None] - idx[None, :]
    dec = np.where(diff[None] >= 0, np.exp(np.maximum(diff, 0.0)[None] * log_gamma[:, None, None]), 0.0)
    npair = RET_HEADS // 2
    dec = dec.reshape(npair, 2 * CHUNK, CHUNK)
    zeta = np.exp((CHUNK - 1 - idx)[:, None] * log_gamma[None, :])
    xi = np.exp((idx + 1.0)[:, None] * log_gamma[None, :])
    zeta_l = np.tile(np.repeat(zeta, RET_DK, axis=1), (nchunk, 1))
    xi_l = np.repeat(xi, RET_DV, axis=1)
    chunk_decay = np.exp(CHUNK * log_gamma)
    head_of = np.arange(LANES) // RET_DK
    bd = (head_of[:, None] == head_of[None, :]).astype(np.float64)
    cd = np.stack([bd * chunk_decay[2 * p + head_of][:, None] for p in range(npair)])
    gh = np.arange(MXU_TILE) // RET_DV
    avg = (gh[:, None] == gh[None, :]).astype(np.float64) / RET_DV
    f = lambda a: jnp.asarray(a, dtype=F32)
    return f(dec), f(zeta_l), f(xi_l), f(cd), f(bd), jnp.asarray(avg, dtype=BF16)


def _retention(proj_ret, proj_bf, cos_t, sin_t, gn, nchunk):
    b, s, _ = proj_ret.shape
    width = RET_HEADS * RET_DK
    rows = nchunk * CHUNK
    dec, zeta_l, xi_l, cd, bd, avg = _retention_consts(nchunk)
    full = lambda a: pl.BlockSpec(a.shape, lambda bi, n: (0,) * a.ndim)
    blk = lambda c: pl.BlockSpec((1, rows, width), lambda bi, n: (bi, n, c))
    tab = pl.BlockSpec((1, rows, LANES), lambda bi, n: (bi, n, 0))
    return pl.pallas_call(
        functools.partial(_retention_kernel, nchunk=nchunk),
        grid=(b, s // rows),
        in_specs=[blk(0), blk(1), blk(2), blk(3), tab, tab,
                  full(dec), full(zeta_l), full(xi_l), full(cd), full(bd), full(avg), full(gn)],
        out_specs=blk(0),
        out_shape=jax.ShapeDtypeStruct((b, s, RET_HEADS * RET_DV), BF16),
        scratch_shapes=[pltpu.VMEM((RET_HEADS // 2, LANES, LANES), F32)],
        compiler_params=_params(("parallel", "arbitrary")),
        name="retention",
    )(proj_ret, proj_ret, proj_ret, proj_bf, cos_t, sin_t, dec, zeta_l, xi_l, cd, bd, avg, gn)


def _merge_kernel(x_ref, g_ref, ya_ref, yb_ref, yc_ref, wg_ref, wb_ref, wo_ref, o_ref):
    x = x_ref[...]
    h = _rms(x, g_ref[...]).astype(BF16)
    merged = None
    for n, y_ref in enumerate((ya_ref, yb_ref, yc_ref)):
        gate = jax.nn.sigmoid(_dot(h, wg_ref[:, n * D_MODEL:(n + 1) * D_MODEL]))
        term = gate * _dot(y_ref[...], wb_ref[n])
        merged = term if merged is None else merged + term
    o_ref[...] = x + _dot(merged.astype(BF16), wo_ref[...])


def _merge(x2d, g, ya, yb, yc, wg, wb, wo, l, tm):
    t = x2d.shape[0]
    full = lambda a: _layer_spec(a, l)
    row = lambda w: pl.BlockSpec((tm, w), lambda i: (i, 0))
    return pl.pallas_call(
        _merge_kernel,
        grid=(t // tm,),
        in_specs=[row(D_MODEL), full(g), row(BRANCH_WIDTH), row(BRANCH_WIDTH), row(BRANCH_WIDTH),
                  full(wg), full(wb), full(wo)],
        out_specs=row(D_MODEL),
        out_shape=jax.ShapeDtypeStruct((t, D_MODEL), F32),
        compiler_params=_params(("parallel",)),
        name="merge",
    )(x2d, g, ya, yb, yc, wg, wb, wo)


MLP_CHUNK = 1024


def _mlp_kernel(x_ref, g_ref, wu_ref, wd_ref, gf_ref, o_ref, *, final_norm):
    x = x_ref[...]
    h = _rms(x, g_ref[...]).astype(BF16)
    acc = None
    for c0 in range(0, D_FF, MLP_CHUNK):
        u = jnp.maximum(_dot(h, wu_ref[:, c0:c0 + MLP_CHUNK]), 0.0)
        part = _dot((u * u).astype(BF16), wd_ref[c0:c0 + MLP_CHUNK, :])
        acc = part if acc is None else acc + part
    y = x + acc
    o_ref[...] = _rms(y, gf_ref[...]) if final_norm else y


def _mlp(x2d, g, wu, wd, gf, l, final_norm, tm):
    t, d = x2d.shape
    full = lambda a: _layer_spec(a, l)
    row = pl.BlockSpec((tm, d), lambda i: (i, 0))
    return pl.pallas_call(
        functools.partial(_mlp_kernel, final_norm=final_norm),
        grid=(t // tm,),
        in_specs=[row, full(g), full(wu), full(wd), pl.BlockSpec(gf.shape, lambda i: (0, 0))],
        out_specs=row,
        out_shape=jax.ShapeDtypeStruct((t, d), F32),
        compiler_params=_params(("parallel",)),
        name="mlp",
    )(x2d, g, wu, wd, gf)


def _pick(n, pref):
    t = min(pref, n)
    while n % t:
        t //= 2
    return t


def _rope_tables(positions):
    pos = positions.astype(F32)[..., None]
    b, s = positions.shape
    lane = np.arange(LANES)

    def cs(dim, freq_of_lane):
        inv_freq = ROPE_BASE ** (-jnp.arange(0, dim, 2, dtype=F32) / dim)
        ang = pos * inv_freq[freq_of_lane]
        return jnp.cos(ang), jnp.sin(ang)

    hm = MLA_ROPE // 2
    is_rope = (lane >= MLA_NOPE) & (lane < MLA_NOPE + MLA_ROPE)
    cm, sm = cs(MLA_ROPE, np.where(is_rope, (lane - MLA_NOPE) % hm, 0))
    sign_m = np.where(lane < MLA_NOPE + hm, -1.0, 1.0).astype(np.float32)
    mla_cos = jnp.where(lane < MLA_NOPE, 1.0, jnp.where(is_rope, cm, 0.0)).reshape(b * s, LANES)
    mla_sin = jnp.where(is_rope, sign_m * sm, 0.0).reshape(b * s, LANES)
    hr = RET_DK // 2
    ret_cos, sr = cs(RET_DK, lane % hr)
    ret_sin = np.where(lane % RET_DK < hr, -1.0, 1.0).astype(np.float32) * sr
    return mla_cos, mla_sin, ret_cos, ret_sin


def _prep_weights(w_in, mla_w_uq, mla_w_ukv, w_branch, w_out, w_up, w_down):
    depth, d, _ = w_in.shape
    offs = [int(o) for o in np.cumsum((0, MLA_Q_RANK, MLA_KV_RANK, MLA_ROPE) + (BRANCH_WIDTH,) * 7
                                      + (N_BRANCH * D_MODEL,))]
    col_scale = np.ones(offs[-1], np.float32)
    col_scale[offs[3]:offs[4]] = SB_DIM ** -0.5 * LOG2E
    col_scale[offs[7]:offs[8]] = RET_DK ** -0.5
    w = (w_in * col_scale).astype(BF16)
    seg = lambda n: w[:, :, offs[n]:offs[n + 1]]
    c_q, c_kv, k_pe, sb_q, sb_k, sb_v, r_q, r_k, r_v, r_g, gates = (seg(n) for n in range(11))
    zeros = lambda *shape: jnp.zeros(shape, BF16)
    pad = LANES - MLA_NOPE - MLA_ROPE
    w_proj = jnp.concatenate(
        [c_q, c_kv,
         zeros(depth, d, MLA_NOPE), k_pe, zeros(depth, d, pad),
         r_q, r_k, r_g,
         sb_q, sb_k, sb_v, r_v], axis=-1)

    uq = mla_w_uq.astype(BF16).reshape(depth, MLA_Q_RANK, MLA_HEADS, MLA_NOPE + MLA_ROPE)
    zq = lambda width: jnp.zeros((depth, MLA_Q_RANK, MLA_HEADS, width), BF16)
    nope, rope = uq[..., :MLA_NOPE], uq[..., MLA_NOPE:]
    wqa = jnp.concatenate([nope, rope, zq(pad)], axis=-1)
    ukv = mla_w_ukv.astype(BF16).reshape(depth, MLA_KV_RANK, MLA_HEADS, MLA_NOPE + MLA_V)
    wk = ukv[..., :MLA_NOPE]
    wv = ukv[..., MLA_NOPE:]
    flat = lambda a: a.reshape(depth, a.shape[1], -1)
    bf = lambda a: a.astype(BF16)
    return dict(w_in=w_proj, w_gate=gates,
                wqa=flat(wqa), wk_t=jnp.swapaxes(flat(wk), 1, 2), wv=flat(wv),
                wb=bf(w_branch), wo=bf(w_out), wu=bf(w_up), wd=bf(w_down))


def kernel(x, positions, norm_mix_g, w_in, mla_q_norm_g, mla_w_uq, mla_kv_norm_g, mla_w_ukv, ret_norm_g, w_branch, w_out, norm_mlp_g, w_up, w_down, final_norm_g):
    b, s, d = x.shape
    depth = w_in.shape[0]
    t = b * s
    assert d == D_MODEL and s % CHUNK == 0
    tm_row = _pick(t, 512)
    t_mla = _pick(s, 1024)
    t_sb = max(_pick(s, 256), SB_SUB)
    w = _prep_weights(w_in, mla_w_uq, mla_w_ukv, w_branch, w_out, w_up, w_down)
    mla_cos, mla_sin, ret_cos, ret_sin = _rope_tables(positions)
    vec = lambda a: a.reshape(1, -1)
    gains = lambda a: a.reshape(depth, 1, -1)
    g_mix, g_q, g_kv, g_mlp = gains(norm_mix_g), gains(mla_q_norm_g), gains(mla_kv_norm_g), gains(norm_mlp_g)

    x2d = x.reshape(t, d)
    for l in range(depth):
        proj_ret, proj_bf, q_a, k_t, v_a = _in_proj(x2d, g_mix, w["w_in"], mla_cos, mla_sin, g_q, g_kv,
                                                    w["wqa"], w["wk_t"], w["wv"], l, b, tm_row)
        proj_bf = proj_bf.reshape(b, s, -1)

        y_a = _mla_attn(q_a.reshape(b, s, -1), k_t, v_a.reshape(b, s, -1), t_mla, 4)
        y_b = _sb_attn(proj_bf, t_sb, 8)
        y_c = _retention(proj_ret.reshape(b, s, -1), proj_bf, ret_cos, ret_sin, vec(ret_norm_g[l]),
                         _pick(s // CHUNK, 8))

        x2d = _merge(x2d, g_mix, y_a.reshape(t, -1), y_b.reshape(t, -1), y_c.reshape(t, -1),
                     w["w_gate"], w["wb"], w["wo"], l, tm_row)
        x2d = _mlp(x2d, g_mlp, w["wu"], w["wd"], vec(final_norm_g), l, l == depth - 1, tm_row)
    return x2d.reshape(b, s, d)
```

```python
import functools

import numpy as np
import jax
import jax.numpy as jnp
from jax import lax
from jax.experimental import pallas as pl
from jax.experimental.pallas import tpu as pltpu

F32 = jnp.float32
BF16 = jnp.bfloat16

D_MODEL = 1024
EPS = 1e-6
ROPE_BASE = 10000.0
CHUNK = 128

MLA_HEADS = 8
MLA_NOPE = 64
MLA_ROPE = 32
MLA_V = 64
MLA_Q_RANK = 384
MLA_KV_RANK = 256
SB_HEADS = 8
SB_DIM = 64
RET_HEADS = 8
RET_DK = 64
RET_DV = 64
BRANCH_WIDTH = 512
N_BRANCH = 3
D_FF = 4 * D_MODEL

LANES = 128
MXU_TILE = 256
LAT_WIDTH = MLA_Q_RANK + MLA_KV_RANK + LANES
VMEM_LIMIT = 56 * 1024 * 1024
LOG2E = 1.4426950408889634
MLA_Q_SCALE = (MLA_NOPE + MLA_ROPE) ** -0.5 * LOG2E
SIGN_BIT = np.uint32(0x80000000)
SB_MASKED_LOGIT = -1e30
SB_SUB = MXU_TILE
SB_LEFT_ROWS = 160
SB_SKIP_LOG2 = -135.0

NT_DIMS = (((1,), (1,)), ((), ()))
TN_DIMS = (((0,), (0,)), ((), ()))


def _dot(a, b):
    return jnp.dot(a, b, preferred_element_type=F32)


def _dot_nt(a, b):
    return lax.dot_general(a, b, NT_DIMS, preferred_element_type=F32)


def _dot_hilo(x, m_bf16):
    hi = x.astype(BF16)
    lo = (x - hi.astype(F32)).astype(BF16)
    return _dot(hi, m_bf16) + _dot(lo, m_bf16)


def _rep_lanes(x, n):
    return x if n == 1 else jnp.concatenate([x] * n, axis=1)


def _rms(x, g):
    return x * lax.rsqrt(jnp.mean(x * x, axis=-1, keepdims=True) + EPS) * g


def _layer_spec(stacked, l):
    nd = stacked.ndim - 1
    return pl.BlockSpec((None,) + stacked.shape[1:], lambda *_: (l,) + (0,) * nd, pipeline_mode=pl.Buffered(1))


def _params(sem):
    return pltpu.CompilerParams(dimension_semantics=sem, vmem_limit_bytes=VMEM_LIMIT)


IN_GROUPS = ((3 * BRANCH_WIDTH, F32),
             (4 * BRANCH_WIDTH, BF16))
IN_CHUNK = 512


def _swap_rope_halves(x):
    width = x.shape[1]
    half = MLA_ROPE // 2
    lane = lax.broadcasted_iota(jnp.int32, x.shape, 1) % LANES
    return jnp.where(lane < MLA_NOPE + half, pltpu.roll(x, width - half, 1), pltpu.roll(x, half, 1))


def _mla_prep(lat, cos_t, sin_t, gq, gkv, wqa, wk_t, wv, q_ref, k_ref, v_ref):
    c_q = lat[:, :MLA_Q_RANK]
    c_kv = lat[:, MLA_Q_RANK:MLA_Q_RANK + MLA_KV_RANK]
    k_pe = lat[:, MLA_Q_RANK + MLA_KV_RANK:]
    nq = _rms(c_q, gq).astype(BF16)
    nkv = _rms(c_kv, gkv).astype(BF16)
    q = _dot(nq, wqa)
    q = q * _rep_lanes(cos_t, MLA_HEADS) + _swap_rope_halves(q) * _rep_lanes(sin_t, MLA_HEADS)
    q_ref[...] = (q * MLA_Q_SCALE).astype(BF16)
    k_rot_t = jnp.transpose(k_pe * cos_t + _swap_rope_halves(k_pe) * sin_t)
    k_nope_t = _dot_nt(wk_t, nkv)
    k_rot_rows = k_rot_t[MLA_NOPE:]
    k_ref[0] = jnp.concatenate(
        [part for h in range(MLA_HEADS) for part in (k_nope_t[h * MLA_NOPE:(h + 1) * MLA_NOPE], k_rot_rows)],
        axis=0).astype(BF16)
    v_ref[...] = _dot(nkv, wv).astype(BF16)


def _in_proj_kernel(x_ref, g_ref, w_ref, cos_ref, sin_ref, gq_ref, gkv_ref, wqa_ref, wk_ref, wv_ref,
                    ret_ref, bf_ref, q_ref, k_ref, v_ref):
    h = _rms(x_ref[...], g_ref[...]).astype(BF16)
    lat = jnp.concatenate([_dot(h, w_ref[:, c0:min(c0 + IN_CHUNK, LAT_WIDTH)])
                           for c0 in range(0, LAT_WIDTH, IN_CHUNK)], axis=1)
    _mla_prep(lat, cos_ref[...], sin_ref[...], gq_ref[...], gkv_ref[...], wqa_ref[...],
              wk_ref[...], wv_ref[...], q_ref, k_ref, v_ref)
    col = LAT_WIDTH
    for o_ref, (width, _) in zip((ret_ref, bf_ref), IN_GROUPS):
        for c0 in range(0, width, IN_CHUNK):
            o_ref[:, c0:c0 + IN_CHUNK] = _dot(h, w_ref[:, col + c0:col + c0 + IN_CHUNK]).astype(o_ref.dtype)
        col += width


def _in_proj(x2d, g, w, cos_t, sin_t, gq, gkv, wqa, wk_t, wv, l, b, tm):
    t, d = x2d.shape
    hw = MLA_HEADS * LANES
    nsb = t // b // tm
    full = lambda a: _layer_spec(a, l)
    row = lambda width: pl.BlockSpec((tm, width), lambda i: (i, 0))
    return pl.pallas_call(
        _in_proj_kernel,
        grid=(t // tm,),
        in_specs=[row(d), full(g), full(w), row(LANES), row(LANES), full(gq), full(gkv),
                  full(wqa), full(wk_t), full(wv)],
        out_specs=[row(IN_GROUPS[0][0]), row(IN_GROUPS[1][0]), row(hw),
                   pl.BlockSpec((1, hw, tm), lambda i: (i // nsb, 0, i % nsb)), row(MLA_HEADS * MLA_V)],
        out_shape=[jax.ShapeDtypeStruct((t, IN_GROUPS[0][0]), IN_GROUPS[0][1]),
                   jax.ShapeDtypeStruct((t, IN_GROUPS[1][0]), IN_GROUPS[1][1]),
                   jax.ShapeDtypeStruct((t, hw), BF16), jax.ShapeDtypeStruct((b, hw, t // b), BF16),
                   jax.ShapeDtypeStruct((t, MLA_HEADS * MLA_V), BF16)],
        compiler_params=_params(("parallel",)),
        name="in_proj",
    )(x2d, g, w, cos_t, sin_t, gq, gkv, wqa, wk_t, wv)


def _mla_attn_kernel(q_ref, kt_ref, v_ref, o_ref, m_ref, l_ref, acc_ref, *, tq, tk, nheads):
    i = pl.program_id(2)
    npairs = nheads // 2
    heads = [slice(e * LANES, (e + 1) * LANES) for e in range(nheads)]

    def tiles(ks, nk, r0, nr, masked):
        rows = slice(r0, r0 + nr)
        scores = [_dot(q_ref[0, rows, heads[e]], kt_ref[0, heads[e], pl.ds(ks, nk)]) for e in range(nheads)]
        if masked:
            q_pos = i * tq + r0 + lax.broadcasted_iota(jnp.int32, (nr, nk), 0)
            k_pos = ks + lax.broadcasted_iota(jnp.int32, (nr, nk), 1)
            scores = [jnp.where(k_pos <= q_pos, s, -jnp.inf) for s in scores]
        probs, alphas = [], []
        for e, s in enumerate(scores):
            m_prev = m_ref[e, rows]
            m_next = jnp.maximum(m_prev, jnp.max(s, axis=1, keepdims=True))
            alpha = jnp.exp2(m_prev - m_next)
            p = jnp.exp2(s - _rep_lanes(m_next, nk // LANES))
            partial = functools.reduce(jnp.add, [p[:, c:c + LANES] for c in range(0, nk, LANES)])
            l_ref[e, rows] = alpha * l_ref[e, rows] + partial
            m_ref[e, rows] = m_next
            probs.append(p.astype(BF16))
            alphas.append(alpha)
        for p in range(npairs):
            pv = _dot(jnp.concatenate([probs[2 * p], probs[2 * p + 1]], axis=0), v_ref[0, pl.ds(ks, nk), heads[p]])
            for e in range(2):
                arows = slice(e * tq + r0, e * tq + r0 + nr)
                acc_ref[p, arows] = alphas[2 * p + e] * acc_ref[p, arows] + pv[e * nr:(e + 1) * nr]

    m_ref[...] = jnp.full(m_ref.shape, -jnp.inf, F32)
    l_ref[...] = jnp.zeros(l_ref.shape, F32)
    acc_ref[...] = jnp.zeros(acc_ref.shape, F32)

    def body(j, carry):
        tiles(pl.multiple_of(j * tq, tq), tq, 0, tq, False)
        return carry

    lax.fori_loop(0, i, body, 0)
    diag = pl.multiple_of(i * tq, tq)
    half = tk // 2
    tiles(diag, half, 0, half, True)
    tiles(diag, tk, half, tq - half, True)
    tiles(diag + tk, half, tk, half, True)
    tiles(diag + tk, tk, tk + half, tq - tk - half, True)
    lane = lax.broadcasted_iota(jnp.int32, (tq, LANES), 1)
    denom = [jnp.sum(l_ref[e], axis=1, keepdims=True) for e in range(nheads)]
    for p in range(npairs):
        o_ref[0, :, heads[p]] = jnp.where(
            lane < MLA_V, acc_ref[p, :tq] / denom[2 * p], acc_ref[p, tq:] / denom[2 * p + 1]).astype(o_ref.dtype)


def _mla_attn(q, k_t, v, tq, nheads):
    b, s, _ = q.shape
    tk = tq // 2
    return pl.pallas_call(
        functools.partial(_mla_attn_kernel, tq=tq, tk=tk, nheads=nheads),
        grid=(b, MLA_HEADS // nheads, s // tq),
        in_specs=[pl.BlockSpec((1, tq, nheads * LANES), lambda bi, p, i: (bi, i, p)),
                  pl.BlockSpec((1, nheads * LANES, s), lambda bi, p, i: (bi, p, 0)),
                  pl.BlockSpec((1, s, nheads * MLA_V), lambda bi, p, i: (bi, 0, p))],
        out_specs=pl.BlockSpec((1, tq, nheads * MLA_V), lambda bi, p, i: (bi, i, p)),
        out_shape=jax.ShapeDtypeStruct((b, s, MLA_HEADS * MLA_V), BF16),
        scratch_shapes=[pltpu.VMEM((nheads, tq, LANES), F32), pltpu.VMEM((nheads, tq, LANES), F32),
                        pltpu.VMEM((nheads // 2, 2 * tq, LANES), F32)],
        compiler_params=_params(("parallel", "parallel", "arbitrary")),
        name="mla_attn",
    )(q, k_t, v)


def _sb_attn_kernel(q_ref, k_ref, v_ref, u_ref, o_ref, qm_ref, c_ref, acc_ref, *, tq, nheads):
    i = pl.program_id(2)
    npairs = nheads // 2
    lane = lax.broadcasted_iota(jnp.int32, (tq, LANES), 1)
    zero = jnp.zeros((tq, LANES), BF16)
    for p in range(npairs):
        q_pair = q_ref[0, :, p * LANES:(p + 1) * LANES]
        qm_ref[p, :tq] = jnp.where(lane < SB_DIM, q_pair, zero)
        qm_ref[p, tq:] = jnp.where(lane < SB_DIM, zero, q_pair)

    pairs = [slice(p * LANES, (p + 1) * LANES) for p in range(npairs)]
    has_left = i > 0
    hq = tq // 2
    row_sets = {"all": ((0, 2 * tq),), "upper": ((0, hq), (tq, hq)), "lower": ((hq, hq), (tq + hq, hq)),
                "head": ((0, SB_LEFT_ROWS), (tq, SB_LEFT_ROWS)),
                "tail": ((SB_LEFT_ROWS, tq - SB_LEFT_ROWS), (tq + SB_LEFT_ROWS, tq - SB_LEFT_ROWS))}

    def load(ref, p, sel):
        parts = [ref[p, s:s + n] for s, n in row_sets[sel]]
        return parts[0] if len(parts) == 1 else jnp.concatenate(parts, axis=0)

    def store(ref, p, sel, val):
        off = 0
        for s, n in row_sets[sel]:
            ref[p, s:s + n] = val[off:off + n]
            off += n

    def diag_mask(nk, first_row):
        row = lax.broadcasted_iota(jnp.int32, (2 * hq, nk), 0) % hq + first_row
        return lax.broadcasted_iota(jnp.int32, (2 * hq, nk), 1) < row

    masks = {"diag_upper": diag_mask(hq, 0), "diag_lower": diag_mask(tq, hq)}
    tri = {tq: u_ref[...], hq: jnp.concatenate([u_ref[:hq, :hq], u_ref[tq:tq + hq, :hq]], axis=0)}

    def run(jobs):
        zs = [_dot_nt(load(qm_ref, p, sel), k_ref[0, pl.ds(ks, nk), pairs[p]]) for p, ks, nk, sel, _ in jobs]
        stats = []
        for z, (_, _, _, _, kind) in zip(zs, jobs):
            if kind in masks:
                z = jnp.where(masks[kind], z, SB_MASKED_LOGIT)
            neg_abs = lax.bitcast_convert_type(lax.bitcast_convert_type(z, jnp.uint32) | SIGN_BIT, F32)
            log_beta = jnp.minimum(z, 0.0) - jnp.log(1.0 + jnp.exp2(neg_abs)) * LOG2E
            log_not = log_beta - z
            hi = log_not.astype(BF16)
            lo = (log_not - hi.astype(F32)).astype(BF16)
            stats.append((log_beta, log_not, jnp.concatenate([hi, lo], axis=1)))
        withins = [_dot(hilo, tri[nk]) for (_, _, nk, _, _), (_, _, hilo) in zip(jobs, stats)]
        weights = []
        for (p, _, nk, sel, kind), (log_beta, log_not, _), within in zip(jobs, stats, withins):
            c = load(c_ref, p, sel)
            if kind == "left":
                c = jnp.where(has_left, c, -jnp.inf)
            a = jnp.exp2(log_beta + within + _rep_lanes(c, nk // LANES))
            weights.append(a.astype(BF16))
            store(c_ref, p, sel, c + jnp.sum(log_not, axis=1, keepdims=True))
        for (p, ks, nk, sel, _), a in zip(jobs, weights):
            store(acc_ref, p, sel, load(acc_ref, p, sel) + _dot(a, v_ref[0, pl.ds(ks, nk), pairs[p]]))

    def live(sel="all"):
        worst = functools.reduce(jnp.maximum, [jnp.max(load(c_ref, p, sel)) for p in range(npairs)])
        return (worst > SB_SKIP_LOG2).astype(jnp.int32)

    c_ref[...] = jnp.zeros(c_ref.shape, F32)
    acc_ref[...] = jnp.zeros(acc_ref.shape, F32)
    here = pl.multiple_of(i * tq, tq)
    left = pl.multiple_of(jnp.maximum(i - 1, 0) * tq, tq)
    run([(p, here, hq, "upper", "diag_upper") for p in range(npairs)]
        + [(p, here, tq, "lower", "diag_lower") for p in range(npairs)]
        + [(p, left, tq, "head", "left") for p in range(npairs)])

    @pl.when(jnp.logical_and(has_left, live("tail") > 0))
    def _():
        run([(p, left, tq, "tail", "plain") for p in range(npairs)])

    def cond(carry):
        jj, go = carry
        return jnp.logical_and(jj < i, go > 0)

    def body(carry):
        jj, _ = carry
        ks = pl.multiple_of((i - 1 - jj) * tq, tq)
        run([(p, ks, tq, "all", "plain") for p in range(npairs)])
        return jj + 1, live()

    lax.while_loop(cond, body, (jnp.int32(1), live()))
    for p in range(npairs):
        o_ref[0, :, p * LANES:(p + 1) * LANES] = jnp.where(
            lane < SB_DIM, acc_ref[p, :tq], acc_ref[p, tq:]).astype(o_ref.dtype)


def _sb_attn(proj_bf, tq, nheads):
    b, s, _ = proj_bf.shape
    assert tq == SB_SUB
    ngroup = SB_HEADS // nheads
    width = nheads * SB_DIM
    idx = np.arange(SB_SUB)
    tri = (idx[:, None] > idx[None, :]).astype(np.float32)
    u = jnp.asarray(np.concatenate([tri, tri], axis=0), dtype=BF16)
    return pl.pallas_call(
        functools.partial(_sb_attn_kernel, tq=tq, nheads=nheads),
        grid=(b, ngroup, s // tq),
        in_specs=[pl.BlockSpec((1, tq, width), lambda bi, p, i: (bi, i, p)),
                  pl.BlockSpec((1, s, width), lambda bi, p, i: (bi, 0, ngroup + p)),
                  pl.BlockSpec((1, s, width), lambda bi, p, i: (bi, 0, 2 * ngroup + p)),
                  pl.BlockSpec(u.shape, lambda bi, p, i: (0, 0))],
        out_specs=pl.BlockSpec((1, tq, width), lambda bi, p, i: (bi, i, p)),
        out_shape=jax.ShapeDtypeStruct((b, s, SB_HEADS * SB_DIM), BF16),
        scratch_shapes=[pltpu.VMEM((nheads // 2, 2 * tq, LANES), BF16),
                        pltpu.VMEM((nheads // 2, 2 * tq, LANES), F32),
                        pltpu.VMEM((nheads // 2, 2 * tq, LANES), F32)],
        compiler_params=_params(("parallel", "parallel", "arbitrary")),
        name="sb_attn",
    )(proj_bf, proj_bf, proj_bf, u)


def _retention_kernel(q_ref, k_ref, g_ref, v_ref, cos_ref, sin_ref, dec_ref, zeta_ref, xi_ref,
                      cd_ref, bd_ref, avg_ref, gn_ref, o_ref, state_ref, *, nchunk):
    @pl.when(pl.program_id(1) == 0)
    def _():
        state_ref[...] = jnp.zeros(state_ref.shape, F32)

    rows = nchunk * CHUNK
    width = RET_HEADS * RET_DK
    npair = RET_HEADS // 2
    half = RET_DK // 2
    lane_w = lax.broadcasted_iota(jnp.int32, (rows, width), 1)
    first_half = (lane_w % RET_DK) < half
    head0 = (lane_w % LANES) < RET_DK
    cos_t = _rep_lanes(cos_ref[0], width // LANES)
    sin_t = _rep_lanes(sin_ref[0], width // LANES)

    def rope(x):
        swapped = jnp.where(first_half, pltpu.roll(x, width - half, 1), pltpu.roll(x, half, 1))
        return x * cos_t + swapped * sin_t

    q = rope(q_ref[0])
    k = rope(k_ref[0])
    kz = (k * zeta_ref[...]).astype(BF16)
    qb = q.astype(BF16)
    kb = k.astype(BF16)
    zero = jnp.zeros_like(qb)
    q0 = jnp.where(head0, qb, zero)
    q1 = jnp.where(head0, zero, qb)
    v = v_ref[0]
    units = [(c, p) for c in range(nchunk) for p in range(npair)]
    rs = lambda c: slice(c * CHUNK, (c + 1) * CHUNK)
    ls = lambda p: slice(p * LANES, (p + 1) * LANES)
    scores = [_dot_nt(jnp.concatenate([q0[rs(c), ls(p)], q1[rs(c), ls(p)]], axis=0), kb[rs(c), ls(p)])
              for c, p in units]
    scores = [(s * dec_ref[p]).astype(BF16) for s, (c, p) in zip(scores, units)]
    intra = [_dot(s, v[rs(c), ls(p)]) for s, (c, p) in zip(scores, units)]
    kvs = [lax.dot_general(kz[rs(c), ls(p)], v[rs(c), ls(p)], TN_DIMS, preferred_element_type=F32)
           for c, p in units]
    lane = lax.broadcasted_iota(jnp.int32, (CHUNK, LANES), 1)
    chunks = []
    for c in range(nchunk):
        ys = []
        for p in range(npair):
            u = c * npair + p
            state = state_ref[p]
            y_cross = _dot(qb[rs(c), ls(p)], state.astype(BF16)) * xi_ref[:, ls(p)]
            state_ref[p] = state * cd_ref[p] + kvs[u] * bd_ref[...]
            y_intra = jnp.where(lane < RET_DV, intra[u][:CHUNK], intra[u][CHUNK:])
            ys.append(y_intra + y_cross)
        chunks.append(jnp.concatenate(ys, axis=1))
    y = jnp.concatenate(chunks, axis=0)
    avg = avg_ref[...]

    def head_mean(a):
        return jnp.concatenate([_dot_hilo(a[:, c0:c0 + MXU_TILE], avg) for c0 in range(0, width, MXU_TILE)], axis=1)

    d = y - head_mean(y)
    var = head_mean(d * d)
    yn = d * lax.rsqrt(var + EPS) * gn_ref[...]
    g = g_ref[0]
    o_ref[0] = (g * jax.nn.sigmoid(g) * yn).astype(o_ref.dtype)


def _retention_consts(nchunk):
    h = np.arange(RET_HEADS, dtype=np.float64)
    log_gamma = np.log1p(-np.exp2(-5.0 - h))
    idx = np.arange(CHUNK, dtype=np.float64)
    diff = idx[:, None] - idx[None, :]
    dec = np.where(diff[None] >= 0, np.exp(np.maximum(diff, 0.0)[None] * log_gamma[:, None, None]), 0.0)
    npair = RET_HEADS // 2
    dec = dec.reshape(npair, 2 * CHUNK, CHUNK)
    zeta = np.exp((CHUNK - 1 - idx)[:, None] * log_gamma[None, :])
    xi = np.exp((idx + 1.0)[:, None] * log_gamma[None, :])
    zeta_l = np.tile(np.repeat(zeta, RET_DK, axis=1), (nchunk, 1))
    xi_l = np.repeat(xi, RET_DV, axis=1)
    chunk_decay = np.exp(CHUNK * log_gamma)
    head_of = np.arange(LANES) // RET_DK
    bd = (head_of[:, None] == head_of[None, :]).astype(np.float64)
    cd = np.stack([bd * chunk_decay[2 * p + head_of][:, None] for p in range(npair)])
    gh = np.arange(MXU_TILE) // RET_DV
    avg = (gh[:, None] == gh[None, :]).astype(np.float64) / RET_DV
    f = lambda a: jnp.asarray(a, dtype=F32)
    return f(dec), f(zeta_l), f(xi_l), f(cd), f(bd), jnp.asarray(avg, dtype=BF16)


def _retention(proj_ret, proj_bf, cos_t, sin_t, gn, nchunk):
    b, s, _ = proj_ret.shape
    width = RET_HEADS * RET_DK
    rows = nchunk * CHUNK
    dec, zeta_l, xi_l, cd, bd, avg = _retention_consts(nchunk)
    full = lambda a: pl.BlockSpec(a.shape, lambda bi, n: (0,) * a.ndim)
    blk = lambda c: pl.BlockSpec((1, rows, width), lambda bi, n: (bi, n, c))
    tab = pl.BlockSpec((1, rows, LANES), lambda bi, n: (bi, n, 0))
    return pl.pallas_call(
        functools.partial(_retention_kernel, nchunk=nchunk),
        grid=(b, s // rows),
        in_specs=[blk(0), blk(1), blk(2), blk(3), tab, tab,
                  full(dec), full(zeta_l), full(xi_l), full(cd), full(bd), full(avg), full(gn)],
        out_specs=blk(0),
        out_shape=jax.ShapeDtypeStruct((b, s, RET_HEADS * RET_DV), BF16),
        scratch_shapes=[pltpu.VMEM((RET_HEADS // 2, LANES, LANES), F32)],
        compiler_params=_params(("parallel", "arbitrary")),
        name="retention",
    )(proj_ret, proj_ret, proj_ret, proj_bf, cos_t, sin_t, dec, zeta_l, xi_l, cd, bd, avg, gn)


def _merge_kernel(x_ref, g_ref, ya_ref, yb_ref, yc_ref, wg_ref, wb_ref, wo_ref, o_ref):
    x = x_ref[...]
    h = _rms(x, g_ref[...]).astype(BF16)
    merged = None
    for n, y_ref in enumerate((ya_ref, yb_ref, yc_ref)):
        gate = jax.nn.sigmoid(_dot(h, wg_ref[:, n * D_MODEL:(n + 1) * D_MODEL]))
        term = gate * _dot(y_ref[...], wb_ref[n])
        merged = term if merged is None else merged + term
    o_ref[...] = x + _dot(merged.astype(BF16), wo_ref[...])


def _merge(x2d, g, ya, yb, yc, wg, wb, wo, l, tm):
    t = x2d.shape[0]
    full = lambda a: _layer_spec(a, l)
    row = lambda w: pl.BlockSpec((tm, w), lambda i: (i, 0))
    return pl.pallas_call(
        _merge_kernel,
        grid=(t // tm,),
        in_specs=[row(D_MODEL), full(g), row(BRANCH_WIDTH), row(BRANCH_WIDTH), row(BRANCH_WIDTH),
                  full(wg), full(wb), full(wo)],
        out_specs=row(D_MODEL),
        out_shape=jax.ShapeDtypeStruct((t, D_MODEL), F32),
        compiler_params=_params(("parallel",)),
        name="merge",
    )(x2d, g, ya, yb, yc, wg, wb, wo)


MLP_CHUNK = 1024


def _mlp_kernel(x_ref, g_ref, wu_ref, wd_ref, gf_ref, o_ref, *, final_norm):
    x = x_ref[...]
    h = _rms(x, g_ref[...]).astype(BF16)
    acc = None
    for c0 in range(0, D_FF, MLP_CHUNK):
        u = jnp.maximum(_dot(h, wu_ref[:, c0:c0 + MLP_CHUNK]), 0.0)
        part = _dot((u * u).astype(BF16), wd_ref[c0:c0 + MLP_CHUNK, :])
        acc = part if acc is None else acc + part
    y = x + acc
    o_ref[...] = _rms(y, gf_ref[...]) if final_norm else y


def _mlp(x2d, g, wu, wd, gf, l, final_norm, tm):
    t, d = x2d.shape
    full = lambda a: _layer_spec(a, l)
    row = pl.BlockSpec((tm, d), lambda i: (i, 0))
    return pl.pallas_call(
        functools.partial(_mlp_kernel, final_norm=final_norm),
        grid=(t // tm,),
        in_specs=[row, full(g), full(wu), full(wd), pl.BlockSpec(gf.shape, lambda i: (0, 0))],
        out_specs=row,
        out_shape=jax.ShapeDtypeStruct((t, d), F32),
        compiler_params=_params(("parallel",)),
        name="mlp",
    )(x2d, g, wu, wd, gf)


def _pick(n, pref):
    t = min(pref, n)
    while n % t:
        t //= 2
    return t


def _rope_tables(positions):
    pos = positions.astype(F32)[..., None]
    b, s = positions.shape
    lane = np.arange(LANES)

    def cs(dim, freq_of_lane):
        inv_freq = ROPE_BASE ** (-jnp.arange(0, dim, 2, dtype=F32) / dim)
        ang = pos * inv_freq[freq_of_lane]
        return jnp.cos(ang), jnp.sin(ang)

    hm = MLA_ROPE // 2
    is_rope = (lane >= MLA_NOPE) & (lane < MLA_NOPE + MLA_ROPE)
    cm, sm = cs(MLA_ROPE, np.where(is_rope, (lane - MLA_NOPE) % hm, 0))
    sign_m = np.where(lane < MLA_NOPE + hm, -1.0, 1.0).astype(np.float32)
    mla_cos = jnp.where(lane < MLA_NOPE, 1.0, jnp.where(is_rope, cm, 0.0)).reshape(b * s, LANES)
    mla_sin = jnp.where(is_rope, sign_m * sm, 0.0).reshape(b * s, LANES)
    hr = RET_DK // 2
    ret_cos, sr = cs(RET_DK, lane % hr)
    ret_sin = np.where(lane % RET_DK < hr, -1.0, 1.0).astype(np.float32) * sr
    return mla_cos, mla_sin, ret_cos, ret_sin


def _prep_weights(w_in, mla_w_uq, mla_w_ukv, w_branch, w_out, w_up, w_down):
    depth, d, _ = w_in.shape
    offs = [int(o) for o in np.cumsum((0, MLA_Q_RANK, MLA_KV_RANK, MLA_ROPE) + (BRANCH_WIDTH,) * 7
                                      + (N_BRANCH * D_MODEL,))]
    col_scale = np.ones(offs[-1], np.float32)
    col_scale[offs[3]:offs[4]] = SB_DIM ** -0.5 * LOG2E
    col_scale[offs[7]:offs[8]] = RET_DK ** -0.5
    w = (w_in * col_scale).astype(BF16)
    seg = lambda n: w[:, :, offs[n]:offs[n + 1]]
    c_q, c_kv, k_pe, sb_q, sb_k, sb_v, r_q, r_k, r_v, r_g, gates = (seg(n) for n in range(11))
    zeros = lambda *shape: jnp.zeros(shape, BF16)
    pad = LANES - MLA_NOPE - MLA_ROPE
    w_proj = jnp.concatenate(
        [c_q, c_kv,
         zeros(depth, d, MLA_NOPE), k_pe, zeros(depth, d, pad),
         r_q, r_k, r_g,
         sb_q, sb_k, sb_v, r_v], axis=-1)

    uq = mla_w_uq.astype(BF16).reshape(depth, MLA_Q_RANK, MLA_HEADS, MLA_NOPE + MLA_ROPE)
    zq = lambda width: jnp.zeros((depth, MLA_Q_RANK, MLA_HEADS, width), BF16)
    nope, rope = uq[..., :MLA_NOPE], uq[..., MLA_NOPE:]
    wqa = jnp.concatenate([nope, rope, zq(pad)], axis=-1)
    ukv = mla_w_ukv.astype(BF16).reshape(depth, MLA_KV_RANK, MLA_HEADS, MLA_NOPE + MLA_V)
    wk = ukv[..., :MLA_NOPE]
    wv = ukv[..., MLA_NOPE:]
    flat = lambda a: a.reshape(depth, a.shape[1], -1)
    bf = lambda a: a.astype(BF16)
    return dict(w_in=w_proj, w_gate=gates,
                wqa=flat(wqa), wk_t=jnp.swapaxes(flat(wk), 1, 2), wv=flat(wv),
                wb=bf(w_branch), wo=bf(w_out), wu=bf(w_up), wd=bf(w_down))


def kernel(x, positions, norm_mix_g, w_in, mla_q_norm_g, mla_w_uq, mla_kv_norm_g, mla_w_ukv, ret_norm_g, w_branch, w_out, norm_mlp_g, w_up, w_down, final_norm_g):
    b, s, d = x.shape
    depth = w_in.shape[0]
    t = b * s
    assert d == D_MODEL and s % CHUNK == 0
    tm_row = _pick(t, 512)
    t_mla = _pick(s, 1024)
    t_sb = max(_pick(s, 256), SB_SUB)
    w = _prep_weights(w_in, mla_w_uq, mla_w_ukv, w_branch, w_out, w_up, w_down)
    mla_cos, mla_sin, ret_cos, ret_sin = _rope_tables(positions)
    vec = lambda a: a.reshape(1, -1)
    gains = lambda a: a.reshape(depth, 1, -1)
    g_mix, g_q, g_kv, g_mlp = gains(norm_mix_g), gains(mla_q_norm_g), gains(mla_kv_norm_g), gains(norm_mlp_g)

    x2d = x.reshape(t, d)
    for l in range(depth):
        proj_ret, proj_bf, q_a, k_t, v_a = _in_proj(x2d, g_mix, w["w_in"], mla_cos, mla_sin, g_q, g_kv,
                                                    w["wqa"], w["wk_t"], w["wv"], l, b, tm_row)
        proj_bf = proj_bf.reshape(b, s, -1)

        y_a = _mla_attn(q_a.reshape(b, s, -1), k_t, v_a.reshape(b, s, -1), t_mla, 4)
        y_b = _sb_attn(proj_bf, t_sb, 8)
        y_c = _retention(proj_ret.reshape(b, s, -1), proj_bf, ret_cos, ret_sin, vec(ret_norm_g[l]),
                         _pick(s // CHUNK, 8))

        x2d = _merge(x2d, g_mix, y_a.reshape(t, -1), y_b.reshape(t, -1), y_c.reshape(t, -1),
                     w["w_gate"], w["wb"], w["wo"], l, tm_row)
        x2d = _mlp(x2d, g_mlp, w["wu"], w["wd"], vec(final_norm_g), l, l == depth - 1, tm_row)
    return x2d.reshape(b, s, d)
```
